```python
import jax, jax.numpy as jnp
from jax import lax
import numpy as np

D_MODEL = 1024
BATCH = 2
SEQ = 8192
DEPTH = 1
DEC_BATCH = 128
DEC_SEQ = 8
PAST_LEN = 8192
PAGE_SIZE = 128

POOL_WINDOWS = (2, 4, 8, 16)
POOL_GROUPS = len(POOL_WINDOWS)
POOL_WIDTH = D_MODEL // 2
POOL_CH = POOL_WIDTH // POOL_GROUPS
POOL_BUF = max(POOL_WINDOWS) - 1
N_HEADS = 8
N_KV_HEADS = 2
HEAD_DIM = 64
GQ = N_HEADS // N_KV_HEADS
ATTN_WIDTH = N_HEADS * HEAD_DIM
CMP_BLOCK = 32
CMP_STRIDE = 16
R_CMP = CMP_BLOCK // CMP_STRIDE
CMP_HID = 2 * HEAD_DIM
SEL_BLOCK = 64
SUB_PER_SEL = SEL_BLOCK // CMP_STRIDE
N_SEL = 16
N_LOCAL = 2
WINDOW = 512
Q_BLOCK = 128
FORCE_SCORE = 1e9
N_EXPERTS = 32
TOP_K = 4
D_FF = D_MODEL
SWIGLU_LIMIT = 7.0
SWIGLU_ALPHA = 1.702
MOE_BLOCK = 128
RMS_EPS = 1e-6
KV_COLS = 2 * N_KV_HEADS * HEAD_DIM
OFF_Q = POOL_WIDTH
OFF_KVC = OFF_Q + ATTN_WIDTH
OFF_KVS = OFF_KVC + KV_COLS
OFF_KVW = OFF_KVS + KV_COLS
OFF_NG = OFF_KVW + KV_COLS
OFF_MG = OFF_NG + 3 * N_HEADS
IN_COLS = OFF_MG + 2 * D_MODEL

kernel_name = 'hybrid_pool_nsa_moe_adaln_step'


def rmsnorm(x, g):
    xf = x.astype(jnp.float32)
    y = xf * lax.rsqrt(jnp.mean(xf * xf, axis=-1, keepdims=True) + RMS_EPS)
    return (y * g.astype(jnp.float32)).astype(x.dtype)


def masked_softmax(s, mask):
    s = jnp.where(mask, s, jnp.finfo(jnp.float32).min)
    m = jnp.max(s, axis=-1, keepdims=True)
    p = jnp.where(mask, jnp.exp(s - m), 0.0)
    return p / jnp.maximum(jnp.sum(p, axis=-1, keepdims=True), 1e-30)


def alibi_slopes():
    return jnp.exp2(-8.0 * jnp.arange(1, N_HEADS + 1, dtype=jnp.float32) / N_HEADS)


def ada_terms(c, w_ada, b_ada):
    m = (c @ w_ada + b_ada)[:, None, :]
    return jnp.split(m, 6, axis=-1)


def split_proj(proj):
    B, L = proj.shape[:2]
    u = proj[..., :OFF_Q]
    q = proj[..., OFF_Q:OFF_KVC].reshape(B, L, N_HEADS, HEAD_DIM)
    kvc = proj[..., OFF_KVC:OFF_KVS].reshape(B, L, 2, N_KV_HEADS, HEAD_DIM)
    kvs = proj[..., OFF_KVS:OFF_KVW].reshape(B, L, 2, N_KV_HEADS, HEAD_DIM)
    kvw = proj[..., OFF_KVW:OFF_NG].reshape(B, L, 2, N_KV_HEADS, HEAD_DIM)
    ng = proj[..., OFF_NG:OFF_MG].reshape(B, L, N_HEADS, 3)
    mg = proj[..., OFF_MG:IN_COLS].reshape(B, L, 2, D_MODEL)
    return u, q, kvc, kvs, kvw, ng, mg


def pool_mix(u_ext, p0, n_out, pool_w, pool_scale):
    B, L, P = u_ext.shape
    uf = u_ext.astype(jnp.float32)
    cs = jnp.concatenate([jnp.zeros((B, 1, P), jnp.float32), jnp.cumsum(uf, axis=1)], axis=1)
    i = jnp.arange(L - n_out, L)
    pos = p0 + i
    outs = []
    for gi, w in enumerate(POOL_WINDOWS):
        sl = slice(gi * POOL_CH, (gi + 1) * POOL_CH)
        lo = jnp.maximum(i + 1 - w, 0)
        cnt = jnp.minimum(pos + 1, w).astype(jnp.float32)
        outs.append((cs[:, i + 1, sl] - cs[:, lo, sl]) / cnt[None, :, None] - uf[:, i, sl])
    d = jnp.stack(outs, axis=2)
    y = jnp.einsum('bsgc,gcd->bsgd', d, pool_w.astype(jnp.float32)).reshape(B, n_out, P)
    return (y * pool_scale.astype(jnp.float32)).astype(u_ext.dtype)


def compress(kv, cmp_pe, cmp_w1, cmp_b1, cmp_w2):
    B, L = kv.shape[:2]
    n_sub = L // CMP_STRIDE
    nc = n_sub - R_CMP + 1
    sub = kv[:, :n_sub * CMP_STRIDE].reshape(B, n_sub, CMP_STRIDE, 2, N_KV_HEADS, HEAD_DIM).astype(jnp.float32)
    w1 = cmp_w1.astype(jnp.float32)
    w1r = w1.reshape(2, R_CMP, CMP_STRIDE, HEAD_DIM, CMP_HID)
    bias = jnp.einsum('sjd,sjdh->sh', cmp_pe.astype(jnp.float32), w1) + cmp_b1
    h = bias[None, None, :, None, :]
    for r in range(R_CMP):
        h = h + jnp.einsum('bnjskd,sjdh->bnskh', sub[:, r:r + nc], w1r[:, r])
    h = jax.nn.gelu(h)
    return jnp.einsum('bnskh,shd->bnskd', h, cmp_w2.astype(jnp.float32))


def sel_blocks(kv):
    B, L = kv.shape[:2]
    nblk = -(-L // SEL_BLOCK)
    kv = jnp.pad(kv, ((0, 0), (0, nblk * SEL_BLOCK - L), (0, 0), (0, 0), (0, 0)))
    kv = kv.reshape(B, nblk, SEL_BLOCK, 2, N_KV_HEADS, HEAD_DIM).transpose(3, 0, 4, 1, 2, 5)
    return kv[0], kv[1]


def nsa_queries(q, ng, t_pos, kc, vc, c_end, ks_blk, vs_blk, kw, vw, w_pos):
    B, Q = q.shape[:2]
    f32 = jnp.float32
    qg = q.astype(f32).reshape(B, Q, N_KV_HEADS, GQ, HEAD_DIM) * (HEAD_DIM ** -0.5)
    slopes = alibi_slopes().reshape(N_KV_HEADS, GQ)
    dist_c = (t_pos[:, None] - c_end[None, :]).astype(f32)
    s_c = jnp.einsum('bqkgd,bnkd->bqkgn', qg, kc.astype(f32)) - slopes[:, :, None] * dist_c[:, None, None, :]
    p_c = masked_softmax(s_c, (dist_c >= 0)[:, None, None, :])
    o_c = jnp.einsum('bqkgn,bnkd->bqkgd', p_c, vc.astype(f32))
    nc = kc.shape[1]
    nblk = ks_blk.shape[2]
    imp = jnp.sum(p_c, axis=3)
    a = jnp.pad(imp, ((0, 0), (0, 0), (0, 0), (0, nblk * SUB_PER_SEL - nc)))
    p_sub = a
    for rr in range(1, R_CMP):
        p_sub = p_sub + jnp.pad(a[..., :-rr], ((0, 0), (0, 0), (0, 0), (rr, 0)))
    p_slc = p_sub.reshape(B, Q, N_KV_HEADS, nblk, SUB_PER_SEL).sum(-1)
    blk = jnp.arange(nblk)[None, None, None, :]
    cur = (t_pos // SEL_BLOCK)[None, :, None, None]
    forced = (blk == 0) | (blk > cur - N_LOCAL)
    score = jnp.where(blk > cur, -jnp.inf, jnp.where(forced, FORCE_SCORE, p_slc))
    n_sel = min(N_SEL, nblk)
    _, idx = lax.top_k(score, n_sel)
    bi = jnp.arange(B)[:, None, None, None]
    ki = jnp.arange(N_KV_HEADS)[None, None, :, None]
    ks = ks_blk[bi, ki, idx].astype(f32)
    vs = vs_blk[bi, ki, idx].astype(f32)
    pos_s = idx[..., None] * SEL_BLOCK + jnp.arange(SEL_BLOCK)
    dist_s = (t_pos[None, :, None, None, None] - pos_s).astype(f32)
    s_s = jnp.einsum('bqkgd,bqknld->bqkgnl', qg, ks) - slopes[None, None, :, :, None, None] * dist_s[:, :, :, None]
    n_keys = n_sel * SEL_BLOCK
    p_s = masked_softmax(s_s.reshape(B, Q, N_KV_HEADS, GQ, n_keys),
                         (dist_s >= 0).reshape(B, Q, N_KV_HEADS, 1, n_keys)).reshape(s_s.shape)
    o_s = jnp.einsum('bqkgnl,bqknld->bqkgd', p_s, vs)
    dist_w = t_pos[:, None] - w_pos[None, :]
    mask_w = (dist_w >= 0) & (dist_w <= WINDOW) & (w_pos[None, :] >= 0)
    s_w = jnp.einsum('bqkgd,bnkd->bqkgn', qg, kw.astype(f32)) - slopes[:, :, None] * dist_w.astype(f32)[:, None, None, :]
    p_w = masked_softmax(s_w, mask_w[:, None, None, :])
    o_w = jnp.einsum('bqkgn,bnkd->bqkgd', p_w, vw.astype(f32))
    g = jax.nn.sigmoid(ng.astype(f32)).reshape(B, Q, N_KV_HEADS, GQ, 3)
    o = g[..., 0:1] * o_c + g[..., 1:2] * o_s + g[..., 2:3] * o_w
    return o.reshape(B, Q, ATTN_WIDTH).astype(q.dtype)


def nsa_prompt(q, ng, kvc, kvs, kvw, cmp_pe, cmp_w1, cmp_b1, cmp_w2):
    B, S = q.shape[:2]
    ckv = compress(kvc, cmp_pe, cmp_w1, cmp_b1, cmp_w2)
    c_end = jnp.arange(ckv.shape[1]) * CMP_STRIDE + CMP_BLOCK - 1
    ks_blk, vs_blk = sel_blocks(kvs)
    kw_pad = jnp.pad(kvw, ((0, 0), (WINDOW, 0), (0, 0), (0, 0), (0, 0)))

    def one_block(i):
        q0 = i * Q_BLOCK
        q_i = lax.dynamic_slice_in_dim(q, q0, Q_BLOCK, axis=1)
        g_i = lax.dynamic_slice_in_dim(ng, q0, Q_BLOCK, axis=1)
        w_i = lax.dynamic_slice_in_dim(kw_pad, q0, WINDOW + Q_BLOCK, axis=1)
        t_pos = q0 + jnp.arange(Q_BLOCK)
        w_pos = q0 - WINDOW + jnp.arange(WINDOW + Q_BLOCK)
        return nsa_queries(q_i, g_i, t_pos, ckv[:, :, 0], ckv[:, :, 1], c_end, ks_blk, vs_blk,
                           w_i[:, :, 0], w_i[:, :, 1], w_pos)

    o = lax.map(one_block, jnp.arange(S // Q_BLOCK))
    return o.transpose(1, 0, 2, 3).reshape(B, S, ATTN_WIDTH)


def moe(h, router_w, router_b, w_gu, b_gu, w_down, b_down):
    T, D = h.shape
    logits = (h @ router_w + router_b).astype(jnp.float32)
    top_v, top_i = lax.top_k(logits, TOP_K)
    gates = jax.nn.softmax(top_v, axis=-1)
    n = T * TOP_K
    e_flat = top_i.reshape(n)
    tok_flat = jnp.arange(n, dtype=jnp.int32) // TOP_K
    order = jnp.argsort(e_flat)
    e_sorted = e_flat[order]
    counts = jnp.bincount(e_flat, length=N_EXPERTS)
    padded = (counts + MOE_BLOCK - 1) // MOE_BLOCK * MOE_BLOCK
    pend = jnp.cumsum(padded)
    pstart = pend - padded
    start = jnp.cumsum(counts) - counts
    dest = pstart[e_sorted] + jnp.arange(n) - start[e_sorted]
    n_blocks = -(-n // MOE_BLOCK) + N_EXPERTS
    rows = n_blocks * MOE_BLOCK
    tok_buf = jnp.full((rows,), T, jnp.int32).at[dest].set(tok_flat[order])
    w_buf = jnp.zeros((rows,), jnp.float32).at[dest].set(gates.reshape(n)[order])
    blk_e = jnp.minimum(jnp.searchsorted(pend, jnp.arange(n_blocks) * MOE_BLOCK, side='right'), N_EXPERTS - 1)
    xb = jnp.concatenate([h, jnp.zeros((1, D), h.dtype)], axis=0)[tok_buf].reshape(n_blocks, MOE_BLOCK, D)

    def expert_block(args):
        xe, e = args
        gu = xe @ w_gu[e] + b_gu[e]
        gate = jnp.minimum(gu[:, :D_FF], SWIGLU_LIMIT)
        up = jnp.clip(gu[:, D_FF:], -SWIGLU_LIMIT, SWIGLU_LIMIT)
        act = (up + 1.0) * gate * jax.nn.sigmoid(SWIGLU_ALPHA * gate)
        return act @ w_down[e] + b_down[e]

    yb = lax.map(expert_block, (xb, blk_e))
    y = jax.ops.segment_sum(yb.reshape(rows, D).astype(jnp.float32) * w_buf[:, None], tok_buf, num_segments=T + 1)
    return y[:T].astype(h.dtype)


def mix_inputs(x, c, lw):
    sh1, sc1, gt1, sh2, sc2, gt2 = ada_terms(c, lw['w_ada'], lw['b_ada'])
    h = rmsnorm(x, lw['g_pre1']) * (1.0 + sc1) + sh1
    return split_proj(h @ lw['w_in']), (gt1, sh2, sc2, gt2)


def finish_layer(x, pool_y, attn_y, mg, mods, lw):
    gt1, sh2, sc2, gt2 = mods
    gates = jax.nn.sigmoid(mg.astype(jnp.float32)).astype(x.dtype)
    merged = gates[..., 0, :] * (pool_y @ lw['w_br_pool']) + gates[..., 1, :] * (attn_y @ lw['w_br_attn'])
    mix = merged @ lw['w_o']
    x = x + gt1 * rmsnorm(mix, lw['g_post1'])
    h = rmsnorm(x, lw['g_pre2']) * (1.0 + sc2) + sh2
    B, L, D = h.shape
    f = moe(h.reshape(B * L, D), lw['router_w'], lw['router_b'], lw['w_gu'], lw['b_gu'],
            lw['w_down'], lw['b_down']).reshape(B, L, D)
    return x + gt2 * rmsnorm(f, lw['g_post2'])


def prompt_layer(x, c, lw):
    (u, q, kvc, kvs, kvw, ng, mg), mods = mix_inputs(x, c, lw)
    S = x.shape[1]
    pool_y = pool_mix(u, 0, S, lw['pool_w'], lw['pool_scale'])
    attn_y = nsa_prompt(q, ng, kvc, kvs, kvw, lw['cmp_pe'], lw['cmp_w1'], lw['cmp_b1'], lw['cmp_w2'])
    y = finish_layer(x, pool_y, attn_y, mg, mods, lw)
    return y, kvc, kvs, kvw[:, -min(WINDOW, S):], u[:, -min(POOL_BUF, S):]


def sample_layer(x, c, l, cache_kv_cmp, cache_kv_sel, state_kv_win, state_pool, page_table, lw):
    (u, q, kvc, kvs, kvw, ng, mg), mods = mix_inputs(x, c, lw)
    Bd, Q = x.shape[:2]
    p_buf = state_pool.shape[2]
    u_ext = jnp.concatenate([state_pool[l].astype(u.dtype), u], axis=1)
    pool_y = pool_mix(u_ext, PAST_LEN - p_buf, Q, lw['pool_w'], lw['pool_scale'])
    past_c = cache_kv_cmp[l, page_table].reshape(Bd, PAST_LEN, 2, N_KV_HEADS, HEAD_DIM)
    past_s = cache_kv_sel[l, page_table].reshape(Bd, PAST_LEN, 2, N_KV_HEADS, HEAD_DIM)
    full_c = jnp.concatenate([past_c.astype(kvc.dtype), kvc], axis=1)
    full_s = jnp.concatenate([past_s.astype(kvs.dtype), kvs], axis=1)
    ckv = compress(full_c, lw['cmp_pe'], lw['cmp_w1'], lw['cmp_b1'], lw['cmp_w2'])
    c_end = jnp.arange(ckv.shape[1]) * CMP_STRIDE + CMP_BLOCK - 1
    ks_blk, vs_blk = sel_blocks(full_s)
    w_buf = state_kv_win.shape[2]
    w_ctx = jnp.concatenate([state_kv_win[l].astype(kvw.dtype), kvw], axis=1)
    t_pos = PAST_LEN + jnp.arange(Q)
    w_pos = PAST_LEN - w_buf + jnp.arange(w_buf + Q)
    attn_y = nsa_queries(q, ng, t_pos, ckv[:, :, 0], ckv[:, :, 1], c_end, ks_blk, vs_blk,
                         w_ctx[:, :, 0], w_ctx[:, :, 1], w_pos)
    y = finish_layer(x, pool_y, attn_y, mg, mods, lw)
    return y, kvc, kvs, w_ctx[:, -w_buf:], u_ext[:, -p_buf:]


def setup_inputs(seed: int = 0) -> dict:
    key = jax.random.key(seed)
    keys = iter(jax.random.split(key, 40))

    def nrm(shape, scale):
        return jax.random.normal(next(keys), shape, jnp.float32) * scale

    n_pages = PAST_LEN // PAGE_SIZE
    n_used = DEC_BATCH * n_pages
    n_pool = (5 * n_used + 3) // 4
    page_table = jax.random.permutation(next(keys), n_pool)[:n_used].reshape(DEC_BATCH, n_pages).astype(jnp.int32)
    kv_row = (2, N_KV_HEADS, HEAD_DIM)
    w_buf = min(WINDOW, PAST_LEN)
    p_buf = min(POOL_BUF, PAST_LEN)
    return {
        'x_prompt': nrm((BATCH, SEQ, D_MODEL), 1.0),
        'x_sample': nrm((DEC_BATCH, DEC_SEQ, D_MODEL), 1.0),
        'cache_kv_cmp': nrm((DEPTH, n_pool, PAGE_SIZE) + kv_row, 1.0),
        'cache_kv_sel': nrm((DEPTH, n_pool, PAGE_SIZE) + kv_row, 1.0),
        'state_kv_win': nrm((DEPTH, DEC_BATCH, w_buf) + kv_row, 1.0),
        'state_pool': nrm((DEPTH, DEC_BATCH, p_buf, POOL_WIDTH), 1.0),
        'page_table': page_table,
        'c_prompt': nrm((BATCH, D_MODEL), 1.0),
        'c_sample': nrm((DEC_BATCH, D_MODEL), 1.0),
        'w_ada': nrm((DEPTH, D_MODEL, 6 * D_MODEL), 0.3 * D_MODEL ** -0.5),
        'b_ada': nrm((DEPTH, 6 * D_MODEL), 0.1),
        'g_pre1': 1.0 + nrm((DEPTH, D_MODEL), 0.1),
        'g_post1': 1.0 + nrm((DEPTH, D_MODEL), 0.1),
        'g_pre2': 1.0 + nrm((DEPTH, D_MODEL), 0.1),
        'g_post2': 1.0 + nrm((DEPTH, D_MODEL), 0.1),
        'w_in': nrm((DEPTH, D_MODEL, IN_COLS), D_MODEL ** -0.5),
        'pool_w': nrm((DEPTH, POOL_GROUPS, POOL_CH, POOL_CH), POOL_CH ** -0.5),
        'pool_scale': 1.0 + nrm((DEPTH, POOL_WIDTH), 0.1),
        'cmp_pe': nrm((DEPTH, 2, CMP_BLOCK, HEAD_DIM), 0.1),
        'cmp_w1': nrm((DEPTH, 2, CMP_BLOCK, HEAD_DIM, CMP_HID), (CMP_BLOCK * HEAD_DIM) ** -0.5),
        'cmp_b1': nrm((DEPTH, 2, CMP_HID), 0.02),
        'cmp_w2': nrm((DEPTH, 2, CMP_HID, HEAD_DIM), CMP_HID ** -0.5),
        'w_br_pool': nrm((DEPTH, POOL_WIDTH, D_MODEL), POOL_WIDTH ** -0.5),
        'w_br_attn': nrm((DEPTH, ATTN_WIDTH, D_MODEL), ATTN_WIDTH ** -0.5),
        'w_o': nrm((DEPTH, D_MODEL, D_MODEL), D_MODEL ** -0.5),
        'router_w': nrm((DEPTH, D_MODEL, N_EXPERTS), D_MODEL ** -0.5),
        'router_b': nrm((DEPTH, N_EXPERTS), 0.01),
        'w_gu': nrm((DEPTH, N_EXPERTS, D_MODEL, 2 * D_FF), D_MODEL ** -0.5),
        'b_gu': nrm((DEPTH, N_EXPERTS, 2 * D_FF), 0.02),
        'w_down': nrm((DEPTH, N_EXPERTS, D_FF, D_MODEL), D_FF ** -0.5),
        'b_down': nrm((DEPTH, N_EXPERTS, D_MODEL), 0.02),
    }


def reference(x_prompt, x_sample, cache_kv_cmp, cache_kv_sel, state_kv_win, state_pool, page_table,
              c_prompt, c_sample, w_ada, b_ada, g_pre1, g_post1, g_pre2, g_post2, w_in, pool_w, pool_scale,
              cmp_pe, cmp_w1, cmp_b1, cmp_w2, w_br_pool, w_br_attn, w_o, router_w, router_b,
              w_gu, b_gu, w_down, b_down):
    xp, xs = x_prompt, x_sample
    kvc_p, kvs_p, kvw_p, pool_p = [], [], [], []
    kvc_s, kvs_s, kvw_s, pool_s = [], [], [], []
    for l in range(DEPTH):
        lw = {
            'w_ada': w_ada[l], 'b_ada': b_ada[l], 'g_pre1': g_pre1[l], 'g_post1': g_post1[l],
            'g_pre2': g_pre2[l], 'g_post2': g_post2[l], 'w_in': w_in[l], 'pool_w': pool_w[l],
            'pool_scale': pool_scale[l], 'cmp_pe': cmp_pe[l], 'cmp_w1': cmp_w1[l], 'cmp_b1': cmp_b1[l],
            'cmp_w2': cmp_w2[l], 'w_br_pool': w_br_pool[l], 'w_br_attn': w_br_attn[l], 'w_o': w_o[l],
            'router_w': router_w[l], 'router_b': router_b[l], 'w_gu': w_gu[l], 'b_gu': b_gu[l],
            'w_down': w_down[l], 'b_down': b_down[l],
        }
        xp, a1, a2, a3, a4 = prompt_layer(xp, c_prompt, lw)
        kvc_p.append(a1); kvs_p.append(a2); kvw_p.append(a3); pool_p.append(a4)
        xs, b1, b2, b3, b4 = sample_layer(xs, c_sample, l, cache_kv_cmp, cache_kv_sel, state_kv_win,
                                          state_pool, page_table, lw)
        kvc_s.append(b1); kvs_s.append(b2); kvw_s.append(b3); pool_s.append(b4)
    return (xp, xs, jnp.stack(kvc_p), jnp.stack(kvs_p), jnp.stack(kvw_p), jnp.stack(pool_p),
            jnp.stack(kvc_s), jnp.stack(kvs_s), jnp.stack(kvw_s), jnp.stack(pool_s))
```

```python
import functools

import jax
import jax.numpy as jnp
import numpy as np
from jax import lax
from jax.experimental import pallas as pl
from jax.experimental.pallas import tpu as pltpu

F32 = jnp.float32
BF16 = jnp.bfloat16

POOL_WINDOWS = (2, 4, 8, 16)
N_HEADS = 8
N_KV_HEADS = 2
HEAD_DIM = 64
GQ = N_HEADS // N_KV_HEADS
CMP_BLOCK = 32
CMP_STRIDE = 16
SEL_BLOCK = 64
SUB_PER_SEL = SEL_BLOCK // CMP_STRIDE
N_SEL = 16
N_LOCAL = 2
WINDOW = 512
TOP_K = 4
SWIGLU_LIMIT = 7.0
SWIGLU_ALPHA = 1.702
RMS_EPS = 1e-6

LANES = 128
SUBLANES = 8
VMEM_LIMIT = 56 * 1024 * 1024
NEG_BIG = -1e30
FAR = 1 << 30
KV_ROWS = 2 * N_KV_HEADS * HEAD_DIM
HALF = N_KV_HEADS * HEAD_DIM
POOL_HIST = 16
MOE_ROWS = 256


def _cparams(*sem):
    return pltpu.CompilerParams(dimension_semantics=sem, vmem_limit_bytes=VMEM_LIMIT)


def _full_spec(w):
    nd = w.ndim
    return pl.BlockSpec(w.shape, lambda *a: (0,) * nd)


def _dot(a, b):
    return jnp.dot(a, b, preferred_element_type=F32)


def _dot_nt(a, b):
    return lax.dot_general(a, b, (((1,), (1,)), ((), ())), preferred_element_type=F32)


def _split3(x):
    hi = x.astype(BF16)
    r1 = x - hi.astype(F32)
    mid = r1.astype(BF16)
    lo = (r1 - mid.astype(F32)).astype(BF16)
    return hi, mid, lo


def _dot_exact_rhs(x, m_bf16):
    hi, mid, lo = _split3(x)
    return _dot(hi, m_bf16) + _dot(mid, m_bf16) + _dot(lo, m_bf16)


def _rms(x, g):
    return x * lax.rsqrt(jnp.mean(x * x, axis=-1, keepdims=True) + RMS_EPS) * g


def _ada_kernel(c_ref, w_ref, b_ref, o_ref):
    o_ref[...] = _dot(c_ref[...].astype(BF16), w_ref[...].astype(BF16)) + b_ref[...]


def _ada(c_all, w_ada, b_ada):
    m, d = c_all.shape
    n = w_ada.shape[1]
    tn = 1024
    return pl.pallas_call(
        _ada_kernel,
        grid=(n // tn,),
        in_specs=[pl.BlockSpec((m, d), lambda j: (0, 0)),
                  pl.BlockSpec((d, tn), lambda j: (0, j)),
                  pl.BlockSpec((1, tn), lambda j: (0, j))],
        out_specs=pl.BlockSpec((m, tn), lambda j: (0, j)),
        out_shape=jax.ShapeDtypeStruct((m, n), F32),
        compiler_params=_cparams("arbitrary"),
        name="ada",
    )(c_all, w_ada, b_ada)


def _proj_kernel(x_ref, sc_ref, sh_ref, g_ref, wa_ref, wkv_ref,
                 u_ref, q_ref, mg_ref, ng_ref, kvt_ref, *, pw, aw, dm):
    hb = (_rms(x_ref[0], g_ref[...]) * (1.0 + sc_ref[0]) + sh_ref[0]).astype(BF16)
    u_ref[0] = _dot(hb, wa_ref[:, 0:pw])
    q_ref[0] = _dot(hb, wa_ref[:, pw:pw + aw])
    o = pw + aw
    mg_ref[0] = jax.nn.sigmoid(_dot(hb, wa_ref[:, o:o + 2 * dm]))
    ng_ref[0] = jax.nn.sigmoid(_dot(hb, wa_ref[:, o + 2 * dm:o + 2 * dm + LANES]))
    kvt_ref[0] = _dot_nt(wkv_ref[...], hb)


def _proj(x3, sc, sh, mod_map, g_pre1, wa, wkvt, tm, pw, aw):
    nb, s, dm = x3.shape
    nt = s // tm
    kern = functools.partial(_proj_kernel, pw=pw, aw=aw, dm=dm)
    mod_block = (1,) + sc.shape[1:]
    row = lambda b, i: (b, i, 0)
    return pl.pallas_call(
        kern,
        grid=(nb, nt),
        in_specs=[pl.BlockSpec((1, tm, dm), row),
                  pl.BlockSpec(mod_block, mod_map),
                  pl.BlockSpec(mod_block, mod_map),
                  _full_spec(g_pre1), _full_spec(wa), _full_spec(wkvt)],
        out_specs=[pl.BlockSpec((1, tm, pw), row),
                   pl.BlockSpec((1, tm, aw), row),
                   pl.BlockSpec((1, tm, 2 * dm), row),
                   pl.BlockSpec((1, tm, LANES), row),
                   pl.BlockSpec((1, 3 * KV_ROWS, tm), lambda b, i: (b, 0, i))],
        out_shape=[jax.ShapeDtypeStruct((nb, s, pw), F32),
                   jax.ShapeDtypeStruct((nb, s, aw), F32),
                   jax.ShapeDtypeStruct((nb, s, 2 * dm), F32),
                   jax.ShapeDtypeStruct((nb, s, LANES), F32),
                   jax.ShapeDtypeStruct((nb, 3 * KV_ROWS, s), F32)],
        compiler_params=_cparams("arbitrary", "arbitrary"),
        name="proj",
    )(x3, sc, sh, g_pre1, wa, wkvt)


def _prep_in_weights(w_in):
    dm = w_in.shape[0]
    pw = dm // 2
    aw = N_HEADS * HEAD_DIM
    off_kvc = pw + aw
    off_ng = off_kvc + 3 * KV_ROWS
    off_mg = off_ng + 3 * N_HEADS
    order = np.array([k * GQ + g for g in range(GQ) for k in range(N_KV_HEADS)])
    wq = w_in[:, pw:off_kvc].reshape(dm, N_HEADS, HEAD_DIM)[:, order].reshape(dm, aw) * (HEAD_DIM ** -0.5)
    wng = jnp.pad(w_in[:, off_ng:off_mg], ((0, 0), (0, LANES - 3 * N_HEADS)))
    wa = jnp.concatenate([w_in[:, :pw], wq, w_in[:, off_mg:], wng], axis=1).astype(BF16)
    wkvt = w_in[:, off_kvc:off_ng].T.astype(BF16)
    return wa, wkvt, pw, aw, order


def _compress_core(get_page, n_pages, perm_ref, wp_ref, w2t_ref, pe_ref, w1f_ref, b1_ref, s2_ref, out_ref):
    sub_pp = LANES // CMP_STRIDE
    n_sub = n_pages * sub_pp
    r_cmp = CMP_BLOCK // CMP_STRIDE
    hid = w1f_ref.shape[-1]

    def page_body(p, carry):
        pg = get_page(p).astype(BF16)
        t = _dot_nt(perm_ref[...], pg)
        row0 = pl.multiple_of(p * sub_pp, sub_pp)
        for s in range(2):
            for j in range(CMP_STRIDE):
                s2_ref[s, j // 2, pl.ds(row0, sub_pp), (j % 2) * HALF:(j % 2 + 1) * HALF] = (
                    t[sub_pp * j:sub_pp * (j + 1), s * HALF:(s + 1) * HALF])
        return carry

    lax.fori_loop(0, n_pages, page_body, 0)
    for s in range(2):
        acc = None
        for jp in range(CMP_STRIDE // 2):
            d = _dot(s2_ref[s, jp].astype(BF16), wp_ref[s, jp])
            acc = d if acc is None else acc + d
        bias = _dot(pe_ref[s], w1f_ref[s])[0:1] + b1_ref[s]
        bias = jnp.concatenate([bias] * N_KV_HEADS, axis=1)
        w = N_KV_HEADS * hid
        hpre = acc[:, :w] + bias
        for r in range(1, r_cmp):
            hpre = hpre + pltpu.roll(acc[:, r * w:(r + 1) * w], n_sub - r, axis=0)
        g = jax.nn.gelu(hpre).astype(BF16)
        out_ref[s * HALF:(s + 1) * HALF, :] = _dot_nt(w2t_ref[s], g)


def _compress_prompt_kernel(kv_ref, perm_ref, wp_ref, w2t_ref, pe_ref, w1f_ref, b1_ref, out_ref, s2_ref, *, n_pages):
    def get_page(p):
        return kv_ref[0, :, pl.ds(pl.multiple_of(p * LANES, LANES), LANES)]
    _compress_core(get_page, n_pages, perm_ref, wp_ref, w2t_ref, pe_ref, w1f_ref, b1_ref, s2_ref, out_ref.at[0])


def _page_fetch(pt_ref, cache_hbm, buf, sem, b, slot, n_pages):
    def start(p, c):
        pltpu.make_async_copy(cache_hbm.at[pt_ref[b * n_pages + p]], buf.at[slot, p], sem.at[slot]).start()
        return c
    lax.fori_loop(0, n_pages, start, 0)


def _page_wait(cache_hbm, buf, sem, slot, n_pages):
    def wait(p, c):
        pltpu.make_async_copy(cache_hbm.at[0], buf.at[slot, p], sem.at[slot]).wait()
        return c
    lax.fori_loop(0, n_pages, wait, 0)


def _paged_prologue(pt_ref, cache_hbm, buf, sem, n_pages):
    b = pl.program_id(0)
    slot = b % 2

    @pl.when(b == 0)
    def _():
        _page_fetch(pt_ref, cache_hbm, buf, sem, 0, 0, n_pages)

    @pl.when(b + 1 < pl.num_programs(0))
    def _():
        _page_fetch(pt_ref, cache_hbm, buf, sem, b + 1, 1 - slot, n_pages)

    _page_wait(cache_hbm, buf, sem, slot, n_pages)
    return slot


def _compress_paged_kernel(pt_ref, cache_hbm, perm_ref, wp_ref, w2t_ref, pe_ref, w1f_ref, b1_ref, out_ref,
                           buf, sem, s2_ref, *, n_pages):
    slot = _paged_prologue(pt_ref, cache_hbm, buf, sem, n_pages)
    _compress_core(lambda p: buf[slot, p], n_pages, perm_ref, wp_ref, w2t_ref, pe_ref, w1f_ref, b1_ref,
                   s2_ref, out_ref.at[0])


def _prep_cmp_weights(cmp_pe, cmp_w1, cmp_b1, cmp_w2):
    hid = cmp_w1.shape[-1]
    r_cmp = CMP_BLOCK // CMP_STRIDE
    eye = jnp.eye(N_KV_HEADS, dtype=F32)
    w1r = cmp_w1.reshape(2, r_cmp, CMP_STRIDE // 2, 2, HEAD_DIM, hid)
    wp = jnp.einsum('srpjdh,kc->spjkdrch', w1r, eye).reshape(
        2, CMP_STRIDE // 2, 2 * HALF, r_cmp * N_KV_HEADS * hid).astype(BF16)
    w2t = jnp.einsum('shd,kc->skdch', cmp_w2, eye).reshape(2, HALF, N_KV_HEADS * hid).astype(BF16)
    pe = jnp.broadcast_to(cmp_pe.reshape(2, 1, CMP_BLOCK * HEAD_DIM),
                          (2, SUBLANES, CMP_BLOCK * HEAD_DIM)).astype(BF16)
    w1f = cmp_w1.reshape(2, CMP_BLOCK * HEAD_DIM, hid).astype(BF16)
    b1 = cmp_b1.reshape(2, 1, hid)
    sub_pp = LANES // CMP_STRIDE
    x = np.arange(LANES)
    perm = np.zeros((LANES, LANES), np.float32)
    perm[x, (x % sub_pp) * CMP_STRIDE + x // sub_pp] = 1.0
    return (jnp.asarray(perm, BF16), wp, w2t, pe, w1f, b1)


def _compress_prompt(kvt, cw):
    nb, _, s = kvt.shape
    n_pages = s // LANES
    n_sub = s // CMP_STRIDE
    return pl.pallas_call(
        functools.partial(_compress_prompt_kernel, n_pages=n_pages),
        grid=(nb,),
        in_specs=[pl.BlockSpec((1, KV_ROWS, s), lambda b: (b, 0, 0))] + [_full_spec(w) for w in cw],
        out_specs=pl.BlockSpec((1, KV_ROWS, n_sub), lambda b: (b, 0, 0)),
        out_shape=jax.ShapeDtypeStruct((nb, KV_ROWS, n_sub), F32),
        scratch_shapes=[pltpu.VMEM((2, CMP_STRIDE // 2, n_sub, 2 * HALF), F32)],
        compiler_params=_cparams("arbitrary"),
        name="compress_prompt",
    )(kvt, *cw)


def _compress_paged(page_tab, n_pages, cache_t, cw):
    nb = page_tab.shape[0] // n_pages
    n_sub = n_pages * (LANES // CMP_STRIDE)
    return pl.pallas_call(
        functools.partial(_compress_paged_kernel, n_pages=n_pages),
        grid_spec=pltpu.PrefetchScalarGridSpec(
            num_scalar_prefetch=1,
            grid=(nb,),
            in_specs=[pl.BlockSpec(memory_space=pl.ANY)] + [_full_spec(w) for w in cw],
            out_specs=pl.BlockSpec((1, KV_ROWS, n_sub), lambda b, pt: (b, 0, 0)),
            scratch_shapes=[pltpu.VMEM((2, n_pages, KV_ROWS, LANES), F32),
                            pltpu.SemaphoreType.DMA((2,)),
                            pltpu.VMEM((2, CMP_STRIDE // 2, n_sub, 2 * HALF), F32)]),
        out_shape=jax.ShapeDtypeStruct((nb, KV_ROWS, n_sub), F32),
        compiler_params=_cparams("arbitrary"),
        name="compress_paged",
    )(page_tab, cache_t, *cw)


def _stack_heads(fn):
    return jnp.concatenate([fn(k, g) for k in range(N_KV_HEADS) for g in range(GQ)], axis=0)


def _alibi_slope(k, g):
    return 2.0 ** (-8.0 * (k * GQ + g + 1) / N_HEADS)


def _nsa_block(q, sgate, t0, ckvt_ref, mmat_ref, bmat_ref, sel_tile, n_sel_tiles, sel_tk, extra_sel, win_tiles,
               m_ref, l_ref, acc_ref, out_ref, *, qt, nblk_pad, n_pick):
    rows = N_HEADS * qt
    t_pos = t0 + lax.broadcasted_iota(jnp.int32, (qt, 1), 0)
    lane_half = lax.broadcasted_iota(jnp.int32, (qt, LANES), 1) // HEAD_DIM

    qp = _stack_heads(lambda k, g: jnp.where(lane_half == k, q[:, g * LANES:(g + 1) * LANES], 0.0)).astype(BF16)

    def alibi(distf):
        return _stack_heads(lambda k, g: _alibi_slope(k, g) * distf)

    def stack_masks(mk):
        return jnp.concatenate([mk[k] for k in range(N_KV_HEADS) for _ in range(GQ)], axis=0) > 0.5

    nc = ckvt_ref.shape[-1]
    c_end = lax.broadcasted_iota(jnp.int32, (1, nc), 1) * CMP_STRIDE + (CMP_BLOCK - 1)
    dist_c = t_pos - c_end
    valid_c = jnp.where(dist_c >= 0, 1.0, 0.0)
    mask_c = stack_masks([valid_c] * N_KV_HEADS)
    s = _dot(qp, ckvt_ref[0:HALF, :].astype(BF16)) - alibi(dist_c.astype(F32))
    s = jnp.where(mask_c, s, NEG_BIG)
    p = jnp.where(mask_c, jnp.exp(s - jnp.max(s, axis=1, keepdims=True)), 0.0)
    p = p / jnp.maximum(jnp.sum(p, axis=1, keepdims=True), 1e-30)
    o_c = _dot_nt(p.astype(BF16), ckvt_ref[HALF:2 * HALF, :].astype(BF16))

    blk = lax.broadcasted_iota(jnp.int32, (qt, nblk_pad), 1)
    cur = t_pos // SEL_BLOCK
    forced = (blk == 0) | ((blk > cur - N_LOCAL) & (blk <= cur))
    free = (blk >= 1) & (blk <= cur - N_LOCAL)
    sel = []
    for k in range(N_KV_HEADS):
        imp = p[k * GQ * qt:(k * GQ + 1) * qt]
        for g in range(1, GQ):
            imp = imp + p[(k * GQ + g) * qt:(k * GQ + g + 1) * qt]
        score = jnp.where(free, _dot_exact_rhs(imp, mmat_ref[...]), -jnp.inf)
        chosen = jnp.where(forced, 1.0, 0.0)
        for _ in range(n_pick):
            best = jnp.max(score, axis=1, keepdims=True)
            idx = jnp.min(jnp.where(score == best, blk, nblk_pad), axis=1, keepdims=True)
            hit = blk == idx
            chosen = jnp.where(hit, 1.0, chosen)
            score = jnp.where(hit, -jnp.inf, score)
        sel.append(chosen.astype(BF16))

    def reset():
        m_ref[...] = jnp.full((rows, 1), NEG_BIG, F32)
        l_ref[...] = jnp.zeros((rows, 1), F32)
        acc_ref[...] = jnp.zeros((rows, HALF), F32)

    def flash(kt, vt, dist, valid):
        mask = stack_masks(valid)
        sc = _dot(qp, kt) - alibi(dist.astype(F32))
        sc = jnp.where(mask, sc, NEG_BIG)
        m_old = m_ref[...]
        m_new = jnp.maximum(m_old, jnp.max(sc, axis=1, keepdims=True))
        alpha = jnp.exp(m_old - m_new)
        pr = jnp.where(mask, jnp.exp(sc - m_new), 0.0)
        l_ref[...] = alpha * l_ref[...] + jnp.sum(pr, axis=1, keepdims=True)
        acc_ref[...] = alpha * acc_ref[...] + _dot_nt(pr.astype(BF16), vt)
        m_ref[...] = m_new

    def result():
        return acc_ref[...] / jnp.maximum(l_ref[...], 1e-30)

    def sel_step(kt, vt, pos0, tk):
        pos = pos0 + lax.broadcasted_iota(jnp.int32, (1, tk), 1)
        dist = t_pos - pos
        expand = jnp.where(lax.broadcasted_iota(jnp.int32, (nblk_pad, 1), 0) == pos // SEL_BLOCK,
                           1.0, 0.0).astype(BF16)
        valid = [jnp.where(dist >= 0, _dot(sel[k], expand), 0.0) for k in range(N_KV_HEADS)]
        flash(kt, vt, dist, valid)

    reset()

    def sel_body(j, c):
        kt, vt = sel_tile(j)
        sel_step(kt, vt, j * sel_tk, sel_tk)
        return c

    lax.fori_loop(0, n_sel_tiles, sel_body, 0)
    if extra_sel is not None:
        kt, vt, pos0 = extra_sel
        sel_step(kt, vt, pos0, kt.shape[-1])
    o_s = result()

    reset()
    for kt, vt, pos0 in win_tiles:
        pos = pos0 + lax.broadcasted_iota(jnp.int32, (1, kt.shape[-1]), 1)
        dist = t_pos - pos
        valid = jnp.where((dist >= 0) & (dist <= WINDOW), 1.0, 0.0)
        flash(kt, vt, dist, [valid] * N_KV_HEADS)
    o_w = result()

    g3 = _split3(sgate)
    for g in range(GQ):
        slab = None
        for c, o in enumerate((o_c, o_s, o_w)):
            bm = bmat_ref[g * 3 + c]
            gate = _dot(g3[0], bm) + _dot(g3[1], bm) + _dot(g3[2], bm)
            val = jnp.where(lane_half == 0, o[g * qt:(g + 1) * qt], o[(GQ + g) * qt:(GQ + g + 1) * qt])
            slab = gate * val if slab is None else slab + gate * val
        out_ref[:, g * LANES:(g + 1) * LANES] = slab


def _nsa_consts(nc, nblk_pad):
    j = np.arange(nc)[:, None]
    lo = SUB_PER_SEL * np.arange(nblk_pad)[None, :]
    mm = (np.where((j >= lo) & (j < lo + SUB_PER_SEL - 1), 2.0, 0.0)
          + np.where((j == lo - 1) | (j == lo + SUB_PER_SEL - 1), 1.0, 0.0))
    col = np.arange(LANES)[:, None]
    lane = np.arange(LANES)[None, :]
    bm = np.stack([(col == ((lane // HEAD_DIM) * GQ + g) * 3 + c)
                   for g in range(GQ) for c in range(3)]).astype(np.float32)
    return jnp.asarray(mm, BF16), jnp.asarray(bm, BF16)


def _kv_halves(x):
    return x[0:HALF].astype(BF16), x[HALF:2 * HALF].astype(BF16)


def _nsa_prompt_kernel(*refs, qt, n_win, nblk_pad, n_pick):
    q_ref, ng_ref, ckvt_ref, kvs_ref = refs[:4]
    win_refs = refs[4:4 + n_win]
    mmat_ref, bmat_ref, out_ref, m_ref, l_ref, acc_ref = refs[4 + n_win:]
    i = pl.program_id(1)

    def sel_tile(j):
        off = pl.multiple_of(j * qt, LANES)
        return (kvs_ref[0, 0:HALF, pl.ds(off, qt)].astype(BF16),
                kvs_ref[0, HALF:2 * HALF, pl.ds(off, qt)].astype(BF16))

    win = []
    for jj, w in enumerate(win_refs):
        bi = i - (n_win - 1) + jj
        win.append(_kv_halves(w[0]) + (jnp.where(bi >= 0, bi * qt, FAR),))
    _nsa_block(q_ref[0], ng_ref[0], i * qt, ckvt_ref.at[0], mmat_ref, bmat_ref, sel_tile, i + 1, qt, None, win,
               m_ref, l_ref, acc_ref, out_ref.at[0], qt=qt, nblk_pad=nblk_pad, n_pick=n_pick)


def _nsa_prompt(q, ng, ckvt, kvt, qt):
    nb, s, aw = q.shape
    nc = ckvt.shape[-1]
    nblk = -(-s // SEL_BLOCK)
    nblk_pad = -(-nblk // LANES) * LANES
    n_pick = max(min(N_SEL, nblk) - (N_LOCAL + 1), 0)
    n_win = WINDOW // qt + 1
    mmat, bmat = _nsa_consts(nc, nblk_pad)
    rows = N_HEADS * qt
    row = lambda b, i: (b, i, 0)
    win_specs = [pl.BlockSpec((1, KV_ROWS, qt), (lambda jj: (lambda b, i: (b, 2, jnp.maximum(i - (n_win - 1) + jj, 0))))(jj))
                 for jj in range(n_win)]
    return pl.pallas_call(
        functools.partial(_nsa_prompt_kernel, qt=qt, n_win=n_win, nblk_pad=nblk_pad, n_pick=n_pick),
        grid=(nb, s // qt),
        in_specs=[pl.BlockSpec((1, qt, aw), row),
                  pl.BlockSpec((1, qt, LANES), row),
                  pl.BlockSpec((1, KV_ROWS, nc), lambda b, i: (b, 0, 0)),
                  pl.BlockSpec((1, KV_ROWS, s), lambda b, i: (b, 1, 0))] + win_specs
                 + [_full_spec(mmat), _full_spec(bmat)],
        out_specs=pl.BlockSpec((1, qt, aw), row),
        out_shape=jax.ShapeDtypeStruct((nb, s, aw), F32),
        scratch_shapes=[pltpu.VMEM((rows, 1), F32), pltpu.VMEM((rows, 1), F32), pltpu.VMEM((rows, HALF), F32)],
        compiler_params=_cparams("arbitrary", "arbitrary"),
        name="nsa_prompt",
    )(q, ng, ckvt, kvt, *([kvt] * n_win), mmat, bmat)


def _nsa_sample_kernel(pt_ref, q_ref, ng_ref, ckvt_ref, cache_hbm, ksn_ref, wst_ref, kwn_ref, mmat_ref, bmat_ref,
                       out_ref, buf, sem, m_ref, l_ref, acc_ref, *, qt, n_pages, ppt, nblk_pad, n_pick):
    slot = _paged_prologue(pt_ref, cache_hbm, buf, sem, n_pages)
    past = n_pages * LANES

    def sel_tile(j):
        ks = [buf[slot, j * ppt + pp, 0:HALF, :] for pp in range(ppt)]
        vs = [buf[slot, j * ppt + pp, HALF:2 * HALF, :] for pp in range(ppt)]
        return jnp.concatenate(ks, axis=1).astype(BF16), jnp.concatenate(vs, axis=1).astype(BF16)

    extra = _kv_halves(ksn_ref[0]) + (past,)
    win = [_kv_halves(wst_ref[0]) + (past - wst_ref.shape[-1],), _kv_halves(kwn_ref[0]) + (past,)]
    _nsa_block(q_ref[0], ng_ref[0], past, ckvt_ref.at[0], mmat_ref, bmat_ref, sel_tile, n_pages // ppt, ppt * LANES,
               extra, win, m_ref, l_ref, acc_ref, out_ref.at[0], qt=qt, nblk_pad=nblk_pad, n_pick=n_pick)


def _nsa_sample(page_tab, n_pages, q, ng, ckvt, cache_t, ks_new, w_state, kw_new):
    nb, qt, aw = q.shape
    nc = ckvt.shape[-1]
    past = n_pages * LANES
    nblk = -(-(past + qt) // SEL_BLOCK)
    nblk_pad = -(-nblk // LANES) * LANES
    n_pick = max(min(N_SEL, nblk) - (N_LOCAL + 1), 0)
    ppt = min(8, n_pages)
    mmat, bmat = _nsa_consts(nc, nblk_pad)
    rows = N_HEADS * qt
    b3 = lambda b, pt: (b, 0, 0)
    return pl.pallas_call(
        functools.partial(_nsa_sample_kernel, qt=qt, n_pages=n_pages, ppt=ppt, nblk_pad=nblk_pad, n_pick=n_pick),
        grid_spec=pltpu.PrefetchScalarGridSpec(
            num_scalar_prefetch=1,
            grid=(nb,),
            in_specs=[pl.BlockSpec((1, qt, aw), b3),
                      pl.BlockSpec((1, qt, LANES), b3),
                      pl.BlockSpec((1, KV_ROWS, nc), b3),
                      pl.BlockSpec(memory_space=pl.ANY),
                      pl.BlockSpec((1, KV_ROWS, LANES), b3),
                      pl.BlockSpec((1, KV_ROWS, w_state.shape[-1]), b3),
                      pl.BlockSpec((1, KV_ROWS, LANES), b3),
                      _full_spec(mmat), _full_spec(bmat)],
            out_specs=pl.BlockSpec((1, qt, aw), b3),
            scratch_shapes=[pltpu.VMEM((2, n_pages, KV_ROWS, LANES), F32),
                            pltpu.SemaphoreType.DMA((2,)),
                            pltpu.VMEM((rows, 1), F32), pltpu.VMEM((rows, 1), F32), pltpu.VMEM((rows, HALF), F32)]),
        out_shape=jax.ShapeDtypeStruct((nb, qt, aw), F32),
        compiler_params=_cparams("arbitrary"),
        name="nsa_sample",
    )(page_tab, q, ng, ckvt, cache_t, ks_new, w_state, kw_new, mmat, bmat)


def _pool_kernel(hist_ref, cur_ref, pw_ref, ps_ref, out_ref, *, ts, pos_base, zero_first):
    i = pl.program_id(1)
    hist = hist_ref[0]
    if zero_first:
        hist = jnp.where(i == 0, 0.0, hist)
    cur = cur_ref[0]
    ext = jnp.concatenate([hist, cur], axis=0)
    pos = pos_base + i * ts + lax.broadcasted_iota(jnp.int32, (ts, 1), 0)
    ys = []
    for gi, w in enumerate(POOL_WINDOWS):
        lanes = slice(gi * LANES, (gi + 1) * LANES)
        acc = ext[:, lanes]
        step = 1
        while step < w:
            acc = acc + pltpu.roll(acc, step, axis=0)
            step *= 2
        cnt = jnp.minimum(pos + 1, w).astype(F32)
        dlt = acc[POOL_HIST:] / cnt - cur[:, lanes]
        ys.append(_dot(dlt.astype(BF16), pw_ref[gi].astype(BF16)))
    out_ref[0] = jnp.concatenate(ys, axis=1) * ps_ref[...]


def _pool(hist_arr, hist_map, u, pool_w, pool_scale, ts, pos_base, zero_first):
    nb, s, pw = u.shape
    assert pw == len(POOL_WINDOWS) * LANES and all(w & (w - 1) == 0 and w <= POOL_HIST for w in POOL_WINDOWS)
    row = lambda b, i: (b, i, 0)
    return pl.pallas_call(
        functools.partial(_pool_kernel, ts=ts, pos_base=pos_base, zero_first=zero_first),
        grid=(nb, s // ts),
        in_specs=[pl.BlockSpec((1, POOL_HIST, pw), hist_map),
                  pl.BlockSpec((1, ts, pw), row),
                  _full_spec(pool_w), _full_spec(pool_scale)],
        out_specs=pl.BlockSpec((1, ts, pw), row),
        out_shape=jax.ShapeDtypeStruct((nb, s, pw), F32),
        compiler_params=_cparams("arbitrary", "arbitrary"),
        name="pool",
    )(hist_arr, u, pool_w, pool_scale)


def _merge_kernel(x_ref, py_ref, ay_ref, mg_ref, gt1_ref, sh2_ref, sc2_ref, gpost1_ref, gpre2_ref,
                  wbp_ref, wba_ref, wo_ref, rwh_ref, rwl_ref, rb_ref, tri_ref, cnt0_ref,
                  x1_ref, h2_ref, route_ref, cnt_ref, *, dm):
    first = (pl.program_id(0) == 0) & (pl.program_id(1) == 0)

    @pl.when(first)
    def _():
        cnt_ref[...] = cnt0_ref[...]

    bp = _dot(py_ref[0].astype(BF16), wbp_ref[...])
    ba = _dot(ay_ref[0].astype(BF16), wba_ref[...])
    merged = mg_ref[0, :, 0:dm] * bp + mg_ref[0, :, dm:2 * dm] * ba
    mix = _dot(merged.astype(BF16), wo_ref[...])
    x1 = x_ref[0] + gt1_ref[0] * _rms(mix, gpost1_ref[...])
    h2 = _rms(x1, gpre2_ref[...]) * (1.0 + sc2_ref[0]) + sh2_ref[0]
    x1_ref[0] = x1
    h2_ref[0] = h2

    hh = h2.astype(BF16)
    hl = (h2 - hh.astype(F32)).astype(BF16)
    logits = _dot(hh, rwh_ref[...]) + _dot(hl, rwh_ref[...]) + _dot(hh, rwl_ref[...]) + rb_ref[...]
    tm = logits.shape[0]
    lane = lax.broadcasted_iota(jnp.int32, (tm, LANES), 1)
    hits, vals, idxs = [], [], []
    for _ in range(TOP_K):
        best = jnp.max(logits, axis=1, keepdims=True)
        idx = jnp.min(jnp.where(logits == best, lane, LANES), axis=1, keepdims=True)
        hit = lane == idx
        hits.append(hit)
        vals.append(best)
        idxs.append(idx)
        logits = jnp.where(hit, -jnp.inf, logits)
    ex = [jnp.exp(v - vals[0]) for v in vals]
    den = ex[0]
    for e in ex[1:]:
        den = den + e
    onehot = jnp.where(hits[0], 1.0, 0.0)
    for h in hits[1:]:
        onehot = onehot + jnp.where(h, 1.0, 0.0)
    before = _dot(tri_ref[...], onehot.astype(BF16)) + cnt_ref[...]
    route = jnp.zeros((tm, LANES), F32)
    for k in range(TOP_K):
        rank = jnp.sum(jnp.where(hits[k], before, 0.0), axis=1, keepdims=True)
        route = route + jnp.where(lane == k, idxs[k].astype(F32), 0.0)
        route = route + jnp.where(lane == TOP_K + k, ex[k] / den, 0.0)
        route = route + jnp.where(lane == 2 * TOP_K + k, rank, 0.0)
    route_ref[0] = route
    cnt_ref[...] = cnt_ref[...] + jnp.sum(onehot, axis=0, keepdims=True)


def _merge(x3, py, ay, mg, gt1, sh2, sc2, mod_map, weights, cnt0, tm):
    nb, s, dm = x3.shape
    row = lambda b, i: (b, i, 0)
    mod_block = (1,) + gt1.shape[1:]
    tri = jnp.asarray(np.tril(np.ones((tm, tm), np.float32), -1), BF16)
    consts = list(weights) + [tri, cnt0]
    return pl.pallas_call(
        functools.partial(_merge_kernel, dm=dm),
        grid=(nb, s // tm),
        in_specs=[pl.BlockSpec((1, tm, dm), row),
                  pl.BlockSpec((1, tm, py.shape[-1]), row),
                  pl.BlockSpec((1, tm, ay.shape[-1]), row),
                  pl.BlockSpec((1, tm, 2 * dm), row),
                  pl.BlockSpec(mod_block, mod_map), pl.BlockSpec(mod_block, mod_map), pl.BlockSpec(mod_block, mod_map)]
                 + [_full_spec(w) for w in consts],
        out_specs=[pl.BlockSpec((1, tm, dm), row), pl.BlockSpec((1, tm, dm), row),
                   pl.BlockSpec((1, tm, LANES), row), pl.BlockSpec((1, LANES), lambda b, i: (0, 0))],
        out_shape=[jax.ShapeDtypeStruct((nb, s, dm), F32), jax.ShapeDtypeStruct((nb, s, dm), F32),
                   jax.ShapeDtypeStruct((nb, s, LANES), F32), jax.ShapeDtypeStruct((1, LANES), F32)],
        compiler_params=_cparams("arbitrary", "arbitrary"),
        name="merge",
    )(x3, py, ay, mg, gt1, sh2, sc2, *consts)


def _dispatch_kernel(dest_ref, h_ref, xs_in, xs_out, sem, *, tm):
    del xs_in

    def row_copy(r, d):
        return pltpu.make_async_copy(h_ref.at[pl.ds(r, 1), :], xs_out.at[pl.ds(d, 1), :], sem)

    def start(r, c):
        for k in range(TOP_K):
            row_copy(r, dest_ref[r * TOP_K + k]).start()
        return c

    def wait(r, c):
        for k in range(TOP_K):
            row_copy(0, 0).wait()
        return c

    lax.fori_loop(0, tm, start, 0)
    lax.fori_loop(0, tm, wait, 0)


def _dispatch(dest, h2, xs, tm):
    t, dm = h2.shape
    return pl.pallas_call(
        functools.partial(_dispatch_kernel, tm=tm),
        grid=(t // tm,),
        in_specs=[pl.BlockSpec((tm * TOP_K,), lambda i: (i,), memory_space=pltpu.SMEM),
                  pl.BlockSpec((tm, dm), lambda i: (i, 0)),
                  pl.BlockSpec(memory_space=pl.ANY)],
        out_specs=pl.BlockSpec(memory_space=pl.ANY),
        out_shape=jax.ShapeDtypeStruct(xs.shape, xs.dtype),
        scratch_shapes=[pltpu.SemaphoreType.DMA(())],
        input_output_aliases={2: 0},
        compiler_params=_cparams("arbitrary"),
        name="dispatch",
    )(dest, h2, xs)


def _expert_kernel(be_ref, nv_ref, x_ref, wgu_ref, bgu_ref, wd_ref, bd_ref, y_ref, wgu_bf, wd_bf, *, dff):
    i = pl.program_id(0)
    live = i < nv_ref[0]
    fresh = (i == 0) | (be_ref[i] != be_ref[jnp.maximum(i - 1, 0)])

    @pl.when(live & fresh)
    def _():
        wgu_bf[...] = wgu_ref[0].astype(BF16)
        wd_bf[...] = wd_ref[0].astype(BF16)

    @pl.when(live)
    def _():
        gu = _dot(x_ref[...].astype(BF16), wgu_bf[...]) + bgu_ref[0]
        gate = jnp.minimum(gu[:, :dff], SWIGLU_LIMIT)
        up = jnp.clip(gu[:, dff:], -SWIGLU_LIMIT, SWIGLU_LIMIT)
        act = (up + 1.0) * gate * jax.nn.sigmoid(SWIGLU_ALPHA * gate)
        y_ref[...] = _dot(act.astype(BF16), wd_bf[...]) + bd_ref[0]

    @pl.when(jnp.logical_not(live))
    def _():
        y_ref[...] = jnp.zeros(y_ref.shape, F32)


def _experts(blk_e, n_live, xs, w_gu, b_gu, w_down, b_down):
    rows, dm = xs.shape
    n_exp, _, dff2 = w_gu.shape
    dff = dff2 // 2
    nblk = rows // MOE_ROWS
    xmap = lambda i, be, nv: (jnp.minimum(i, nv[0] - 1), 0)
    emap = lambda i, be, nv: (be[i], 0, 0)
    return pl.pallas_call(
        functools.partial(_expert_kernel, dff=dff),
        grid_spec=pltpu.PrefetchScalarGridSpec(
            num_scalar_prefetch=2,
            grid=(nblk,),
            in_specs=[pl.BlockSpec((MOE_ROWS, dm), xmap),
                      pl.BlockSpec((1, dm, dff2), emap),
                      pl.BlockSpec((1, 1, dff2), emap),
                      pl.BlockSpec((1, dff, dm), emap),
                      pl.BlockSpec((1, 1, dm), emap)],
            out_specs=pl.BlockSpec((MOE_ROWS, dm), lambda i, be, nv: (i, 0)),
            scratch_shapes=[pltpu.VMEM((dm, dff2), BF16), pltpu.VMEM((dff, dm), BF16)]),
        out_shape=jax.ShapeDtypeStruct((rows, dm), F32),
        compiler_params=_cparams("arbitrary"),
        name="experts",
    )(blk_e, n_live, xs, w_gu, b_gu.reshape(n_exp, 1, dff2), w_down, b_down.reshape(n_exp, 1, dm))


def _final_kernel(dest_ref, route_ref, x1_ref, gt2_ref, gpost2_ref, ys_hbm, out_ref, rows_buf, sem, *, tm):
    def row_copy(r, k, d):
        return pltpu.make_async_copy(ys_hbm.at[pl.ds(d, 1), :], rows_buf.at[k, pl.ds(r, 1), :], sem)

    def start(r, c):
        for k in range(TOP_K):
            row_copy(r, k, dest_ref[r * TOP_K + k]).start()
        return c

    def wait(r, c):
        for k in range(TOP_K):
            row_copy(0, k, 0).wait()
        return c

    lax.fori_loop(0, tm, start, 0)
    lax.fori_loop(0, tm, wait, 0)
    route = route_ref[0]
    lane = lax.broadcasted_iota(jnp.int32, route.shape, 1)
    f = None
    for k in range(TOP_K):
        gate = jnp.sum(jnp.where(lane == TOP_K + k, route, 0.0), axis=1, keepdims=True)
        term = gate * rows_buf[k]
        f = term if f is None else f + term
    out_ref[0] = x1_ref[0] + gt2_ref[0] * _rms(f, gpost2_ref[...])


def _final(dest, route, x1, gt2, mod_map, g_post2, ys, tm):
    nb, s, dm = x1.shape
    nt = s // tm
    row = lambda b, i: (b, i, 0)
    mod_block = (1,) + gt2.shape[1:]
    return pl.pallas_call(
        functools.partial(_final_kernel, tm=tm),
        grid=(nb, nt),
        in_specs=[pl.BlockSpec((tm * TOP_K,), lambda b, i: (b * nt + i,), memory_space=pltpu.SMEM),
                  pl.BlockSpec((1, tm, LANES), row),
                  pl.BlockSpec((1, tm, dm), row),
                  pl.BlockSpec(mod_block, mod_map),
                  _full_spec(g_post2),
                  pl.BlockSpec(memory_space=pl.ANY)],
        out_specs=pl.BlockSpec((1, tm, dm), row),
        out_shape=jax.ShapeDtypeStruct((nb, s, dm), F32),
        scratch_shapes=[pltpu.VMEM((TOP_K, tm, dm), F32), pltpu.SemaphoreType.DMA(())],
        compiler_params=_cparams("arbitrary", "arbitrary"),
        name="final",
    )(dest, route, x1, gt2, g_post2, ys)


def _transpose_rows(kv):
    lead = kv.shape[:-4]
    n = len(lead)
    perm = tuple(range(n)) + (n + 1, n + 2, n + 3, n)
    return kv.transpose(perm).reshape(lead + (KV_ROWS, kv.shape[-4]))


def _untranspose_rows(kvt):
    lead = kvt.shape[:-2]
    n = len(lead)
    x = kvt.reshape(lead + (2, N_KV_HEADS, HEAD_DIM, kvt.shape[-1]))
    return x.transpose(tuple(range(n)) + (n + 3, n, n + 1, n + 2))


def _layer(xp, xs, cache_cmp, cache_sel, st_win, st_pool, page_table, cp, cs, w):
    nb, s, dm = xp.shape
    bd, qn, _ = xs.shape
    ts_ = bd * qn
    n_pages = page_table.shape[1]
    assert cache_cmp.shape[1] == LANES, "page size must equal the lane count"
    past = n_pages * LANES
    tm_p = 256
    tm_s = min(256, ts_)
    assert s % tm_p == 0 and ts_ % tm_s == 0 and qn % SUBLANES == 0 and qn <= LANES
    n_exp = w['router_w'].shape[1]
    assert n_exp <= LANES

    c_all = jnp.concatenate([cp, cs], axis=0)
    c_all = jnp.pad(c_all, ((0, -c_all.shape[0] % SUBLANES), (0, 0)))
    mods = _ada(c_all, w['w_ada'], w['b_ada'][None])
    mp = mods[:nb].reshape(nb, 6, 1, dm)
    mod_p = [mp[:, j] for j in range(6)]
    ms = jnp.repeat(mods[nb:nb + bd], qn, axis=0).reshape(ts_ // tm_s, tm_s, 6, dm)
    mod_s = [ms[:, :, j] for j in range(6)]
    map_p = lambda b, i: (b, 0, 0)
    map_s = lambda b, i: (i, 0, 0)

    wa, wkvt, pw, aw, order = _prep_in_weights(w['w_in'])
    g_pre1 = w['g_pre1'][None]
    u_p, q_p, mg_p, ng_p, kvt_p = _proj(xp, mod_p[1], mod_p[0], map_p, g_pre1, wa, wkvt, tm_p, pw, aw)
    xs3 = xs.reshape(1, ts_, dm)
    u_s, q_s, mg_s, ng_s, kvt_s = _proj(xs3, mod_s[1], mod_s[0], map_s, g_pre1, wa, wkvt, tm_s, pw, aw)

    cw = _prep_cmp_weights(w['cmp_pe'], w['cmp_w1'], w['cmp_b1'], w['cmp_w2'])
    page_tab = page_table.reshape(-1).astype(jnp.int32)
    ckvt_p = _compress_prompt(kvt_p, cw)
    ckvt_s = _compress_paged(page_tab, n_pages, _transpose_rows(cache_cmp), cw)
    attn_p = _nsa_prompt(q_p, ng_p, ckvt_p, kvt_p, LANES)
    new_t = kvt_s[0].reshape(3 * KV_ROWS, bd, qn).transpose(1, 0, 2)
    new_pad = jnp.pad(new_t, ((0, 0), (0, 0), (0, LANES - qn)))
    attn_s = _nsa_sample(page_tab, n_pages, q_s.reshape(bd, qn, aw), ng_s.reshape(bd, qn, LANES), ckvt_s,
                         _transpose_rows(cache_sel), new_pad[:, KV_ROWS:2 * KV_ROWS], _transpose_rows(st_win),
                         new_pad[:, 2 * KV_ROWS:])

    pool_w, pool_scale = w['pool_w'], w['pool_scale'][None]
    hpt = tm_p // POOL_HIST
    py_p = _pool(u_p, lambda b, i: (b, jnp.maximum(i * hpt - 1, 0), 0), u_p, pool_w, pool_scale, tm_p, 0, True)
    p_buf = st_pool.shape[1]
    assert p_buf == max(POOL_WINDOWS) - 1
    u_sb = u_s.reshape(bd, qn, pw)
    hist_s = jnp.pad(st_pool, ((0, 0), (POOL_HIST - p_buf, 0), (0, 0)))
    py_s = _pool(hist_s, lambda b, i: (b, 0, 0), u_sb, pool_w, pool_scale, qn, past, False)

    rw = jnp.pad(w['router_w'], ((0, 0), (0, LANES - n_exp)))
    rwh = rw.astype(BF16)
    rwl = (rw - rwh.astype(F32)).astype(BF16)
    rb = jnp.pad(w['router_b'], (0, LANES - n_exp), constant_values=NEG_BIG)[None]
    wba = w['w_br_attn'].reshape(N_HEADS, HEAD_DIM, dm)[order].reshape(aw, dm).astype(BF16)
    mweights = [w['g_post1'][None], w['g_pre2'][None], w['w_br_pool'].astype(BF16), wba, w['w_o'].astype(BF16),
                rwh, rwl, rb]
    cnt0 = jnp.zeros((1, LANES), F32)
    x1_p, h2_p, route_p, cnt_p = _merge(xp, py_p, attn_p, mg_p, mod_p[2], mod_p[3], mod_p[4], map_p, mweights,
                                        cnt0, tm_p)
    x1_s, h2_s, route_s, cnt_a = _merge(xs3, py_s.reshape(1, ts_, pw), attn_s.reshape(1, ts_, aw), mg_s, mod_s[2],
                                        mod_s[3], mod_s[4], map_s, mweights, cnt_p, tm_s)

    tp_ = nb * s
    route = jnp.concatenate([route_p.reshape(tp_, LANES), route_s.reshape(ts_, LANES)], axis=0)
    eidx = route[:, 0:TOP_K].astype(jnp.int32)
    rank = route[:, 2 * TOP_K:3 * TOP_K].astype(jnp.int32)
    counts = cnt_a[0, :n_exp].astype(jnp.int32)
    padded = (counts + MOE_ROWS - 1) // MOE_ROWS * MOE_ROWS
    pend = jnp.cumsum(padded)
    dest = ((pend - padded)[eidx] + rank).reshape(-1)
    n_blocks = (tp_ + ts_) * TOP_K // MOE_ROWS + n_exp
    blk_e = jnp.minimum(jnp.searchsorted(pend, jnp.arange(n_blocks, dtype=jnp.int32) * MOE_ROWS, side='right'),
                        n_exp - 1).astype(jnp.int32)
    n_live = (pend[-1:] // MOE_ROWS).astype(jnp.int32)

    xs_rows = jnp.zeros((n_blocks * MOE_ROWS, dm), F32)
    xs_rows = _dispatch(dest[:tp_ * TOP_K], h2_p.reshape(tp_, dm), xs_rows, tm_p)
    xs_rows = _dispatch(dest[tp_ * TOP_K:], h2_s.reshape(ts_, dm), xs_rows, tm_s)
    ys = _experts(blk_e, n_live, xs_rows, w['w_gu'], w['b_gu'], w['w_down'], w['b_down'])
    g_post2 = w['g_post2'][None]
    y_p = _final(dest[:tp_ * TOP_K], route_p, x1_p, mod_p[5], map_p, g_post2, ys, tm_p)
    y_s = _final(dest[tp_ * TOP_K:], route_s, x1_s, mod_s[5], map_s, g_post2, ys, tm_s)

    wlen = min(WINDOW, s)
    kv_p = [_untranspose_rows(kvt_p[:, j * KV_ROWS:(j + 1) * KV_ROWS]) for j in range(2)]
    kvw_p = _untranspose_rows(kvt_p[:, 2 * KV_ROWS:, s - wlen:])
    pool_p = u_p[:, s - min(POOL_HIST - 1, s):]
    kv_s = [_untranspose_rows(new_t[:, j * KV_ROWS:(j + 1) * KV_ROWS]) for j in range(3)]
    w_buf = st_win.shape[1]
    kvw_s = jnp.concatenate([st_win, kv_s[2]], axis=1)[:, -w_buf:]
    pool_s = jnp.concatenate([st_pool, u_sb], axis=1)[:, -p_buf:]
    return y_p, y_s.reshape(bd, qn, dm), (kv_p[0], kv_p[1], kvw_p, pool_p, kv_s[0], kv_s[1], kvw_s, pool_s)


def kernel(x_prompt, x_sample, cache_kv_cmp, cache_kv_sel, state_kv_win, state_pool, page_table, c_prompt, c_sample,
           w_ada, b_ada, g_pre1, g_post1, g_pre2, g_post2, w_in, pool_w, pool_scale, cmp_pe, cmp_w1, cmp_b1, cmp_w2,
           w_br_pool, w_br_attn, w_o, router_w, router_b, w_gu, b_gu, w_down, b_down):
    names = ('w_ada', 'b_ada', 'g_pre1', 'g_post1', 'g_pre2', 'g_post2', 'w_in', 'pool_w', 'pool_scale', 'cmp_pe',
             'cmp_w1', 'cmp_b1', 'cmp_w2', 'w_br_pool', 'w_br_attn', 'w_o', 'router_w', 'router_b', 'w_gu', 'b_gu',
             'w_down', 'b_down')
    stacked = (w_ada, b_ada, g_pre1, g_post1, g_pre2, g_post2, w_in, pool_w, pool_scale, cmp_pe, cmp_w1, cmp_b1,
               cmp_w2, w_br_pool, w_br_attn, w_o, router_w, router_b, w_gu, b_gu, w_down, b_down)
    xp, xs = x_prompt, x_sample
    states = []
    for l in range(w_ada.shape[0]):
        w = {n: a[l] for n, a in zip(names, stacked)}
        xp, xs, st = _layer(xp, xs, cache_kv_cmp[l], cache_kv_sel[l], state_kv_win[l], state_pool[l], page_table,
                            c_prompt, c_sample, w)
        states.append(st)
    return (xp, xs) + tuple(jnp.stack([st[j] for st in states]) for j in range(8))
```

```python
import functools

import jax
import jax.numpy as jnp
import numpy as np
from jax import lax
from jax.experimental import pallas as pl
from jax.experimental.pallas import tpu as pltpu

F32 = jnp.float32
BF16 = jnp.bfloat16

POOL_WINDOWS = (2, 4, 8, 16)
N_HEADS = 8
N_KV_HEADS = 2
HEAD_DIM = 64
GQ = N_HEADS // N_KV_HEADS
CMP_BLOCK = 32
CMP_STRIDE = 16
SEL_BLOCK = 64
SUB_PER_SEL = SEL_BLOCK // CMP_STRIDE
N_SEL = 16
N_LOCAL = 2
WINDOW = 512
TOP_K = 4
SWIGLU_LIMIT = 7.0
SWIGLU_ALPHA = 1.702
RMS_EPS = 1e-6

LANES = 128
SUBLANES = 8
VMEM_LIMIT = 56 * 1024 * 1024
NEG_BIG = -1e30
M_INIT = -1e29
MXU_DEPTH = 256
FAR = 1 << 30
KV_ROWS = 2 * N_KV_HEADS * HEAD_DIM
HALF = N_KV_HEADS * HEAD_DIM
POOL_HIST = 16
MOE_ROWS = 256


def _cparams(*sem):
    return pltpu.CompilerParams(dimension_semantics=sem, vmem_limit_bytes=VMEM_LIMIT)


def _full_spec(w):
    nd = w.ndim
    return pl.BlockSpec(w.shape, lambda *a: (0,) * nd)


def _dot(a, b):
    return jnp.dot(a, b, preferred_element_type=F32)


def _dot_nt(a, b):
    return lax.dot_general(a, b, (((1,), (1,)), ((), ())), preferred_element_type=F32)


def _split3(x):
    hi = x.astype(BF16)
    r1 = x - hi.astype(F32)
    mid = r1.astype(BF16)
    lo = (r1 - mid.astype(F32)).astype(BF16)
    return hi, mid, lo


def _dot_exact_rhs(x, m_bf16):
    hi, mid, lo = _split3(x)
    return _dot(hi, m_bf16) + _dot(mid, m_bf16) + _dot(lo, m_bf16)


def _rms(x, g):
    return x * lax.rsqrt(jnp.mean(x * x, axis=-1, keepdims=True) + RMS_EPS) * g


def _ada_kernel(c_ref, w_ref, b_ref, o_ref):
    o_ref[...] = _dot(c_ref[...].astype(BF16), w_ref[...].astype(BF16)) + b_ref[...]


def _ada(c_all, w_ada, b_ada):
    m, d = c_all.shape
    n = w_ada.shape[1]
    tn = 1024
    return pl.pallas_call(
        _ada_kernel,
        grid=(n // tn,),
        in_specs=[pl.BlockSpec((m, d), lambda j: (0, 0)),
                  pl.BlockSpec((d, tn), lambda j: (0, j)),
                  pl.BlockSpec((1, tn), lambda j: (0, j))],
        out_specs=pl.BlockSpec((m, tn), lambda j: (0, j)),
        out_shape=jax.ShapeDtypeStruct((m, n), F32),
        compiler_params=_cparams("arbitrary"),
        name="ada",
    )(c_all, w_ada, b_ada)


def _proj_kernel(x_ref, sc_ref, sh_ref, g_ref, wa_ref, wkv_ref,
                 u_ref, q_ref, mg_ref, ng_ref, kvt_ref, *, pw, aw, dm):
    hb = (_rms(x_ref[0], g_ref[...]) * (1.0 + sc_ref[0]) + sh_ref[0]).astype(BF16)
    u_ref[0] = _dot(hb, wa_ref[:, 0:pw])
    q_ref[0] = _dot(hb, wa_ref[:, pw:pw + aw])
    o = pw + aw
    mg_ref[0] = jax.nn.sigmoid(_dot(hb, wa_ref[:, o:o + 2 * dm]))
    ng_ref[0] = jax.nn.sigmoid(_dot(hb, wa_ref[:, o + 2 * dm:o + 2 * dm + LANES]))
    kvt_ref[0] = _dot_nt(wkv_ref[...], hb)


def _proj(x3, sc, sh, mod_map, g_pre1, wa, wkvt, tm, pw, aw):
    nb, s, dm = x3.shape
    nt = s // tm
    kern = functools.partial(_proj_kernel, pw=pw, aw=aw, dm=dm)
    mod_block = (1,) + sc.shape[1:]
    row = lambda b, i: (b, i, 0)
    return pl.pallas_call(
        kern,
        grid=(nb, nt),
        in_specs=[pl.BlockSpec((1, tm, dm), row),
                  pl.BlockSpec(mod_block, mod_map),
                  pl.BlockSpec(mod_block, mod_map),
                  _full_spec(g_pre1), _full_spec(wa), _full_spec(wkvt)],
        out_specs=[pl.BlockSpec((1, tm, pw), row),
                   pl.BlockSpec((1, tm, aw), row),
                   pl.BlockSpec((1, tm, 2 * dm), row),
                   pl.BlockSpec((1, tm, LANES), row),
                   pl.BlockSpec((1, 3 * KV_ROWS, tm), lambda b, i: (b, 0, i))],
        out_shape=[jax.ShapeDtypeStruct((nb, s, pw), F32),
                   jax.ShapeDtypeStruct((nb, s, aw), F32),
                   jax.ShapeDtypeStruct((nb, s, 2 * dm), F32),
                   jax.ShapeDtypeStruct((nb, s, LANES), F32),
                   jax.ShapeDtypeStruct((nb, 3 * KV_ROWS, s), F32)],
        compiler_params=_cparams("arbitrary", "arbitrary"),
        name="proj",
    )(x3, sc, sh, g_pre1, wa, wkvt)


def _prep_in_weights(w_in):
    dm = w_in.shape[0]
    pw = dm // 2
    aw = N_HEADS * HEAD_DIM
    off_kvc = pw + aw
    off_ng = off_kvc + 3 * KV_ROWS
    off_mg = off_ng + 3 * N_HEADS
    order = np.array([k * GQ + g for g in range(GQ) for k in range(N_KV_HEADS)])
    wq = w_in[:, pw:off_kvc].reshape(dm, N_HEADS, HEAD_DIM)[:, order].reshape(dm, aw) * (HEAD_DIM ** -0.5)
    wng = jnp.pad(w_in[:, off_ng:off_mg], ((0, 0), (0, LANES - 3 * N_HEADS)))
    wa = jnp.concatenate([w_in[:, :pw], wq, w_in[:, off_mg:], wng], axis=1).astype(BF16)
    wkvt = w_in[:, off_kvc:off_ng].T.astype(BF16)
    return wa, wkvt, pw, aw, order


def _compress_core(get_page, n_pages, perm_ref, wp_ref, w2t_ref, pe_ref, w1f_ref, b1_ref, s2_ref, out_ref):
    sub_pp = LANES // CMP_STRIDE
    n_sub = n_pages * sub_pp
    r_cmp = CMP_BLOCK // CMP_STRIDE
    hid = w1f_ref.shape[-1]

    def page_body(p, carry):
        pg = get_page(p).astype(BF16)
        t = _dot_nt(perm_ref[...], pg)
        row0 = pl.multiple_of(p * sub_pp, sub_pp)
        for s in range(2):
            for j in range(CMP_STRIDE):
                s2_ref[s, j // 2, pl.ds(row0, sub_pp), (j % 2) * HALF:(j % 2 + 1) * HALF] = (
                    t[sub_pp * j:sub_pp * (j + 1), s * HALF:(s + 1) * HALF])
        return carry

    lax.fori_loop(0, n_pages, page_body, 0, unroll=4)
    for s in range(2):
        acc = None
        for jp in range(CMP_STRIDE // 2):
            d = _dot(s2_ref[s, jp].astype(BF16), wp_ref[s, jp])
            acc = d if acc is None else acc + d
        bias = _dot(pe_ref[s], w1f_ref[s])[0:1] + b1_ref[s]
        bias = jnp.concatenate([bias] * N_KV_HEADS, axis=1)
        w = N_KV_HEADS * hid
        hpre = acc[:, :w] + bias
        for r in range(1, r_cmp):
            hpre = hpre + pltpu.roll(acc[:, r * w:(r + 1) * w], n_sub - r, axis=0)
        g = jax.nn.gelu(hpre).astype(BF16)
        out_ref[s * HALF:(s + 1) * HALF, :] = _dot_nt(w2t_ref[s], g)


def _compress_prompt_kernel(kv_ref, perm_ref, wp_ref, w2t_ref, pe_ref, w1f_ref, b1_ref, out_ref, s2_ref, *, n_pages):
    def get_page(p):
        return kv_ref[0, :, pl.ds(pl.multiple_of(p * LANES, LANES), LANES)]
    _compress_core(get_page, n_pages, perm_ref, wp_ref, w2t_ref, pe_ref, w1f_ref, b1_ref, s2_ref, out_ref.at[0])


def _page_fetch(pt_ref, cache_hbm, buf, sem, b, slot, n_pages):
    def start(p, c):
        pltpu.make_async_copy(cache_hbm.at[pt_ref[b * n_pages + p]], buf.at[slot, p], sem.at[slot]).start()
        return c
    lax.fori_loop(0, n_pages, start, 0)


def _page_wait(cache_hbm, buf, sem, slot, n_pages):
    def wait(p, c):
        pltpu.make_async_copy(cache_hbm.at[0], buf.at[slot, p], sem.at[slot]).wait()
        return c
    lax.fori_loop(0, n_pages, wait, 0)


def _paged_prologue(pt_ref, cache_hbm, buf, sem, n_pages):
    b = pl.program_id(0)
    slot = b % 2

    @pl.when(b == 0)
    def _():
        _page_fetch(pt_ref, cache_hbm, buf, sem, 0, 0, n_pages)

    @pl.when(b + 1 < pl.num_programs(0))
    def _():
        _page_fetch(pt_ref, cache_hbm, buf, sem, b + 1, 1 - slot, n_pages)

    _page_wait(cache_hbm, buf, sem, slot, n_pages)
    return slot


def _compress_paged_kernel(pt_ref, cache_hbm, perm_ref, wp_ref, w2t_ref, pe_ref, w1f_ref, b1_ref, out_ref,
                           buf, sem, s2_ref, *, n_pages):
    slot = _paged_prologue(pt_ref, cache_hbm, buf, sem, n_pages)
    _compress_core(lambda p: buf[slot, p], n_pages, perm_ref, wp_ref, w2t_ref, pe_ref, w1f_ref, b1_ref,
                   s2_ref, out_ref.at[0])


def _prep_cmp_weights(cmp_pe, cmp_w1, cmp_b1, cmp_w2):
    hid = cmp_w1.shape[-1]
    r_cmp = CMP_BLOCK // CMP_STRIDE
    eye = jnp.eye(N_KV_HEADS, dtype=F32)
    w1r = cmp_w1.reshape(2, r_cmp, CMP_STRIDE // 2, 2, HEAD_DIM, hid)
    wp = jnp.einsum('srpjdh,kc->spjkdrch', w1r, eye).reshape(
        2, CMP_STRIDE // 2, 2 * HALF, r_cmp * N_KV_HEADS * hid).astype(BF16)
    w2t = jnp.einsum('shd,kc->skdch', cmp_w2, eye).reshape(2, HALF, N_KV_HEADS * hid).astype(BF16)
    pe = jnp.broadcast_to(cmp_pe.reshape(2, 1, CMP_BLOCK * HEAD_DIM),
                          (2, SUBLANES, CMP_BLOCK * HEAD_DIM)).astype(BF16)
    w1f = cmp_w1.reshape(2, CMP_BLOCK * HEAD_DIM, hid).astype(BF16)
    b1 = cmp_b1.reshape(2, 1, hid)
    sub_pp = LANES // CMP_STRIDE
    x = np.arange(LANES)
    perm = np.zeros((LANES, LANES), np.float32)
    perm[x, (x % sub_pp) * CMP_STRIDE + x // sub_pp] = 1.0
    return (jnp.asarray(perm, BF16), wp, w2t, pe, w1f, b1)


def _compress_prompt(kvt, cw):
    nb, _, s = kvt.shape
    n_pages = s // LANES
    n_sub = s // CMP_STRIDE
    return pl.pallas_call(
        functools.partial(_compress_prompt_kernel, n_pages=n_pages),
        grid=(nb,),
        in_specs=[pl.BlockSpec((1, KV_ROWS, s), lambda b: (b, 0, 0))] + [_full_spec(w) for w in cw],
        out_specs=pl.BlockSpec((1, KV_ROWS, n_sub), lambda b: (b, 0, 0)),
        out_shape=jax.ShapeDtypeStruct((nb, KV_ROWS, n_sub), F32),
        scratch_shapes=[pltpu.VMEM((2, CMP_STRIDE // 2, n_sub, 2 * HALF), F32)],
        compiler_params=_cparams("arbitrary"),
        name="compress_prompt",
    )(kvt, *cw)


def _compress_paged(page_tab, n_pages, cache_t, cw):
    nb = page_tab.shape[0] // n_pages
    n_sub = n_pages * (LANES // CMP_STRIDE)
    return pl.pallas_call(
        functools.partial(_compress_paged_kernel, n_pages=n_pages),
        grid_spec=pltpu.PrefetchScalarGridSpec(
            num_scalar_prefetch=1,
            grid=(nb,),
            in_specs=[pl.BlockSpec(memory_space=pl.ANY)] + [_full_spec(w) for w in cw],
            out_specs=pl.BlockSpec((1, KV_ROWS, n_sub), lambda b, pt: (b, 0, 0)),
            scratch_shapes=[pltpu.VMEM((2, n_pages, KV_ROWS, LANES), F32),
                            pltpu.SemaphoreType.DMA((2,)),
                            pltpu.VMEM((2, CMP_STRIDE // 2, n_sub, 2 * HALF), F32)]),
        out_shape=jax.ShapeDtypeStruct((nb, KV_ROWS, n_sub), F32),
        compiler_params=_cparams("arbitrary"),
        name="compress_paged",
    )(page_tab, cache_t, *cw)


def _stack_heads(fn):
    return jnp.concatenate([fn(k, g) for k in range(N_KV_HEADS) for g in range(GQ)], axis=0)


def _alibi_slope(k, g):
    return 2.0 ** (-8.0 * (k * GQ + g + 1) / N_HEADS)


def _nsa_block(q, sgate, t0, ckvt_ref, mmat_ref, bmat_ref, sel_tile, n_sel_tiles, sel_tk, sel_last, win_tile,
               m_ref, acc_ref, out_ref, *, qt, nblk_pad, n_pick):
    t_pos = t0 + lax.broadcasted_iota(jnp.int32, (qt, 1), 0)
    lane_half = lax.broadcasted_iota(jnp.int32, (qt, LANES), 1) // HEAD_DIM

    qp = _stack_heads(lambda k, g: jnp.where(lane_half == k, q[:, g * LANES:(g + 1) * LANES], 0.0)).astype(BF16)

    def alibi(distf):
        return _stack_heads(lambda k, g: _alibi_slope(k, g) * distf)

    def stack_masks(mk):
        return jnp.concatenate([mk[k] for k in range(N_KV_HEADS) for _ in range(GQ)], axis=0) > 0.5

    nc = ckvt_ref.shape[-1]
    c_end = lax.broadcasted_iota(jnp.int32, (1, nc), 1) * CMP_STRIDE + (CMP_BLOCK - 1)
    dist_c = t_pos - c_end
    valid_c = jnp.where(dist_c >= 0, 1.0, 0.0)
    mask_c = stack_masks([valid_c] * N_KV_HEADS)
    s = _dot(qp, ckvt_ref[0:HALF, :].astype(BF16)) - alibi(dist_c.astype(F32))
    s = jnp.where(mask_c, s, NEG_BIG)
    p = jnp.where(mask_c, jnp.exp(s - jnp.max(s, axis=1, keepdims=True)), 0.0)
    p = p / jnp.maximum(jnp.sum(p, axis=1, keepdims=True), 1e-30)
    o_c = _dot_nt(p.astype(BF16), ckvt_ref[HALF:2 * HALF, :].astype(BF16))

    blk = lax.broadcasted_iota(jnp.int32, (qt, nblk_pad), 1)
    cur = t_pos // SEL_BLOCK
    forced = (blk == 0) | ((blk > cur - N_LOCAL) & (blk <= cur))
    free = (blk >= 1) & (blk <= cur - N_LOCAL)
    blkf = blk.astype(F32)
    sel = []
    for k in range(N_KV_HEADS):
        imp = p[k * GQ * qt:(k * GQ + 1) * qt]
        for g in range(1, GQ):
            imp = imp + p[(k * GQ + g) * qt:(k * GQ + g + 1) * qt]
        score = jnp.where(free, _dot_exact_rhs(imp, mmat_ref[...]), -jnp.inf)
        chosen = jnp.where(forced, 1.0, 0.0)
        for _ in range(n_pick):
            best = jnp.max(score, axis=1, keepdims=True)
            idx = jnp.min(jnp.where(score == best, blkf, float(nblk_pad)), axis=1, keepdims=True)
            hit = blkf == idx
            chosen = jnp.where(hit, 1.0, chosen)
            score = jnp.where(hit, -jnp.inf, score)
        sel.append(chosen)

    rk = GQ * qt
    lane = lax.broadcasted_iota(jnp.int32, (1, LANES), 1)
    t_rows = jnp.concatenate([t_pos] * GQ, axis=0)

    def q_aug(k):
        selneg = jnp.where(sel[k] > 0.5, 0.0, NEG_BIG)
        parts = []
        for g in range(GQ):
            sl = _alibi_slope(k, g)
            base = (1 - k) * HEAD_DIM
            ext = jnp.where(lane == base, SEL_BLOCK * sl, jnp.where(lane == base + 1, sl, 0.0))
            feat = jnp.where(lane_half == k, q[:, g * LANES:(g + 1) * LANES], ext)
            parts.append(jnp.concatenate([selneg, feat], axis=1))
        return jnp.concatenate(parts, axis=0).astype(BF16)

    qa = [q_aug(k) for k in range(N_KV_HEADS)]

    def k_aug(kt, pos, use_sel):
        tk = kt.shape[-1]
        hi = lax.shift_right_arithmetic(pos, SEL_BLOCK.bit_length() - 1)
        lo = pos & (SEL_BLOCK - 1)
        r = lax.broadcasted_iota(jnp.int32, (HEAD_DIM, tk), 0)
        ext = jnp.where(r == 0, hi.astype(F32), jnp.where(r == 1, lo.astype(F32), 0.0)).astype(BF16)
        if use_sel:
            onehot = jnp.where(lax.broadcasted_iota(jnp.int32, (nblk_pad, 1), 0) == hi, 1.0, 0.0).astype(BF16)
        else:
            onehot = jnp.zeros((nblk_pad, tk), BF16)
        kb = kt.astype(BF16)
        return [jnp.concatenate([onehot, kb[0:HEAD_DIM], ext], axis=0),
                jnp.concatenate([onehot, ext, kb[HEAD_DIM:HALF]], axis=0)]

    def reset():
        m_ref[...] = jnp.full(m_ref.shape, M_INIT, F32)
        acc_ref[...] = jnp.zeros(acc_ref.shape, F32)

    def flash(kt, vt, pos, use_sel, cond):
        ka = k_aug(kt, pos, use_sel)
        va = jnp.concatenate([vt.astype(BF16), jnp.ones((2 * SUBLANES, kt.shape[-1]), BF16)], axis=0)
        for k in range(N_KV_HEADS):
            sc = _dot(qa[k], ka[k])
            if cond is not None:
                sc = jnp.where(cond, sc, NEG_BIG)
            m_old = m_ref[k]
            m_new = jnp.maximum(m_old, jnp.max(sc, axis=1, keepdims=True))
            pr = jnp.exp(sc - m_new).astype(BF16)
            acc_ref[k] = jnp.exp(m_old - m_new) * acc_ref[k] + _dot_nt(pr, va)
            m_ref[k] = m_new

    def result():
        out = []
        for k in range(N_KV_HEADS):
            a = acc_ref[k]
            out.append(a[:, 0:HALF] / jnp.maximum(a[:, HALF:HALF + 1], 1e-30))
        return out

    reset()

    def sel_body(j, c):
        kt, vt = sel_tile(j)
        flash(kt, vt, j * sel_tk + lax.broadcasted_iota(jnp.int32, (1, sel_tk), 1), True, None)
        return c

    lax.fori_loop(0, n_sel_tiles, sel_body, 0)
    kt, vt, pos, use_sel = sel_last
    flash(kt, vt, pos, use_sel, t_rows >= pos)
    o_s = result()

    reset()
    kt, vt, pos = win_tile
    dist = t_rows - pos
    flash(kt, vt, pos, False, (dist >= 0) & (dist <= WINDOW))
    o_w = result()

    o_c = [o_c[0:rk], o_c[rk:2 * rk]]
    g3 = _split3(sgate)
    for g in range(GQ):
        slab = None
        for c, o in enumerate((o_c, o_s, o_w)):
            bm = bmat_ref[g * 3 + c]
            gate = _dot(g3[0], bm) + _dot(g3[1], bm) + _dot(g3[2], bm)
            val = jnp.where(lane_half == 0, o[0][g * qt:(g + 1) * qt], o[1][g * qt:(g + 1) * qt])
            slab = gate * val if slab is None else slab + gate * val
        out_ref[:, g * LANES:(g + 1) * LANES] = slab


def _nsa_consts(nc, nblk_pad):
    j = np.arange(nc)[:, None]
    lo = SUB_PER_SEL * np.arange(nblk_pad)[None, :]
    mm = (np.where((j >= lo) & (j < lo + SUB_PER_SEL - 1), 2.0, 0.0)
          + np.where((j == lo - 1) | (j == lo + SUB_PER_SEL - 1), 1.0, 0.0))
    col = np.arange(LANES)[:, None]
    lane = np.arange(LANES)[None, :]
    bm = np.stack([(col == ((lane // HEAD_DIM) * GQ + g) * 3 + c)
                   for g in range(GQ) for c in range(3)]).astype(np.float32)
    return jnp.asarray(mm, BF16), jnp.asarray(bm, BF16)


def _lane_iota(n):
    return lax.broadcasted_iota(jnp.int32, (1, n), 1)


def _flash_scratch(qt):
    rk = GQ * qt
    return [pltpu.VMEM((N_KV_HEADS, rk, 1), F32), pltpu.VMEM((N_KV_HEADS, rk, HALF + 2 * SUBLANES), F32)]


def _nsa_prompt_kernel(*refs, qt, tk, n_win, nblk_pad, n_pick):
    q_ref, ng_ref, ckvt_ref, kvs_ref = refs[:4]
    win_refs = refs[4:4 + n_win]
    mmat_ref, bmat_ref, out_ref, m_ref, acc_ref = refs[4 + n_win:]
    i = pl.program_id(1)

    def sel_tile(j):
        off = pl.multiple_of(j * tk, LANES)
        return kvs_ref[0, 0:HALF, pl.ds(off, tk)], kvs_ref[0, HALF:2 * HALF, pl.ds(off, tk)]

    n_past = (i * qt) // tk
    sel_last = sel_tile(n_past) + (n_past * tk + _lane_iota(tk), True)
    pieces = [w[0] for w in win_refs]
    pos = []
    for jj in range(n_win):
        bi = i - (n_win - 1) + jj
        pos.append(jnp.where(bi >= 0, bi * qt, -FAR) + _lane_iota(qt))
    win = (jnp.concatenate([p[0:HALF] for p in pieces], axis=1),
           jnp.concatenate([p[HALF:2 * HALF] for p in pieces], axis=1), jnp.concatenate(pos, axis=1))
    _nsa_block(q_ref[0], ng_ref[0], i * qt, ckvt_ref.at[0], mmat_ref, bmat_ref, sel_tile, n_past, tk, sel_last, win,
               m_ref, acc_ref, out_ref.at[0], qt=qt, nblk_pad=nblk_pad, n_pick=n_pick)


def _nsa_prompt(q, ng, ckvt, kvt, qt):
    nb, s, aw = q.shape
    nc = ckvt.shape[-1]
    nblk = -(-s // SEL_BLOCK)
    nblk_pad = -(-nblk // LANES) * LANES
    assert nblk_pad + HALF <= MXU_DEPTH, "selection mask + features must fit one MXU contraction pass"
    n_pick = max(min(N_SEL, nblk) - (N_LOCAL + 1), 0)
    n_win = WINDOW // qt + 1
    tk = min(4 * qt, s)
    assert s % tk == 0 and tk % qt == 0
    mmat, bmat = _nsa_consts(nc, nblk_pad)
    row = lambda b, i: (b, i, 0)
    win_specs = [pl.BlockSpec((1, KV_ROWS, qt), (lambda jj: (lambda b, i: (b, 2, jnp.maximum(i - (n_win - 1) + jj, 0))))(jj))
                 for jj in range(n_win)]
    return pl.pallas_call(
        functools.partial(_nsa_prompt_kernel, qt=qt, tk=tk, n_win=n_win, nblk_pad=nblk_pad, n_pick=n_pick),
        grid=(nb, s // qt),
        in_specs=[pl.BlockSpec((1, qt, aw), row),
                  pl.BlockSpec((1, qt, LANES), row),
                  pl.BlockSpec((1, KV_ROWS, nc), lambda b, i: (b, 0, 0)),
                  pl.BlockSpec((1, KV_ROWS, s), lambda b, i: (b, 1, 0))] + win_specs
                 + [_full_spec(mmat), _full_spec(bmat)],
        out_specs=pl.BlockSpec((1, qt, aw), row),
        out_shape=jax.ShapeDtypeStruct((nb, s, aw), F32),
        scratch_shapes=_flash_scratch(qt),
        compiler_params=_cparams("arbitrary", "arbitrary"),
        name="nsa_prompt",
    )(q, ng, ckvt, kvt, *([kvt] * n_win), mmat, bmat)


def _nsa_sample_kernel(pt_ref, q_ref, ng_ref, ckvt_ref, cache_hbm, ksn_ref, wst_ref, kwn_ref, mmat_ref, bmat_ref,
                       out_ref, buf, sem, m_ref, acc_ref, *, qt, n_pages, ppt, nblk_pad, n_pick):
    slot = _paged_prologue(pt_ref, cache_hbm, buf, sem, n_pages)
    past = n_pages * LANES
    wbuf = wst_ref.shape[-1]

    def sel_tile(j):
        ks = [buf[slot, j * ppt + pp, 0:HALF, :] for pp in range(ppt)]
        vs = [buf[slot, j * ppt + pp, HALF:2 * HALF, :] for pp in range(ppt)]
        return jnp.concatenate(ks, axis=1), jnp.concatenate(vs, axis=1)

    sel_last = (ksn_ref[0, 0:HALF, :], ksn_ref[0, HALF:2 * HALF, :], past + _lane_iota(LANES), False)
    win = (jnp.concatenate([wst_ref[0, 0:HALF, :], kwn_ref[0, 0:HALF, :]], axis=1),
           jnp.concatenate([wst_ref[0, HALF:2 * HALF, :], kwn_ref[0, HALF:2 * HALF, :]], axis=1),
           past - wbuf + _lane_iota(wbuf + LANES))
    _nsa_block(q_ref[0], ng_ref[0], past, ckvt_ref.at[0], mmat_ref, bmat_ref, sel_tile, n_pages // ppt, ppt * LANES,
               sel_last, win, m_ref, acc_ref, out_ref.at[0], qt=qt, nblk_pad=nblk_pad, n_pick=n_pick)


def _nsa_sample(page_tab, n_pages, q, ng, ckvt, cache_t, ks_new, w_state, kw_new):
    nb, qt, aw = q.shape
    nc = ckvt.shape[-1]
    past = n_pages * LANES
    assert past % SEL_BLOCK == 0 and qt <= SEL_BLOCK
    nblk_pad = -(-(past // SEL_BLOCK) // LANES) * LANES
    assert nblk_pad + HALF <= MXU_DEPTH
    n_pick = max(min(N_SEL, past // SEL_BLOCK + 1) - (N_LOCAL + 1), 0)
    ppt = min(8, n_pages)
    assert n_pages % ppt == 0
    mmat, bmat = _nsa_consts(nc, nblk_pad)
    b3 = lambda b, pt: (b, 0, 0)
    return pl.pallas_call(
        functools.partial(_nsa_sample_kernel, qt=qt, n_pages=n_pages, ppt=ppt, nblk_pad=nblk_pad, n_pick=n_pick),
        grid_spec=pltpu.PrefetchScalarGridSpec(
            num_scalar_prefetch=1,
            grid=(nb,),
            in_specs=[pl.BlockSpec((1, qt, aw), b3),
                      pl.BlockSpec((1, qt, LANES), b3),
                      pl.BlockSpec((1, KV_ROWS, nc), b3),
                      pl.BlockSpec(memory_space=pl.ANY),
                      pl.BlockSpec((1, KV_ROWS, LANES), b3),
                      pl.BlockSpec((1, KV_ROWS, w_state.shape[-1]), b3),
                      pl.BlockSpec((1, KV_ROWS, LANES), b3),
                      _full_spec(mmat), _full_spec(bmat)],
            out_specs=pl.BlockSpec((1, qt, aw), b3),
            scratch_shapes=[pltpu.VMEM((2, n_pages, KV_ROWS, LANES), F32),
                            pltpu.SemaphoreType.DMA((2,))] + _flash_scratch(qt)),
        out_shape=jax.ShapeDtypeStruct((nb, qt, aw), F32),
        compiler_params=_cparams("arbitrary"),
        name="nsa_sample",
    )(page_tab, q, ng, ckvt, cache_t, ks_new, w_state, kw_new, mmat, bmat)


def _pool_kernel(hist_ref, cur_ref, pw_ref, ps_ref, out_ref, *, ts, pos_base, zero_first):
    i = pl.program_id(1)
    hist = hist_ref[0]
    if zero_first:
        hist = jnp.where(i == 0, 0.0, hist)
    cur = cur_ref[0]
    ext = jnp.concatenate([hist, cur], axis=0)
    pos = pos_base + i * ts + lax.broadcasted_iota(jnp.int32, (ts, 1), 0)
    ys = []
    for gi, w in enumerate(POOL_WINDOWS):
        lanes = slice(gi * LANES, (gi + 1) * LANES)
        acc = ext[:, lanes]
        step = 1
        while step < w:
            acc = acc + pltpu.roll(acc, step, axis=0)
            step *= 2
        cnt = jnp.minimum(pos + 1, w).astype(F32)
        dlt = acc[POOL_HIST:] / cnt - cur[:, lanes]
        ys.append(_dot(dlt.astype(BF16), pw_ref[gi].astype(BF16)))
    out_ref[0] = jnp.concatenate(ys, axis=1) * ps_ref[...]


def _pool(hist_arr, hist_map, u, pool_w, pool_scale, ts, pos_base, zero_first):
    nb, s, pw = u.shape
    assert pw == len(POOL_WINDOWS) * LANES and all(w & (w - 1) == 0 and w <= POOL_HIST for w in POOL_WINDOWS)
    row = lambda b, i: (b, i, 0)
    return pl.pallas_call(
        functools.partial(_pool_kernel, ts=ts, pos_base=pos_base, zero_first=zero_first),
        grid=(nb, s // ts),
        in_specs=[pl.BlockSpec((1, POOL_HIST, pw), hist_map),
                  pl.BlockSpec((1, ts, pw), row),
                  _full_spec(pool_w), _full_spec(pool_scale)],
        out_specs=pl.BlockSpec((1, ts, pw), row),
        out_shape=jax.ShapeDtypeStruct((nb, s, pw), F32),
        compiler_params=_cparams("arbitrary", "arbitrary"),
        name="pool",
    )(hist_arr, u, pool_w, pool_scale)


def _merge_kernel(x_ref, py_ref, ay_ref, mg_ref, gt1_ref, sh2_ref, sc2_ref, gpost1_ref, gpre2_ref,
                  wbp_ref, wba_ref, wo_ref, rwh_ref, rwl_ref, rb_ref, tri_ref, cnt0_ref,
                  x1_ref, h2_ref, route_ref, cnt_ref, *, dm):
    first = (pl.program_id(0) == 0) & (pl.program_id(1) == 0)

    @pl.when(first)
    def _():
        cnt_ref[...] = cnt0_ref[...]

    bp = _dot(py_ref[0].astype(BF16), wbp_ref[...])
    ba = _dot(ay_ref[0].astype(BF16), wba_ref[...])
    merged = mg_ref[0, :, 0:dm] * bp + mg_ref[0, :, dm:2 * dm] * ba
    mix = _dot(merged.astype(BF16), wo_ref[...])
    x1 = x_ref[0] + gt1_ref[0] * _rms(mix, gpost1_ref[...])
    h2 = _rms(x1, gpre2_ref[...]) * (1.0 + sc2_ref[0]) + sh2_ref[0]
    x1_ref[0] = x1
    h2_ref[0] = h2

    hh = h2.astype(BF16)
    hl = (h2 - hh.astype(F32)).astype(BF16)
    logits = _dot(hh, rwh_ref[...]) + _dot(hl, rwh_ref[...]) + _dot(hh, rwl_ref[...]) + rb_ref[...]
    tm = logits.shape[0]
    lane = lax.broadcasted_iota(jnp.int32, (tm, LANES), 1)
    lanef = lane.astype(F32)
    hits, vals, idxs = [], [], []
    for _ in range(TOP_K):
        best = jnp.max(logits, axis=1, keepdims=True)
        idx = jnp.min(jnp.where(logits == best, lanef, float(LANES)), axis=1, keepdims=True)
        hit = lanef == idx
        hits.append(hit)
        vals.append(best)
        idxs.append(idx)
        logits = jnp.where(hit, -jnp.inf, logits)
    ex = [jnp.exp(v - vals[0]) for v in vals]
    den = ex[0]
    for e in ex[1:]:
        den = den + e
    onehot = jnp.where(hits[0], 1.0, 0.0)
    for h in hits[1:]:
        onehot = onehot + jnp.where(h, 1.0, 0.0)
    before = _dot(tri_ref[...], onehot.astype(BF16)) + cnt_ref[...]
    route = jnp.zeros((tm, LANES), F32)
    for k in range(TOP_K):
        rank = jnp.sum(jnp.where(hits[k], before, 0.0), axis=1, keepdims=True)
        route = route + jnp.where(lane == k, idxs[k], 0.0)
        route = route + jnp.where(lane == TOP_K + k, ex[k] / den, 0.0)
        route = route + jnp.where(lane == 2 * TOP_K + k, rank, 0.0)
    route_ref[0] = route
    cnt_ref[...] = cnt_ref[...] + jnp.sum(onehot, axis=0, keepdims=True)


def _merge(x3, py, ay, mg, gt1, sh2, sc2, mod_map, weights, cnt0, tm):
    nb, s, dm = x3.shape
    row = lambda b, i: (b, i, 0)
    mod_block = (1,) + gt1.shape[1:]
    tri = jnp.asarray(np.tril(np.ones((tm, tm), np.float32), -1), BF16)
    consts = list(weights) + [tri, cnt0]
    return pl.pallas_call(
        functools.partial(_merge_kernel, dm=dm),
        grid=(nb, s // tm),
        in_specs=[pl.BlockSpec((1, tm, dm), row),
                  pl.BlockSpec((1, tm, py.shape[-1]), row),
                  pl.BlockSpec((1, tm, ay.shape[-1]), row),
                  pl.BlockSpec((1, tm, 2 * dm), row),
                  pl.BlockSpec(mod_block, mod_map), pl.BlockSpec(mod_block, mod_map), pl.BlockSpec(mod_block, mod_map)]
                 + [_full_spec(w) for w in consts],
        out_specs=[pl.BlockSpec((1, tm, dm), row), pl.BlockSpec((1, tm, dm), row),
                   pl.BlockSpec((1, tm, LANES), row), pl.BlockSpec((1, LANES), lambda b, i: (0, 0))],
        out_shape=[jax.ShapeDtypeStruct((nb, s, dm), F32), jax.ShapeDtypeStruct((nb, s, dm), F32),
                   jax.ShapeDtypeStruct((nb, s, LANES), F32), jax.ShapeDtypeStruct((1, LANES), F32)],
        compiler_params=_cparams("arbitrary", "arbitrary"),
        name="merge",
    )(x3, py, ay, mg, gt1, sh2, sc2, *consts)


def _dispatch_kernel(dest_ref, h_ref, xs_in, xs_out, sem, *, tm):
    del xs_in

    def row_copy(r, d):
        return pltpu.make_async_copy(h_ref.at[pl.ds(r, 1), :], xs_out.at[pl.ds(d, 1), :], sem)

    def start(r, c):
        for k in range(TOP_K):
            row_copy(r, dest_ref[r * TOP_K + k]).start()
        return c

    def wait(r, c):
        for k in range(TOP_K):
            row_copy(0, 0).wait()
        return c

    lax.fori_loop(0, tm, start, 0)
    lax.fori_loop(0, tm, wait, 0)


def _dispatch(dest, h2, xs, tm):
    t, dm = h2.shape
    return pl.pallas_call(
        functools.partial(_dispatch_kernel, tm=tm),
        grid=(t // tm,),
        in_specs=[pl.BlockSpec((tm * TOP_K,), lambda i: (i,), memory_space=pltpu.SMEM),
                  pl.BlockSpec((tm, dm), lambda i: (i, 0)),
                  pl.BlockSpec(memory_space=pl.ANY)],
        out_specs=pl.BlockSpec(memory_space=pl.ANY),
        out_shape=jax.ShapeDtypeStruct(xs.shape, xs.dtype),
        scratch_shapes=[pltpu.SemaphoreType.DMA(())],
        input_output_aliases={2: 0},
        compiler_params=_cparams("arbitrary"),
        name="dispatch",
    )(dest, h2, xs)


def _expert_kernel(be_ref, nv_ref, x_ref, wgu_ref, bgu_ref, wd_ref, bd_ref, y_ref, wgu_bf, wd_bf, *, dff):
    i = pl.program_id(0)
    live = i < nv_ref[0]
    fresh = (i == 0) | (be_ref[i] != be_ref[jnp.maximum(i - 1, 0)])

    @pl.when(live & fresh)
    def _():
        wgu_bf[...] = wgu_ref[0].astype(BF16)
        wd_bf[...] = wd_ref[0].astype(BF16)

    @pl.when(live)
    def _():
        gu = _dot(x_ref[...].astype(BF16), wgu_bf[...]) + bgu_ref[0]
        gate = jnp.minimum(gu[:, :dff], SWIGLU_LIMIT)
        up = jnp.clip(gu[:, dff:], -SWIGLU_LIMIT, SWIGLU_LIMIT)
        act = (up + 1.0) * gate * jax.nn.sigmoid(SWIGLU_ALPHA * gate)
        y_ref[...] = _dot(act.astype(BF16), wd_bf[...]) + bd_ref[0]

    @pl.when(jnp.logical_not(live))
    def _():
        y_ref[...] = jnp.zeros(y_ref.shape, F32)


def _experts(blk_e, n_live, xs, w_gu, b_gu, w_down, b_down):
    rows, dm = xs.shape
    n_exp, _, dff2 = w_gu.shape
    dff = dff2 // 2
    nblk = rows // MOE_ROWS
    xmap = lambda i, be, nv: (jnp.minimum(i, nv[0] - 1), 0)
    emap = lambda i, be, nv: (be[i], 0, 0)
    return pl.pallas_call(
        functools.partial(_expert_kernel, dff=dff),
        grid_spec=pltpu.PrefetchScalarGridSpec(
            num_scalar_prefetch=2,
            grid=(nblk,),
            in_specs=[pl.BlockSpec((MOE_ROWS, dm), xmap),
                      pl.BlockSpec((1, dm, dff2), emap),
                      pl.BlockSpec((1, 1, dff2), emap),
                      pl.BlockSpec((1, dff, dm), emap),
                      pl.BlockSpec((1, 1, dm), emap)],
            out_specs=pl.BlockSpec((MOE_ROWS, dm), lambda i, be, nv: (i, 0)),
            scratch_shapes=[pltpu.VMEM((dm, dff2), BF16), pltpu.VMEM((dff, dm), BF16)]),
        out_shape=jax.ShapeDtypeStruct((rows, dm), F32),
        compiler_params=_cparams("arbitrary"),
        name="experts",
    )(blk_e, n_live, xs, w_gu, b_gu.reshape(n_exp, 1, dff2), w_down, b_down.reshape(n_exp, 1, dm))


def _final_kernel(dest_ref, route_ref, x1_ref, gt2_ref, gpost2_ref, ys_hbm, out_ref, rows_buf, sem, *, tm):
    def row_copy(r, k, d):
        return pltpu.make_async_copy(ys_hbm.at[pl.ds(d, 1), :], rows_buf.at[k, pl.ds(r, 1), :], sem)

    def start(r, c):
        for k in range(TOP_K):
            row_copy(r, k, dest_ref[r * TOP_K + k]).start()
        return c

    def wait(r, c):
        for k in range(TOP_K):
            row_copy(0, k, 0).wait()
        return c

    lax.fori_loop(0, tm, start, 0)
    lax.fori_loop(0, tm, wait, 0)
    route = route_ref[0]
    lane = lax.broadcasted_iota(jnp.int32, route.shape, 1)
    f = None
    for k in range(TOP_K):
        gate = jnp.sum(jnp.where(lane == TOP_K + k, route, 0.0), axis=1, keepdims=True)
        term = gate * rows_buf[k]
        f = term if f is None else f + term
    out_ref[0] = x1_ref[0] + gt2_ref[0] * _rms(f, gpost2_ref[...])


def _final(dest, route, x1, gt2, mod_map, g_post2, ys, tm):
    nb, s, dm = x1.shape
    nt = s // tm
    row = lambda b, i: (b, i, 0)
    mod_block = (1,) + gt2.shape[1:]
    return pl.pallas_call(
        functools.partial(_final_kernel, tm=tm),
        grid=(nb, nt),
        in_specs=[pl.BlockSpec((tm * TOP_K,), lambda b, i: (b * nt + i,), memory_space=pltpu.SMEM),
                  pl.BlockSpec((1, tm, LANES), row),
                  pl.BlockSpec((1, tm, dm), row),
                  pl.BlockSpec(mod_block, mod_map),
                  _full_spec(g_post2),
                  pl.BlockSpec(memory_space=pl.ANY)],
        out_specs=pl.BlockSpec((1, tm, dm), row),
        out_shape=jax.ShapeDtypeStruct((nb, s, dm), F32),
        scratch_shapes=[pltpu.VMEM((TOP_K, tm, dm), F32), pltpu.SemaphoreType.DMA(())],
        compiler_params=_cparams("arbitrary", "arbitrary"),
        name="final",
    )(dest, route, x1, gt2, g_post2, ys)


def _transpose_rows(kv):
    lead = kv.shape[:-4]
    n = len(lead)
    perm = tuple(range(n)) + (n + 1, n + 2, n + 3, n)
    return kv.transpose(perm).reshape(lead + (KV_ROWS, kv.shape[-4]))


def _untranspose_rows(kvt):
    lead = kvt.shape[:-2]
    n = len(lead)
    x = kvt.reshape(lead + (2, N_KV_HEADS, HEAD_DIM, kvt.shape[-1]))
    return x.transpose(tuple(range(n)) + (n + 3, n, n + 1, n + 2))


def _layer(xp, xs, cache_cmp, cache_sel, st_win, st_pool, page_table, cp, cs, w):
    nb, s, dm = xp.shape
    bd, qn, _ = xs.shape
    ts_ = bd * qn
    n_pages = page_table.shape[1]
    assert cache_cmp.shape[1] == LANES, "page size must equal the lane count"
    past = n_pages * LANES
    tm_p = 256
    tm_s = min(256, ts_)
    assert s % tm_p == 0 and ts_ % tm_s == 0 and qn % SUBLANES == 0 and qn <= LANES
    n_exp = w['router_w'].shape[1]
    assert n_exp <= LANES

    c_all = jnp.concatenate([cp, cs], axis=0)
    c_all = jnp.pad(c_all, ((0, -c_all.shape[0] % SUBLANES), (0, 0)))
    mods = _ada(c_all, w['w_ada'], w['b_ada'][None])
    mp = mods[:nb].reshape(nb, 6, 1, dm)
    mod_p = [mp[:, j] for j in range(6)]
    ms = jnp.repeat(mods[nb:nb + bd], qn, axis=0).reshape(ts_ // tm_s, tm_s, 6, dm)
    mod_s = [ms[:, :, j] for j in range(6)]
    map_p = lambda b, i: (b, 0, 0)
    map_s = lambda b, i: (i, 0, 0)

    wa, wkvt, pw, aw, order = _prep_in_weights(w['w_in'])
    g_pre1 = w['g_pre1'][None]
    u_p, q_p, mg_p, ng_p, kvt_p = _proj(xp, mod_p[1], mod_p[0], map_p, g_pre1, wa, wkvt, tm_p, pw, aw)
    xs3 = xs.reshape(1, ts_, dm)
    u_s, q_s, mg_s, ng_s, kvt_s = _proj(xs3, mod_s[1], mod_s[0], map_s, g_pre1, wa, wkvt, tm_s, pw, aw)

    cw = _prep_cmp_weights(w['cmp_pe'], w['cmp_w1'], w['cmp_b1'], w['cmp_w2'])
    page_tab = page_table.reshape(-1).astype(jnp.int32)
    ckvt_p = _compress_prompt(kvt_p, cw)
    ckvt_s = _compress_paged(page_tab, n_pages, _transpose_rows(cache_cmp), cw)
    attn_p = _nsa_prompt(q_p, ng_p, ckvt_p, kvt_p, LANES)
    new_t = kvt_s[0].reshape(3 * KV_ROWS, bd, qn).transpose(1, 0, 2)
    new_pad = jnp.pad(new_t, ((0, 0), (0, 0), (0, LANES - qn)))
    attn_s = _nsa_sample(page_tab, n_pages, q_s.reshape(bd, qn, aw), ng_s.reshape(bd, qn, LANES), ckvt_s,
                         _transpose_rows(cache_sel), new_pad[:, KV_ROWS:2 * KV_ROWS], _transpose_rows(st_win),
                         new_pad[:, 2 * KV_ROWS:])

    pool_w, pool_scale = w['pool_w'], w['pool_scale'][None]
    hpt = tm_p // POOL_HIST
    py_p = _pool(u_p, lambda b, i: (b, jnp.maximum(i * hpt - 1, 0), 0), u_p, pool_w, pool_scale, tm_p, 0, True)
    p_buf = st_pool.shape[1]
    assert p_buf == max(POOL_WINDOWS) - 1
    u_sb = u_s.reshape(bd, qn, pw)
    hist_s = jnp.pad(st_pool, ((0, 0), (POOL_HIST - p_buf, 0), (0, 0)))
    py_s = _pool(hist_s, lambda b, i: (b, 0, 0), u_sb, pool_w, pool_scale, qn, past, False)

    rw = jnp.pad(w['router_w'], ((0, 0), (0, LANES - n_exp)))
    rwh = rw.astype(BF16)
    rwl = (rw - rwh.astype(F32)).astype(BF16)
    rb = jnp.pad(w['router_b'], (0, LANES - n_exp), constant_values=NEG_BIG)[None]
    wba = w['w_br_attn'].reshape(N_HEADS, HEAD_DIM, dm)[order].reshape(aw, dm).astype(BF16)
    mweights = [w['g_post1'][None], w['g_pre2'][None], w['w_br_pool'].astype(BF16), wba, w['w_o'].astype(BF16),
                rwh, rwl, rb]
    cnt0 = jnp.zeros((1, LANES), F32)
    x1_p, h2_p, route_p, cnt_p = _merge(xp, py_p, attn_p, mg_p, mod_p[2], mod_p[3], mod_p[4], map_p, mweights,
                                        cnt0, tm_p)
    x1_s, h2_s, route_s, cnt_a = _merge(xs3, py_s.reshape(1, ts_, pw), attn_s.reshape(1, ts_, aw), mg_s, mod_s[2],
                                        mod_s[3], mod_s[4], map_s, mweights, cnt_p, tm_s)

    tp_ = nb * s
    route = jnp.concatenate([route_p.reshape(tp_, LANES), route_s.reshape(ts_, LANES)], axis=0)
    eidx = route[:, 0:TOP_K].astype(jnp.int32)
    rank = route[:, 2 * TOP_K:3 * TOP_K].astype(jnp.int32)
    counts = cnt_a[0, :n_exp].astype(jnp.int32)
    padded = (counts + MOE_ROWS - 1) // MOE_ROWS * MOE_ROWS
    pend = jnp.cumsum(padded)
    dest = ((pend - padded)[eidx] + rank).reshape(-1)
    n_blocks = (tp_ + ts_) * TOP_K // MOE_ROWS + n_exp
    blk_start = jnp.arange(n_blocks, dtype=jnp.int32) * MOE_ROWS
    blk_e = jnp.minimum(jnp.sum((pend[None, :] <= blk_start[:, None]).astype(jnp.int32), axis=1), n_exp - 1)
    n_live = (pend[-1:] // MOE_ROWS).astype(jnp.int32)

    xs_rows = jnp.zeros((n_blocks * MOE_ROWS, dm), F32)
    xs_rows = _dispatch(dest[:tp_ * TOP_K], h2_p.reshape(tp_, dm), xs_rows, tm_p)
    xs_rows = _dispatch(dest[tp_ * TOP_K:], h2_s.reshape(ts_, dm), xs_rows, tm_s)
    ys = _experts(blk_e, n_live, xs_rows, w['w_gu'], w['b_gu'], w['w_down'], w['b_down'])
    g_post2 = w['g_post2'][None]
    y_p = _final(dest[:tp_ * TOP_K], route_p, x1_p, mod_p[5], map_p, g_post2, ys, tm_p)
    y_s = _final(dest[tp_ * TOP_K:], route_s, x1_s, mod_s[5], map_s, g_post2, ys, tm_s)

    wlen = min(WINDOW, s)
    kv_p = [_untranspose_rows(kvt_p[:, j * KV_ROWS:(j + 1) * KV_ROWS]) for j in range(2)]
    kvw_p = _untranspose_rows(kvt_p[:, 2 * KV_ROWS:, s - wlen:])
    pool_p = u_p[:, s - min(POOL_HIST - 1, s):]
    kv_s = [_untranspose_rows(new_t[:, j * KV_ROWS:(j + 1) * KV_ROWS]) for j in range(3)]
    w_buf = st_win.shape[1]
    kvw_s = jnp.concatenate([st_win, kv_s[2]], axis=1)[:, -w_buf:]
    pool_s = jnp.concatenate([st_pool, u_sb], axis=1)[:, -p_buf:]
    return y_p, y_s.reshape(bd, qn, dm), (kv_p[0], kv_p[1], kvw_p, pool_p, kv_s[0], kv_s[1], kvw_s, pool_s)


def kernel(x_prompt, x_sample, cache_kv_cmp, cache_kv_sel, state_kv_win, state_pool, page_table, c_prompt, c_sample,
           w_ada, b_ada, g_pre1, g_post1, g_pre2, g_post2, w_in, pool_w, pool_scale, cmp_pe, cmp_w1, cmp_b1, cmp_w2,
           w_br_pool, w_br_attn, w_o, router_w, router_b, w_gu, b_gu, w_down, b_down):
    names = ('w_ada', 'b_ada', 'g_pre1', 'g_post1', 'g_pre2', 'g_post2', 'w_in', 'pool_w', 'pool_scale', 'cmp_pe',
             'cmp_w1', 'cmp_b1', 'cmp_w2', 'w_br_pool', 'w_br_attn', 'w_o', 'router_w', 'router_b', 'w_gu', 'b_gu',
             'w_down', 'b_down')
    stacked = (w_ada, b_ada, g_pre1, g_post1, g_pre2, g_post2, w_in, pool_w, pool_scale, cmp_pe, cmp_w1, cmp_b1,
               cmp_w2, w_br_pool, w_br_attn, w_o, router_w, router_b, w_gu, b_gu, w_down, b_down)
    xp, xs = x_prompt, x_sample
    states = []
    for l in range(w_ada.shape[0]):
        w = {n: a[l] for n, a in zip(names, stacked)}
        xp, xs, st = _layer(xp, xs, cache_kv_cmp[l], cache_kv_sel[l], state_kv_win[l], state_pool[l], page_table,
                            c_prompt, c_sample, w)
        states.append(st)
    return (xp, xs) + tuple(jnp.stack([st[j] for st in states]) for j in range(8))
```

```python
import functools

import jax
import jax.numpy as jnp
import numpy as np
from jax import lax
from jax.experimental import pallas as pl
from jax.experimental.pallas import tpu as pltpu

F32 = jnp.float32
BF16 = jnp.bfloat16

POOL_WINDOWS = (2, 4, 8, 16)
N_HEADS = 8
N_KV_HEADS = 2
HEAD_DIM = 64
GQ = N_HEADS // N_KV_HEADS
CMP_BLOCK = 32
CMP_STRIDE = 16
SEL_BLOCK = 64
SUB_PER_SEL = SEL_BLOCK // CMP_STRIDE
N_SEL = 16
N_LOCAL = 2
WINDOW = 512
TOP_K = 4
SWIGLU_LIMIT = 7.0
SWIGLU_ALPHA = 1.702
RMS_EPS = 1e-6

LANES = 128
SUBLANES = 8
VMEM_LIMIT = 56 * 1024 * 1024
NEG_BIG = -1e30
M_INIT = -1e29
MXU_DEPTH = 256
FAR = 1 << 30
KV_ROWS = 2 * N_KV_HEADS * HEAD_DIM
HALF = N_KV_HEADS * HEAD_DIM
POOL_HIST = 16
MOE_ROWS = 256
SAMPLE_TILE_PAGES = 64


def _cparams(*sem):
    return pltpu.CompilerParams(dimension_semantics=sem, vmem_limit_bytes=VMEM_LIMIT)


def _full_spec(w):
    nd = w.ndim
    return pl.BlockSpec(w.shape, lambda *a: (0,) * nd)


def _dot(a, b):
    return jnp.dot(a, b, preferred_element_type=F32)


def _dot_nt(a, b):
    return lax.dot_general(a, b, (((1,), (1,)), ((), ())), preferred_element_type=F32)


def _split3(x):
    hi = x.astype(BF16)
    r1 = x - hi.astype(F32)
    mid = r1.astype(BF16)
    lo = (r1 - mid.astype(F32)).astype(BF16)
    return hi, mid, lo


def _dot_exact_rhs(x, m_bf16):
    hi, mid, lo = _split3(x)
    return _dot(hi, m_bf16) + _dot(mid, m_bf16) + _dot(lo, m_bf16)


def _rms(x, g):
    return x * lax.rsqrt(jnp.mean(x * x, axis=-1, keepdims=True) + RMS_EPS) * g


def _ada_kernel(c_ref, w_ref, b_ref, o_ref):
    o_ref[...] = _dot(c_ref[...].astype(BF16), w_ref[...].astype(BF16)) + b_ref[...]


def _ada(c_all, w_ada, b_ada):
    m, d = c_all.shape
    n = w_ada.shape[1]
    tn = 1024
    return pl.pallas_call(
        _ada_kernel,
        grid=(n // tn,),
        in_specs=[pl.BlockSpec((m, d), lambda j: (0, 0)),
                  pl.BlockSpec((d, tn), lambda j: (0, j)),
                  pl.BlockSpec((1, tn), lambda j: (0, j))],
        out_specs=pl.BlockSpec((m, tn), lambda j: (0, j)),
        out_shape=jax.ShapeDtypeStruct((m, n), F32),
        compiler_params=_cparams("arbitrary"),
        name="ada",
    )(c_all, w_ada, b_ada)


def _proj_kernel(x_ref, sc_ref, sh_ref, g_ref, wa_ref, wkv_ref,
                 u_ref, q_ref, mg_ref, ng_ref, kvt_ref, *, pw, aw, dm):
    hb = (_rms(x_ref[0], g_ref[...]) * (1.0 + sc_ref[0]) + sh_ref[0]).astype(BF16)
    u_ref[0] = _dot(hb, wa_ref[:, 0:pw])
    q_ref[0] = _dot(hb, wa_ref[:, pw:pw + aw])
    o = pw + aw
    mg_ref[0] = jax.nn.sigmoid(_dot(hb, wa_ref[:, o:o + 2 * dm]))
    ng_ref[0] = jax.nn.sigmoid(_dot(hb, wa_ref[:, o + 2 * dm:o + 2 * dm + LANES]))
    kvt_ref[0] = _dot_nt(wkv_ref[...], hb)


def _proj(x3, sc, sh, mod_map, g_pre1, wa, wkvt, tm, pw, aw):
    nb, s, dm = x3.shape
    nt = s // tm
    kern = functools.partial(_proj_kernel, pw=pw, aw=aw, dm=dm)
    mod_block = (1,) + sc.shape[1:]
    row = lambda b, i: (b, i, 0)
    return pl.pallas_call(
        kern,
        grid=(nb, nt),
        in_specs=[pl.BlockSpec((1, tm, dm), row),
                  pl.BlockSpec(mod_block, mod_map),
                  pl.BlockSpec(mod_block, mod_map),
                  _full_spec(g_pre1), _full_spec(wa), _full_spec(wkvt)],
        out_specs=[pl.BlockSpec((1, tm, pw), row),
                   pl.BlockSpec((1, tm, aw), row),
                   pl.BlockSpec((1, tm, 2 * dm), row),
                   pl.BlockSpec((1, tm, LANES), row),
                   pl.BlockSpec((1, 3 * KV_ROWS, tm), lambda b, i: (b, 0, i))],
        out_shape=[jax.ShapeDtypeStruct((nb, s, pw), F32),
                   jax.ShapeDtypeStruct((nb, s, aw), F32),
                   jax.ShapeDtypeStruct((nb, s, 2 * dm), F32),
                   jax.ShapeDtypeStruct((nb, s, LANES), F32),
                   jax.ShapeDtypeStruct((nb, 3 * KV_ROWS, s), F32)],
        compiler_params=_cparams("arbitrary", "arbitrary"),
        name="proj",
    )(x3, sc, sh, g_pre1, wa, wkvt)


def _prep_in_weights(w_in):
    dm = w_in.shape[0]
    pw = dm // 2
    aw = N_HEADS * HEAD_DIM
    off_kvc = pw + aw
    off_ng = off_kvc + 3 * KV_ROWS
    off_mg = off_ng + 3 * N_HEADS
    order = np.array([k * GQ + g for g in range(GQ) for k in range(N_KV_HEADS)])
    wq = w_in[:, pw:off_kvc].reshape(dm, N_HEADS, HEAD_DIM)[:, order].reshape(dm, aw) * (HEAD_DIM ** -0.5)
    wng = jnp.pad(w_in[:, off_ng:off_mg], ((0, 0), (0, LANES - 3 * N_HEADS)))
    wa = jnp.concatenate([w_in[:, :pw], wq, w_in[:, off_mg:], wng], axis=1).astype(BF16)
    wkvt = w_in[:, off_kvc:off_ng].T.astype(BF16)
    return wa, wkvt, pw, aw, order


def _compress_core(get_page, n_pages, perm_ref, wp_ref, w2t_ref, pe_ref, w1f_ref, b1_ref, s2_ref, out_ref):
    sub_pp = LANES // CMP_STRIDE
    n_sub = n_pages * sub_pp
    r_cmp = CMP_BLOCK // CMP_STRIDE
    hid = w1f_ref.shape[-1]

    def page_body(p, carry):
        pg = get_page(p).astype(BF16)
        t = _dot_nt(perm_ref[...], pg)
        row0 = pl.multiple_of(p * sub_pp, sub_pp)
        for s in range(2):
            for j in range(CMP_STRIDE):
                s2_ref[s, j // 2, pl.ds(row0, sub_pp), (j % 2) * HALF:(j % 2 + 1) * HALF] = (
                    t[sub_pp * j:sub_pp * (j + 1), s * HALF:(s + 1) * HALF])
        return carry

    lax.fori_loop(0, n_pages, page_body, 0, unroll=4)
    for s in range(2):
        acc = None
        for jp in range(CMP_STRIDE // 2):
            d = _dot(s2_ref[s, jp].astype(BF16), wp_ref[s, jp])
            acc = d if acc is None else acc + d
        bias = _dot(pe_ref[s], w1f_ref[s])[0:1] + b1_ref[s]
        bias = jnp.concatenate([bias] * N_KV_HEADS, axis=1)
        w = N_KV_HEADS * hid
        hpre = acc[:, :w] + bias
        for r in range(1, r_cmp):
            hpre = hpre + pltpu.roll(acc[:, r * w:(r + 1) * w], n_sub - r, axis=0)
        g = jax.nn.gelu(hpre).astype(BF16)
        out_ref[s * HALF:(s + 1) * HALF, :] = _dot_nt(w2t_ref[s], g)


def _compress_prompt_kernel(kv_ref, perm_ref, wp_ref, w2t_ref, pe_ref, w1f_ref, b1_ref, out_ref, s2_ref, *, n_pages):
    def get_page(p):
        return kv_ref[0, :, pl.ds(pl.multiple_of(p * LANES, LANES), LANES)]
    _compress_core(get_page, n_pages, perm_ref, wp_ref, w2t_ref, pe_ref, w1f_ref, b1_ref, s2_ref, out_ref.at[0])


def _page_fetch(pt_ref, cache_hbm, buf, sem, b, slot, n_pages):
    def start(p, c):
        pltpu.make_async_copy(cache_hbm.at[pt_ref[b * n_pages + p]], buf.at[slot, p], sem.at[slot]).start()
        return c
    lax.fori_loop(0, n_pages, start, 0)


def _page_wait(cache_hbm, buf, sem, slot, n_pages):
    def wait(p, c):
        pltpu.make_async_copy(cache_hbm.at[0], buf.at[slot, p], sem.at[slot]).wait()
        return c
    lax.fori_loop(0, n_pages, wait, 0)


def _paged_prologue(pt_ref, cache_hbm, buf, sem, n_pages):
    b = pl.program_id(0)
    slot = b % 2

    @pl.when(b == 0)
    def _():
        _page_fetch(pt_ref, cache_hbm, buf, sem, 0, 0, n_pages)

    @pl.when(b + 1 < pl.num_programs(0))
    def _():
        _page_fetch(pt_ref, cache_hbm, buf, sem, b + 1, 1 - slot, n_pages)

    _page_wait(cache_hbm, buf, sem, slot, n_pages)
    return slot


def _compress_paged_kernel(pt_ref, cache_hbm, perm_ref, wp_ref, w2t_ref, pe_ref, w1f_ref, b1_ref, out_ref,
                           buf, sem, s2_ref, *, n_pages):
    slot = _paged_prologue(pt_ref, cache_hbm, buf, sem, n_pages)
    _compress_core(lambda p: buf[slot, p], n_pages, perm_ref, wp_ref, w2t_ref, pe_ref, w1f_ref, b1_ref,
                   s2_ref, out_ref.at[0])


def _prep_cmp_weights(cmp_pe, cmp_w1, cmp_b1, cmp_w2):
    hid = cmp_w1.shape[-1]
    r_cmp = CMP_BLOCK // CMP_STRIDE
    eye = jnp.eye(N_KV_HEADS, dtype=F32)
    w1r = cmp_w1.reshape(2, r_cmp, CMP_STRIDE // 2, 2, HEAD_DIM, hid)
    wp = jnp.einsum('srpjdh,kc->spjkdrch', w1r, eye).reshape(
        2, CMP_STRIDE // 2, 2 * HALF, r_cmp * N_KV_HEADS * hid).astype(BF16)
    w2t = jnp.einsum('shd,kc->skdch', cmp_w2, eye).reshape(2, HALF, N_KV_HEADS * hid).astype(BF16)
    pe = jnp.broadcast_to(cmp_pe.reshape(2, 1, CMP_BLOCK * HEAD_DIM),
                          (2, SUBLANES, CMP_BLOCK * HEAD_DIM)).astype(BF16)
    w1f = cmp_w1.reshape(2, CMP_BLOCK * HEAD_DIM, hid).astype(BF16)
    b1 = cmp_b1.reshape(2, 1, hid)
    sub_pp = LANES // CMP_STRIDE
    x = np.arange(LANES)
    perm = np.zeros((LANES, LANES), np.float32)
    perm[x, (x % sub_pp) * CMP_STRIDE + x // sub_pp] = 1.0
    return (jnp.asarray(perm, BF16), wp, w2t, pe, w1f, b1)


def _compress_prompt(kvt, cw):
    nb, _, s = kvt.shape
    n_pages = s // LANES
    n_sub = s // CMP_STRIDE
    return pl.pallas_call(
        functools.partial(_compress_prompt_kernel, n_pages=n_pages),
        grid=(nb,),
        in_specs=[pl.BlockSpec((1, KV_ROWS, s), lambda b: (b, 0, 0))] + [_full_spec(w) for w in cw],
        out_specs=pl.BlockSpec((1, KV_ROWS, n_sub), lambda b: (b, 0, 0)),
        out_shape=jax.ShapeDtypeStruct((nb, KV_ROWS, n_sub), F32),
        scratch_shapes=[pltpu.VMEM((2, CMP_STRIDE // 2, n_sub, 2 * HALF), F32)],
        compiler_params=_cparams("arbitrary"),
        name="compress_prompt",
    )(kvt, *cw)


def _compress_paged(page_tab, n_pages, cache_t, cw):
    nb = page_tab.shape[0] // n_pages
    n_sub = n_pages * (LANES // CMP_STRIDE)
    return pl.pallas_call(
        functools.partial(_compress_paged_kernel, n_pages=n_pages),
        grid_spec=pltpu.PrefetchScalarGridSpec(
            num_scalar_prefetch=1,
            grid=(nb,),
            in_specs=[pl.BlockSpec(memory_space=pl.ANY)] + [_full_spec(w) for w in cw],
            out_specs=pl.BlockSpec((1, KV_ROWS, n_sub), lambda b, pt: (b, 0, 0)),
            scratch_shapes=[pltpu.VMEM((2, n_pages, KV_ROWS, LANES), F32),
                            pltpu.SemaphoreType.DMA((2,)),
                            pltpu.VMEM((2, CMP_STRIDE // 2, n_sub, 2 * HALF), F32)]),
        out_shape=jax.ShapeDtypeStruct((nb, KV_ROWS, n_sub), F32),
        compiler_params=_cparams("arbitrary"),
        name="compress_paged",
    )(page_tab, cache_t, *cw)


def _stack_heads(fn):
    return jnp.concatenate([fn(k, g) for k in range(N_KV_HEADS) for g in range(GQ)], axis=0)


def _alibi_slope(k, g):
    return 2.0 ** (-8.0 * (k * GQ + g + 1) / N_HEADS)


def _nsa_block(q, sgate, t0, ckvt_ref, mmat_ref, bmat_ref, sel_tile, n_sel_tiles, sel_tk, sel_last, win_tile,
               m_ref, acc_ref, out_ref, *, qt, nblk_pad, n_pick):
    t_pos = t0 + lax.broadcasted_iota(jnp.int32, (qt, 1), 0)
    lane_half = lax.broadcasted_iota(jnp.int32, (qt, LANES), 1) // HEAD_DIM

    qp = _stack_heads(lambda k, g: jnp.where(lane_half == k, q[:, g * LANES:(g + 1) * LANES], 0.0)).astype(BF16)

    def alibi(distf):
        return _stack_heads(lambda k, g: _alibi_slope(k, g) * distf)

    def stack_masks(mk):
        return jnp.concatenate([mk[k] for k in range(N_KV_HEADS) for _ in range(GQ)], axis=0) > 0.5

    nc = ckvt_ref.shape[-1]
    c_end = lax.broadcasted_iota(jnp.int32, (1, nc), 1) * CMP_STRIDE + (CMP_BLOCK - 1)
    dist_c = t_pos - c_end
    valid_c = jnp.where(dist_c >= 0, 1.0, 0.0)
    mask_c = stack_masks([valid_c] * N_KV_HEADS)
    s = _dot(qp, ckvt_ref[0:HALF, :].astype(BF16)) - alibi(dist_c.astype(F32))
    s = jnp.where(mask_c, s, NEG_BIG)
    p = jnp.where(mask_c, jnp.exp(s - jnp.max(s, axis=1, keepdims=True)), 0.0)
    p = p / jnp.maximum(jnp.sum(p, axis=1, keepdims=True), 1e-30)
    o_c = _dot_nt(p.astype(BF16), ckvt_ref[HALF:2 * HALF, :].astype(BF16))

    blk = lax.broadcasted_iota(jnp.int32, (qt, nblk_pad), 1)
    cur = t_pos // SEL_BLOCK
    forced = (blk == 0) | ((blk > cur - N_LOCAL) & (blk <= cur))
    free = (blk >= 1) & (blk <= cur - N_LOCAL)
    blkf = blk.astype(F32)
    sel = []
    for k in range(N_KV_HEADS):
        imp = p[k * GQ * qt:(k * GQ + 1) * qt]
        for g in range(1, GQ):
            imp = imp + p[(k * GQ + g) * qt:(k * GQ + g + 1) * qt]
        score = jnp.where(free, _dot_exact_rhs(imp, mmat_ref[...]), -jnp.inf)
        chosen = jnp.where(forced, 1.0, 0.0)
        for _ in range(n_pick):
            best = jnp.max(score, axis=1, keepdims=True)
            idx = jnp.min(jnp.where(score == best, blkf, float(nblk_pad)), axis=1, keepdims=True)
            hit = blkf == idx
            chosen = jnp.where(hit, 1.0, chosen)
            score = jnp.where(hit, -jnp.inf, score)
        sel.append(chosen)

    rk = GQ * qt
    lane = lax.broadcasted_iota(jnp.int32, (1, LANES), 1)
    t_rows = jnp.concatenate([t_pos] * GQ, axis=0)

    def q_feat(k):
        parts = []
        for g in range(GQ):
            sl = _alibi_slope(k, g)
            base = (1 - k) * HEAD_DIM
            ext = jnp.where(lane == base, SEL_BLOCK * sl, jnp.where(lane == base + 1, sl, 0.0))
            parts.append(jnp.where(lane_half == k, q[:, g * LANES:(g + 1) * LANES], ext))
        return jnp.concatenate(parts, axis=0).astype(BF16)

    qf = [q_feat(k) for k in range(N_KV_HEADS)]

    def k_aug(kt, pos, use_sel):
        tk = kt.shape[-1]
        hi = lax.shift_right_arithmetic(pos, SEL_BLOCK.bit_length() - 1)
        lo = pos & (SEL_BLOCK - 1)
        r = lax.broadcasted_iota(jnp.int32, (HEAD_DIM, tk), 0)
        ext = jnp.where(r == 0, hi.astype(F32), jnp.where(r == 1, lo.astype(F32), 0.0)).astype(BF16)
        if use_sel:
            onehot = jnp.where(lax.broadcasted_iota(jnp.int32, (nblk_pad, 1), 0) == hi, 1.0, 0.0).astype(BF16)
        else:
            onehot = jnp.zeros((nblk_pad, tk), BF16)
        kb = kt.astype(BF16)
        return [jnp.concatenate([onehot, kb[0:HEAD_DIM], ext], axis=0),
                jnp.concatenate([onehot, ext, kb[HEAD_DIM:HALF]], axis=0)]

    m_ref[...] = jnp.full(m_ref.shape, M_INIT, F32)
    acc_ref[...] = jnp.zeros(acc_ref.shape, F32)

    def flash(br, qa, kt, vt, pos, use_sel, cond):
        ka = k_aug(kt, pos, use_sel)
        va = jnp.concatenate([vt.astype(BF16), jnp.ones((2 * SUBLANES, kt.shape[-1]), BF16)], axis=0)
        for k in range(N_KV_HEADS):
            sc = _dot(qa[k], ka[k])
            if cond is not None:
                sc = jnp.where(cond, sc, NEG_BIG)
            m_old = m_ref[br, k]
            m_new = jnp.maximum(m_old, jnp.max(sc, axis=1, keepdims=True))
            pr = jnp.exp(sc - m_new).astype(BF16)
            acc_ref[br, k] = jnp.exp(m_old - m_new) * acc_ref[br, k] + _dot_nt(pr, va)
            m_ref[br, k] = m_new

    def result(br):
        out = []
        for k in range(N_KV_HEADS):
            a = acc_ref[br, k]
            out.append(a[:, 0:HALF] / jnp.maximum(a[:, HALF:HALF + 1], 1e-30))
        return out

    qa_w = [jnp.concatenate([jnp.zeros((rk, nblk_pad), BF16), qf[k]], axis=1) for k in range(N_KV_HEADS)]
    kt, vt, pos = win_tile
    dist = t_rows - pos
    flash(1, qa_w, kt, vt, pos, False, (dist >= 0) & (dist <= WINDOW))

    qa_s = []
    for k in range(N_KV_HEADS):
        selneg = jnp.where(sel[k] > 0.5, 0.0, NEG_BIG).astype(BF16)
        qa_s.append(jnp.concatenate([jnp.concatenate([selneg] * GQ, axis=0), qf[k]], axis=1))

    def sel_step(j):
        kt, vt = sel_tile(j)
        flash(0, qa_s, kt, vt, j * sel_tk + lax.broadcasted_iota(jnp.int32, (1, sel_tk), 1), True, None)

    if isinstance(n_sel_tiles, int):
        for j in range(n_sel_tiles):
            sel_step(j)
    else:
        def pair_body(p, c):
            sel_step(2 * p)
            sel_step(2 * p + 1)
            return c

        lax.fori_loop(0, n_sel_tiles // 2, pair_body, 0)

        @pl.when(n_sel_tiles % 2 == 1)
        def _():
            sel_step(n_sel_tiles - 1)
    kt, vt, pos, use_sel = sel_last
    flash(0, qa_s, kt, vt, pos, use_sel, t_rows >= pos)
    o_s = result(0)
    o_w = result(1)

    o_c = [o_c[0:rk], o_c[rk:2 * rk]]
    g3 = _split3(sgate)
    for g in range(GQ):
        slab = None
        for c, o in enumerate((o_c, o_s, o_w)):
            bm = bmat_ref[g * 3 + c]
            gate = _dot(g3[0], bm) + _dot(g3[1], bm) + _dot(g3[2], bm)
            val = jnp.where(lane_half == 0, o[0][g * qt:(g + 1) * qt], o[1][g * qt:(g + 1) * qt])
            slab = gate * val if slab is None else slab + gate * val
        out_ref[:, g * LANES:(g + 1) * LANES] = slab


def _nsa_consts(nc, nblk_pad):
    j = np.arange(nc)[:, None]
    lo = SUB_PER_SEL * np.arange(nblk_pad)[None, :]
    mm = (np.where((j >= lo) & (j < lo + SUB_PER_SEL - 1), 2.0, 0.0)
          + np.where((j == lo - 1) | (j == lo + SUB_PER_SEL - 1), 1.0, 0.0))
    col = np.arange(LANES)[:, None]
    lane = np.arange(LANES)[None, :]
    bm = np.stack([(col == ((lane // HEAD_DIM) * GQ + g) * 3 + c)
                   for g in range(GQ) for c in range(3)]).astype(np.float32)
    return jnp.asarray(mm, BF16), jnp.asarray(bm, BF16)


def _lane_iota(n):
    return lax.broadcasted_iota(jnp.int32, (1, n), 1)


def _flash_scratch(qt, lead=()):
    rk = GQ * qt
    lead = (lead,) if isinstance(lead, int) else lead
    return [pltpu.VMEM(lead + (2, N_KV_HEADS, rk, 1), F32),
            pltpu.VMEM(lead + (2, N_KV_HEADS, rk, HALF + 2 * SUBLANES), F32)]


def _nsa_prompt_kernel(*refs, qt, tk, n_win, nblk_pad, n_pick):
    q_ref, ng_ref, ckvt_ref, kvs_ref = refs[:4]
    win_refs = refs[4:4 + n_win]
    mmat_ref, bmat_ref, out_ref, m_ref, acc_ref = refs[4 + n_win:]
    i = pl.program_id(1)

    def sel_tile(j):
        off = pl.multiple_of(j * tk, LANES)
        return kvs_ref[0, 0:HALF, pl.ds(off, tk)], kvs_ref[0, HALF:2 * HALF, pl.ds(off, tk)]

    n_past = (i * qt) // tk
    sel_last = sel_tile(n_past) + (n_past * tk + _lane_iota(tk), True)
    pieces = [w[0] for w in win_refs]
    pos = []
    for jj in range(n_win):
        bi = i - (n_win - 1) + jj
        pos.append(jnp.where(bi >= 0, bi * qt, -FAR) + _lane_iota(qt))
    win = (jnp.concatenate([p[0:HALF] for p in pieces], axis=1),
           jnp.concatenate([p[HALF:2 * HALF] for p in pieces], axis=1), jnp.concatenate(pos, axis=1))
    _nsa_block(q_ref[0], ng_ref[0], i * qt, ckvt_ref.at[0], mmat_ref, bmat_ref, sel_tile, n_past, tk, sel_last, win,
               m_ref, acc_ref, out_ref.at[0], qt=qt, nblk_pad=nblk_pad, n_pick=n_pick)


def _nsa_prompt(q, ng, ckvt, kvt, qt):
    nb, s, aw = q.shape
    nc = ckvt.shape[-1]
    nblk = -(-s // SEL_BLOCK)
    nblk_pad = -(-nblk // LANES) * LANES
    assert nblk_pad + HALF <= MXU_DEPTH, "selection mask + features must fit one MXU contraction pass"
    n_pick = max(min(N_SEL, nblk) - (N_LOCAL + 1), 0)
    n_win = WINDOW // qt + 1
    tk = min(4 * qt, s)
    assert s % tk == 0 and tk % qt == 0
    mmat, bmat = _nsa_consts(nc, nblk_pad)
    row = lambda b, i: (b, i, 0)
    win_specs = [pl.BlockSpec((1, KV_ROWS, qt), (lambda jj: (lambda b, i: (b, 2, jnp.maximum(i - (n_win - 1) + jj, 0))))(jj))
                 for jj in range(n_win)]
    return pl.pallas_call(
        functools.partial(_nsa_prompt_kernel, qt=qt, tk=tk, n_win=n_win, nblk_pad=nblk_pad, n_pick=n_pick),
        grid=(nb, s // qt),
        in_specs=[pl.BlockSpec((1, qt, aw), row),
                  pl.BlockSpec((1, qt, LANES), row),
                  pl.BlockSpec((1, KV_ROWS, nc), lambda b, i: (b, 0, 0)),
                  pl.BlockSpec((1, KV_ROWS, s), lambda b, i: (b, 1, 0))] + win_specs
                 + [_full_spec(mmat), _full_spec(bmat)],
        out_specs=pl.BlockSpec((1, qt, aw), row),
        out_shape=jax.ShapeDtypeStruct((nb, s, aw), F32),
        scratch_shapes=_flash_scratch(qt),
        compiler_params=_cparams("arbitrary", "arbitrary"),
        name="nsa_prompt",
    )(q, ng, ckvt, kvt, *([kvt] * n_win), mmat, bmat)


def _nsa_sample_kernel(pt_ref, q_ref, ng_ref, ckvt_ref, cache_hbm, ksn_ref, wst_ref, kwn_ref, mmat_ref, bmat_ref,
                       out_ref, buf, sem, m_ref, acc_ref, *, qt, nbs, n_pages, ppt, nblk_pad, n_pick):
    slot = _paged_prologue(pt_ref, cache_hbm, buf, sem, nbs * n_pages)
    past = n_pages * LANES
    wbuf = wst_ref.shape[-1]
    for bb in range(nbs):
        def sel_tile(j, bb=bb):
            p0 = bb * n_pages + j * ppt
            ks = [buf[slot, p0 + pp, 0:HALF, :] for pp in range(ppt)]
            vs = [buf[slot, p0 + pp, HALF:2 * HALF, :] for pp in range(ppt)]
            return jnp.concatenate(ks, axis=1), jnp.concatenate(vs, axis=1)

        sel_last = (ksn_ref[bb, 0:HALF, :], ksn_ref[bb, HALF:2 * HALF, :], past + _lane_iota(LANES), False)
        win = (jnp.concatenate([wst_ref[bb, 0:HALF, :], kwn_ref[bb, 0:HALF, :]], axis=1),
               jnp.concatenate([wst_ref[bb, HALF:2 * HALF, :], kwn_ref[bb, HALF:2 * HALF, :]], axis=1),
               past - wbuf + _lane_iota(wbuf + LANES))
        _nsa_block(q_ref[bb], ng_ref[bb], past, ckvt_ref.at[bb], mmat_ref, bmat_ref, sel_tile, n_pages // ppt,
                   ppt * LANES, sel_last, win, m_ref.at[bb], acc_ref.at[bb], out_ref.at[bb],
                   qt=qt, nblk_pad=nblk_pad, n_pick=n_pick)


def _nsa_sample(page_tab, n_pages, q, ng, ckvt, cache_t, ks_new, w_state, kw_new):
    nb, qt, aw = q.shape
    nc = ckvt.shape[-1]
    past = n_pages * LANES
    assert past % SEL_BLOCK == 0 and qt <= SEL_BLOCK
    nblk_pad = -(-(past // SEL_BLOCK) // LANES) * LANES
    assert nblk_pad + HALF <= MXU_DEPTH
    n_pick = max(min(N_SEL, past // SEL_BLOCK + 1) - (N_LOCAL + 1), 0)
    ppt = min(SAMPLE_TILE_PAGES, n_pages)
    nbs = 2 if nb % 2 == 0 else 1
    assert n_pages % ppt == 0
    mmat, bmat = _nsa_consts(nc, nblk_pad)
    b3 = lambda b, pt: (b, 0, 0)
    return pl.pallas_call(
        functools.partial(_nsa_sample_kernel, qt=qt, nbs=nbs, n_pages=n_pages, ppt=ppt, nblk_pad=nblk_pad,
                          n_pick=n_pick),
        grid_spec=pltpu.PrefetchScalarGridSpec(
            num_scalar_prefetch=1,
            grid=(nb // nbs,),
            in_specs=[pl.BlockSpec((nbs, qt, aw), b3),
                      pl.BlockSpec((nbs, qt, LANES), b3),
                      pl.BlockSpec((nbs, KV_ROWS, nc), b3),
                      pl.BlockSpec(memory_space=pl.ANY),
                      pl.BlockSpec((nbs, KV_ROWS, LANES), b3),
                      pl.BlockSpec((nbs, KV_ROWS, w_state.shape[-1]), b3),
                      pl.BlockSpec((nbs, KV_ROWS, LANES), b3),
                      _full_spec(mmat), _full_spec(bmat)],
            out_specs=pl.BlockSpec((nbs, qt, aw), b3),
            scratch_shapes=[pltpu.VMEM((2, nbs * n_pages, KV_ROWS, LANES), F32),
                            pltpu.SemaphoreType.DMA((2,))] + _flash_scratch(qt, nbs)),
        out_shape=jax.ShapeDtypeStruct((nb, qt, aw), F32),
        compiler_params=_cparams("arbitrary"),
        name="nsa_sample",
    )(page_tab, q, ng, ckvt, cache_t, ks_new, w_state, kw_new, mmat, bmat)


def _pool_kernel(hist_ref, cur_ref, pw_ref, ps_ref, out_ref, *, ts, pos_base, zero_first):
    i = pl.program_id(1)
    hist = hist_ref[0]
    if zero_first:
        hist = jnp.where(i == 0, 0.0, hist)
    cur = cur_ref[0]
    ext = jnp.concatenate([hist, cur], axis=0)
    pos = pos_base + i * ts + lax.broadcasted_iota(jnp.int32, (ts, 1), 0)
    ys = []
    for gi, w in enumerate(POOL_WINDOWS):
        lanes = slice(gi * LANES, (gi + 1) * LANES)
        acc = ext[:, lanes]
        step = 1
        while step < w:
            acc = acc + pltpu.roll(acc, step, axis=0)
            step *= 2
        cnt = jnp.minimum(pos + 1, w).astype(F32)
        dlt = acc[POOL_HIST:] / cnt - cur[:, lanes]
        ys.append(_dot(dlt.astype(BF16), pw_ref[gi].astype(BF16)))
    out_ref[0] = jnp.concatenate(ys, axis=1) * ps_ref[...]


def _pool(hist_arr, hist_map, u, pool_w, pool_scale, ts, pos_base, zero_first):
    nb, s, pw = u.shape
    assert pw == len(POOL_WINDOWS) * LANES and all(w & (w - 1) == 0 and w <= POOL_HIST for w in POOL_WINDOWS)
    row = lambda b, i: (b, i, 0)
    return pl.pallas_call(
        functools.partial(_pool_kernel, ts=ts, pos_base=pos_base, zero_first=zero_first),
        grid=(nb, s // ts),
        in_specs=[pl.BlockSpec((1, POOL_HIST, pw), hist_map),
                  pl.BlockSpec((1, ts, pw), row),
                  _full_spec(pool_w), _full_spec(pool_scale)],
        out_specs=pl.BlockSpec((1, ts, pw), row),
        out_shape=jax.ShapeDtypeStruct((nb, s, pw), F32),
        compiler_params=_cparams("arbitrary", "arbitrary"),
        name="pool",
    )(hist_arr, u, pool_w, pool_scale)


def _merge_kernel(x_ref, py_ref, ay_ref, mg_ref, gt1_ref, sh2_ref, sc2_ref, gpost1_ref, gpre2_ref,
                  wbp_ref, wba_ref, wo_ref, rwh_ref, rwl_ref, rb_ref, tri_ref, cnt0_ref,
                  x1_ref, h2_ref, route_ref, cnt_ref, *, dm):
    first = (pl.program_id(0) == 0) & (pl.program_id(1) == 0)

    @pl.when(first)
    def _():
        cnt_ref[...] = cnt0_ref[...]

    bp = _dot(py_ref[0].astype(BF16), wbp_ref[...])
    ba = _dot(ay_ref[0].astype(BF16), wba_ref[...])
    merged = mg_ref[0, :, 0:dm] * bp + mg_ref[0, :, dm:2 * dm] * ba
    mix = _dot(merged.astype(BF16), wo_ref[...])
    x1 = x_ref[0] + gt1_ref[0] * _rms(mix, gpost1_ref[...])
    h2 = _rms(x1, gpre2_ref[...]) * (1.0 + sc2_ref[0]) + sh2_ref[0]
    x1_ref[0] = x1
    h2_ref[0] = h2

    hh = h2.astype(BF16)
    hl = (h2 - hh.astype(F32)).astype(BF16)
    logits = _dot(hh, rwh_ref[...]) + _dot(hl, rwh_ref[...]) + _dot(hh, rwl_ref[...]) + rb_ref[...]
    tm = logits.shape[0]
    lane = lax.broadcasted_iota(jnp.int32, (tm, LANES), 1)
    lanef = lane.astype(F32)
    hits, vals, idxs = [], [], []
    for _ in range(TOP_K):
        best = jnp.max(logits, axis=1, keepdims=True)
        idx = jnp.min(jnp.where(logits == best, lanef, float(LANES)), axis=1, keepdims=True)
        hit = lanef == idx
        hits.append(hit)
        vals.append(best)
        idxs.append(idx)
        logits = jnp.where(hit, -jnp.inf, logits)
    ex = [jnp.exp(v - vals[0]) for v in vals]
    den = ex[0]
    for e in ex[1:]:
        den = den + e
    onehot = jnp.where(hits[0], 1.0, 0.0)
    for h in hits[1:]:
        onehot = onehot + jnp.where(h, 1.0, 0.0)
    before = _dot(tri_ref[...], onehot.astype(BF16)) + cnt_ref[...]
    route = jnp.zeros((tm, LANES), F32)
    for k in range(TOP_K):
        rank = jnp.sum(jnp.where(hits[k], before, 0.0), axis=1, keepdims=True)
        route = route + jnp.where(lane == k, idxs[k], 0.0)
        route = route + jnp.where(lane == TOP_K + k, ex[k] / den, 0.0)
        route = route + jnp.where(lane == 2 * TOP_K + k, rank, 0.0)
    route_ref[0] = route
    cnt_ref[...] = cnt_ref[...] + jnp.sum(onehot, axis=0, keepdims=True)


def _merge(x3, py, ay, mg, gt1, sh2, sc2, mod_map, weights, cnt0, tm):
    nb, s, dm = x3.shape
    row = lambda b, i: (b, i, 0)
    mod_block = (1,) + gt1.shape[1:]
    tri = jnp.asarray(np.tril(np.ones((tm, tm), np.float32), -1), BF16)
    consts = list(weights) + [tri, cnt0]
    return pl.pallas_call(
        functools.partial(_merge_kernel, dm=dm),
        grid=(nb, s // tm),
        in_specs=[pl.BlockSpec((1, tm, dm), row),
                  pl.BlockSpec((1, tm, py.shape[-1]), row),
                  pl.BlockSpec((1, tm, ay.shape[-1]), row),
                  pl.BlockSpec((1, tm, 2 * dm), row),
                  pl.BlockSpec(mod_block, mod_map), pl.BlockSpec(mod_block, mod_map), pl.BlockSpec(mod_block, mod_map)]
                 + [_full_spec(w) for w in consts],
        out_specs=[pl.BlockSpec((1, tm, dm), row), pl.BlockSpec((1, tm, dm), row),
                   pl.BlockSpec((1, tm, LANES), row), pl.BlockSpec((1, LANES), lambda b, i: (0, 0))],
        out_shape=[jax.ShapeDtypeStruct((nb, s, dm), F32), jax.ShapeDtypeStruct((nb, s, dm), F32),
                   jax.ShapeDtypeStruct((nb, s, LANES), F32), jax.ShapeDtypeStruct((1, LANES), F32)],
        compiler_params=_cparams("arbitrary", "arbitrary"),
        name="merge",
    )(x3, py, ay, mg, gt1, sh2, sc2, *consts)


def _dispatch_kernel(dest_ref, h_ref, xs_in, xs_out, sem, *, tm):
    del xs_in

    def row_copy(r, d):
        return pltpu.make_async_copy(h_ref.at[pl.ds(r, 1), :], xs_out.at[pl.ds(d, 1), :], sem)

    def start(r, c):
        for k in range(TOP_K):
            row_copy(r, dest_ref[r * TOP_K + k]).start()
        return c

    def wait(r, c):
        for k in range(TOP_K):
            row_copy(0, 0).wait()
        return c

    lax.fori_loop(0, tm, start, 0)
    lax.fori_loop(0, tm, wait, 0)


def _dispatch(dest, h2, xs, tm):
    t, dm = h2.shape
    return pl.pallas_call(
        functools.partial(_dispatch_kernel, tm=tm),
        grid=(t // tm,),
        in_specs=[pl.BlockSpec((tm * TOP_K,), lambda i: (i,), memory_space=pltpu.SMEM),
                  pl.BlockSpec((tm, dm), lambda i: (i, 0)),
                  pl.BlockSpec(memory_space=pl.ANY)],
        out_specs=pl.BlockSpec(memory_space=pl.ANY),
        out_shape=jax.ShapeDtypeStruct(xs.shape, xs.dtype),
        scratch_shapes=[pltpu.SemaphoreType.DMA(())],
        input_output_aliases={2: 0},
        compiler_params=_cparams("arbitrary"),
        name="dispatch",
    )(dest, h2, xs)


def _expert_kernel(be_ref, nv_ref, x_ref, wgu_ref, bgu_ref, wd_ref, bd_ref, y_ref, wgu_bf, wd_bf, *, dff):
    i = pl.program_id(0)
    live = i < nv_ref[0]
    fresh = (i == 0) | (be_ref[i] != be_ref[jnp.maximum(i - 1, 0)])

    @pl.when(live & fresh)
    def _():
        wgu_bf[...] = wgu_ref[0].astype(BF16)
        wd_bf[...] = wd_ref[0].astype(BF16)

    @pl.when(live)
    def _():
        gu = _dot(x_ref[...].astype(BF16), wgu_bf[...]) + bgu_ref[0]
        gate = jnp.minimum(gu[:, :dff], SWIGLU_LIMIT)
        up = jnp.clip(gu[:, dff:], -SWIGLU_LIMIT, SWIGLU_LIMIT)
        act = (up + 1.0) * gate * jax.nn.sigmoid(SWIGLU_ALPHA * gate)
        y_ref[...] = _dot(act.astype(BF16), wd_bf[...]) + bd_ref[0]

    @pl.when(jnp.logical_not(live))
    def _():
        y_ref[...] = jnp.zeros(y_ref.shape, F32)


def _experts(blk_e, n_live, xs, w_gu, b_gu, w_down, b_down):
    rows, dm = xs.shape
    n_exp, _, dff2 = w_gu.shape
    dff = dff2 // 2
    nblk = rows // MOE_ROWS
    xmap = lambda i, be, nv: (jnp.minimum(i, nv[0] - 1), 0)
    emap = lambda i, be, nv: (be[i], 0, 0)
    return pl.pallas_call(
        functools.partial(_expert_kernel, dff=dff),
        grid_spec=pltpu.PrefetchScalarGridSpec(
            num_scalar_prefetch=2,
            grid=(nblk,),
            in_specs=[pl.BlockSpec((MOE_ROWS, dm), xmap),
                      pl.BlockSpec((1, dm, dff2), emap),
                      pl.BlockSpec((1, 1, dff2), emap),
                      pl.BlockSpec((1, dff, dm), emap),
                      pl.BlockSpec((1, 1, dm), emap)],
            out_specs=pl.BlockSpec((MOE_ROWS, dm), lambda i, be, nv: (i, 0)),
            scratch_shapes=[pltpu.VMEM((dm, dff2), BF16), pltpu.VMEM((dff, dm), BF16)]),
        out_shape=jax.ShapeDtypeStruct((rows, dm), F32),
        compiler_params=_cparams("arbitrary"),
        name="experts",
    )(blk_e, n_live, xs, w_gu, b_gu.reshape(n_exp, 1, dff2), w_down, b_down.reshape(n_exp, 1, dm))


def _final_kernel(dest_ref, route_ref, x1_ref, gt2_ref, gpost2_ref, ys_hbm, out_ref, rows_buf, sem, *, tm):
    def row_copy(r, k, d):
        return pltpu.make_async_copy(ys_hbm.at[pl.ds(d, 1), :], rows_buf.at[k, pl.ds(r, 1), :], sem)

    def start(r, c):
        for k in range(TOP_K):
            row_copy(r, k, dest_ref[r * TOP_K + k]).start()
        return c

    def wait(r, c):
        for k in range(TOP_K):
            row_copy(0, k, 0).wait()
        return c

    lax.fori_loop(0, tm, start, 0)
    lax.fori_loop(0, tm, wait, 0)
    route = route_ref[0]
    lane = lax.broadcasted_iota(jnp.int32, route.shape, 1)
    f = None
    for k in range(TOP_K):
        gate = jnp.sum(jnp.where(lane == TOP_K + k, route, 0.0), axis=1, keepdims=True)
        term = gate * rows_buf[k]
        f = term if f is None else f + term
    out_ref[0] = x1_ref[0] + gt2_ref[0] * _rms(f, gpost2_ref[...])


def _final(dest, route, x1, gt2, mod_map, g_post2, ys, tm):
    nb, s, dm = x1.shape
    nt = s // tm
    row = lambda b, i: (b, i, 0)
    mod_block = (1,) + gt2.shape[1:]
    return pl.pallas_call(
        functools.partial(_final_kernel, tm=tm),
        grid=(nb, nt),
        in_specs=[pl.BlockSpec((tm * TOP_K,), lambda b, i: (b * nt + i,), memory_space=pltpu.SMEM),
                  pl.BlockSpec((1, tm, LANES), row),
                  pl.BlockSpec((1, tm, dm), row),
                  pl.BlockSpec(mod_block, mod_map),
                  _full_spec(g_post2),
                  pl.BlockSpec(memory_space=pl.ANY)],
        out_specs=pl.BlockSpec((1, tm, dm), row),
        out_shape=jax.ShapeDtypeStruct((nb, s, dm), F32),
        scratch_shapes=[pltpu.VMEM((TOP_K, tm, dm), F32), pltpu.SemaphoreType.DMA(())],
        compiler_params=_cparams("arbitrary", "arbitrary"),
        name="final",
    )(dest, route, x1, gt2, g_post2, ys)


def _transpose_rows(kv):
    lead = kv.shape[:-4]
    n = len(lead)
    perm = tuple(range(n)) + (n + 1, n + 2, n + 3, n)
    return kv.transpose(perm).reshape(lead + (KV_ROWS, kv.shape[-4]))


def _untranspose_rows(kvt):
    lead = kvt.shape[:-2]
    n = len(lead)
    x = kvt.reshape(lead + (2, N_KV_HEADS, HEAD_DIM, kvt.shape[-1]))
    return x.transpose(tuple(range(n)) + (n + 3, n, n + 1, n + 2))


def _layer(xp, xs, cache_cmp, cache_sel, st_win, st_pool, page_table, cp, cs, w):
    nb, s, dm = xp.shape
    bd, qn, _ = xs.shape
    ts_ = bd * qn
    n_pages = page_table.shape[1]
    assert cache_cmp.shape[1] == LANES, "page size must equal the lane count"
    past = n_pages * LANES
    tm_p = 256
    tm_s = min(256, ts_)
    assert s % tm_p == 0 and ts_ % tm_s == 0 and qn % SUBLANES == 0 and qn <= LANES
    n_exp = w['router_w'].shape[1]
    assert n_exp <= LANES

    c_all = jnp.concatenate([cp, cs], axis=0)
    c_all = jnp.pad(c_all, ((0, -c_all.shape[0] % SUBLANES), (0, 0)))
    mods = _ada(c_all, w['w_ada'], w['b_ada'][None])
    mp = mods[:nb].reshape(nb, 6, 1, dm)
    mod_p = [mp[:, j] for j in range(6)]
    ms = jnp.repeat(mods[nb:nb + bd], qn, axis=0).reshape(ts_ // tm_s, tm_s, 6, dm)
    mod_s = [ms[:, :, j] for j in range(6)]
    map_p = lambda b, i: (b, 0, 0)
    map_s = lambda b, i: (i, 0, 0)

    wa, wkvt, pw, aw, order = _prep_in_weights(w['w_in'])
    g_pre1 = w['g_pre1'][None]
    u_p, q_p, mg_p, ng_p, kvt_p = _proj(xp, mod_p[1], mod_p[0], map_p, g_pre1, wa, wkvt, tm_p, pw, aw)
    xs3 = xs.reshape(1, ts_, dm)
    u_s, q_s, mg_s, ng_s, kvt_s = _proj(xs3, mod_s[1], mod_s[0], map_s, g_pre1, wa, wkvt, tm_s, pw, aw)

    cw = _prep_cmp_weights(w['cmp_pe'], w['cmp_w1'], w['cmp_b1'], w['cmp_w2'])
    page_tab = page_table.reshape(-1).astype(jnp.int32)
    ckvt_p = _compress_prompt(kvt_p, cw)
    ckvt_s = _compress_paged(page_tab, n_pages, _transpose_rows(cache_cmp), cw)
    attn_p = _nsa_prompt(q_p, ng_p, ckvt_p, kvt_p, LANES)
    new_t = kvt_s[0].reshape(3 * KV_ROWS, bd, qn).transpose(1, 0, 2)
    new_pad = jnp.pad(new_t, ((0, 0), (0, 0), (0, LANES - qn)))
    attn_s = _nsa_sample(page_tab, n_pages, q_s.reshape(bd, qn, aw), ng_s.reshape(bd, qn, LANES), ckvt_s,
                         _transpose_rows(cache_sel), new_pad[:, KV_ROWS:2 * KV_ROWS], _transpose_rows(st_win),
                         new_pad[:, 2 * KV_ROWS:])

    pool_w, pool_scale = w['pool_w'], w['pool_scale'][None]
    hpt = tm_p // POOL_HIST
    py_p = _pool(u_p, lambda b, i: (b, jnp.maximum(i * hpt - 1, 0), 0), u_p, pool_w, pool_scale, tm_p, 0, True)
    p_buf = st_pool.shape[1]
    assert p_buf == max(POOL_WINDOWS) - 1
    u_sb = u_s.reshape(bd, qn, pw)
    hist_s = jnp.pad(st_pool, ((0, 0), (POOL_HIST - p_buf, 0), (0, 0)))
    py_s = _pool(hist_s, lambda b, i: (b, 0, 0), u_sb, pool_w, pool_scale, qn, past, False)

    rw = jnp.pad(w['router_w'], ((0, 0), (0, LANES - n_exp)))
    rwh = rw.astype(BF16)
    rwl = (rw - rwh.astype(F32)).astype(BF16)
    rb = jnp.pad(w['router_b'], (0, LANES - n_exp), constant_values=NEG_BIG)[None]
    wba = w['w_br_attn'].reshape(N_HEADS, HEAD_DIM, dm)[order].reshape(aw, dm).astype(BF16)
    mweights = [w['g_post1'][None], w['g_pre2'][None], w['w_br_pool'].astype(BF16), wba, w['w_o'].astype(BF16),
                rwh, rwl, rb]
    cnt0 = jnp.zeros((1, LANES), F32)
    x1_p, h2_p, route_p, cnt_p = _merge(xp, py_p, attn_p, mg_p, mod_p[2], mod_p[3], mod_p[4], map_p, mweights,
                                        cnt0, tm_p)
    x1_s, h2_s, route_s, cnt_a = _merge(xs3, py_s.reshape(1, ts_, pw), attn_s.reshape(1, ts_, aw), mg_s, mod_s[2],
                                        mod_s[3], mod_s[4], map_s, mweights, cnt_p, tm_s)

    tp_ = nb * s
    route = jnp.concatenate([route_p.reshape(tp_, LANES), route_s.reshape(ts_, LANES)], axis=0)
    eidx = route[:, 0:TOP_K].astype(jnp.int32)
    rank = route[:, 2 * TOP_K:3 * TOP_K].astype(jnp.int32)
    counts = cnt_a[0, :n_exp].astype(jnp.int32)
    padded = (counts + MOE_ROWS - 1) // MOE_ROWS * MOE_ROWS
    pend = jnp.cumsum(padded)
    dest = ((pend - padded)[eidx] + rank).reshape(-1)
    n_blocks = (tp_ + ts_) * TOP_K // MOE_ROWS + n_exp
    blk_start = jnp.arange(n_blocks, dtype=jnp.int32) * MOE_ROWS
    blk_e = jnp.minimum(jnp.sum((pend[None, :] <= blk_start[:, None]).astype(jnp.int32), axis=1), n_exp - 1)
    n_live = (pend[-1:] // MOE_ROWS).astype(jnp.int32)

    xs_rows = jnp.zeros((n_blocks * MOE_ROWS, dm), F32)
    xs_rows = _dispatch(dest[:tp_ * TOP_K], h2_p.reshape(tp_, dm), xs_rows, tm_p)
    xs_rows = _dispatch(dest[tp_ * TOP_K:], h2_s.reshape(ts_, dm), xs_rows, tm_s)
    ys = _experts(blk_e, n_live, xs_rows, w['w_gu'], w['b_gu'], w['w_down'], w['b_down'])
    g_post2 = w['g_post2'][None]
    y_p = _final(dest[:tp_ * TOP_K], route_p, x1_p, mod_p[5], map_p, g_post2, ys, tm_p)
    y_s = _final(dest[tp_ * TOP_K:], route_s, x1_s, mod_s[5], map_s, g_post2, ys, tm_s)

    wlen = min(WINDOW, s)
    kv_p = [_untranspose_rows(kvt_p[:, j * KV_ROWS:(j + 1) * KV_ROWS]) for j in range(2)]
    kvw_p = _untranspose_rows(kvt_p[:, 2 * KV_ROWS:, s - wlen:])
    pool_p = u_p[:, s - min(POOL_HIST - 1, s):]
    kv_s = [_untranspose_rows(new_t[:, j * KV_ROWS:(j + 1) * KV_ROWS]) for j in range(3)]
    w_buf = st_win.shape[1]
    kvw_s = jnp.concatenate([st_win, kv_s[2]], axis=1)[:, -w_buf:]
    pool_s = jnp.concatenate([st_pool, u_sb], axis=1)[:, -p_buf:]
    return y_p, y_s.reshape(bd, qn, dm), (kv_p[0], kv_p[1], kvw_p, pool_p, kv_s[0], kv_s[1], kvw_s, pool_s)


def kernel(x_prompt, x_sample, cache_kv_cmp, cache_kv_sel, state_kv_win, state_pool, page_table, c_prompt, c_sample,
           w_ada, b_ada, g_pre1, g_post1, g_pre2, g_post2, w_in, pool_w, pool_scale, cmp_pe, cmp_w1, cmp_b1, cmp_w2,
           w_br_pool, w_br_attn, w_o, router_w, router_b, w_gu, b_gu, w_down, b_down):
    names = ('w_ada', 'b_ada', 'g_pre1', 'g_post1', 'g_pre2', 'g_post2', 'w_in', 'pool_w', 'pool_scale', 'cmp_pe',
             'cmp_w1', 'cmp_b1', 'cmp_w2', 'w_br_pool', 'w_br_attn', 'w_o', 'router_w', 'router_b', 'w_gu', 'b_gu',
             'w_down', 'b_down')
    stacked = (w_ada, b_ada, g_pre1, g_post1, g_pre2, g_post2, w_in, pool_w, pool_scale, cmp_pe, cmp_w1, cmp_b1,
               cmp_w2, w_br_pool, w_br_attn, w_o, router_w, router_b, w_gu, b_gu, w_down, b_down)
    xp, xs = x_prompt, x_sample
    states = []
    for l in range(w_ada.shape[0]):
        w = {n: a[l] for n, a in zip(names, stacked)}
        xp, xs, st = _layer(xp, xs, cache_kv_cmp[l], cache_kv_sel[l], state_kv_win[l], state_pool[l], page_table,
                            c_prompt, c_sample, w)
        states.append(st)
    return (xp, xs) + tuple(jnp.stack([st[j] for st in states]) for j in range(8))
```

```python
import functools

import jax
import jax.numpy as jnp
import numpy as np
from jax import lax
from jax.experimental import pallas as pl
from jax.experimental.pallas import tpu as pltpu

F32 = jnp.float32
BF16 = jnp.bfloat16

POOL_WINDOWS = (2, 4, 8, 16)
N_HEADS = 8
N_KV_HEADS = 2
HEAD_DIM = 64
GQ = N_HEADS // N_KV_HEADS
CMP_BLOCK = 32
CMP_STRIDE = 16
SEL_BLOCK = 64
SUB_PER_SEL = SEL_BLOCK // CMP_STRIDE
N_SEL = 16
N_LOCAL = 2
WINDOW = 512
TOP_K = 4
SWIGLU_LIMIT = 7.0
SWIGLU_ALPHA = 1.702
RMS_EPS = 1e-6

LANES = 128
SUBLANES = 8
VMEM_LIMIT = 56 * 1024 * 1024
NEG_BIG = -1e30
M_INIT = -1e29
MXU_DEPTH = 256
FAR = 1 << 30
KV_ROWS = 2 * N_KV_HEADS * HEAD_DIM
HALF = N_KV_HEADS * HEAD_DIM
POOL_HIST = 16
MOE_ROWS = 512
SAMPLE_TILE_PAGES = 64


def _cparams(*sem):
    return pltpu.CompilerParams(dimension_semantics=sem, vmem_limit_bytes=VMEM_LIMIT)


def _full_spec(w):
    nd = w.ndim
    return pl.BlockSpec(w.shape, lambda *a: (0,) * nd)


def _dot(a, b):
    return jnp.dot(a, b, preferred_element_type=F32)


def _dot_nt(a, b):
    return lax.dot_general(a, b, (((1,), (1,)), ((), ())), preferred_element_type=F32)


def _split3(x):
    hi = x.astype(BF16)
    r1 = x - hi.astype(F32)
    mid = r1.astype(BF16)
    lo = (r1 - mid.astype(F32)).astype(BF16)
    return hi, mid, lo


def _dot_exact_rhs(x, m_bf16):
    hi, mid, lo = _split3(x)
    return _dot(hi, m_bf16) + _dot(mid, m_bf16) + _dot(lo, m_bf16)


def _rms(x, g):
    return x * lax.rsqrt(jnp.mean(x * x, axis=-1, keepdims=True) + RMS_EPS) * g


def _ada_kernel(c_ref, w_ref, b_ref, o_ref):
    o_ref[...] = _dot(c_ref[...].astype(BF16), w_ref[...].astype(BF16)) + b_ref[...]


def _ada(c_all, w_ada, b_ada):
    m, d = c_all.shape
    n = w_ada.shape[1]
    tn = 1024
    return pl.pallas_call(
        _ada_kernel,
        grid=(n // tn,),
        in_specs=[pl.BlockSpec((m, d), lambda j: (0, 0)),
                  pl.BlockSpec((d, tn), lambda j: (0, j)),
                  pl.BlockSpec((1, tn), lambda j: (0, j))],
        out_specs=pl.BlockSpec((m, tn), lambda j: (0, j)),
        out_shape=jax.ShapeDtypeStruct((m, n), F32),
        compiler_params=_cparams("arbitrary"),
        name="ada",
    )(c_all, w_ada, b_ada)


def _proj_kernel(x_ref, sc_ref, sh_ref, g_ref, wa_ref, wkv_ref,
                 u_ref, q_ref, mg_ref, ng_ref, kvt_ref, *, pw, aw, dm):
    hb = (_rms(x_ref[0], g_ref[...]) * (1.0 + sc_ref[0]) + sh_ref[0]).astype(BF16)
    u_ref[0] = _dot(hb, wa_ref[:, 0:pw])
    q_ref[0] = _dot(hb, wa_ref[:, pw:pw + aw])
    o = pw + aw
    mg_ref[0] = jax.nn.sigmoid(_dot(hb, wa_ref[:, o:o + 2 * dm]))
    ng_ref[0] = jax.nn.sigmoid(_dot(hb, wa_ref[:, o + 2 * dm:o + 2 * dm + LANES]))
    kvt_ref[0] = _dot_nt(wkv_ref[...], hb)


def _proj(x3, sc, sh, mod_map, g_pre1, wa, wkvt, tm, pw, aw):
    nb, s, dm = x3.shape
    nt = s // tm
    kern = functools.partial(_proj_kernel, pw=pw, aw=aw, dm=dm)
    mod_block = (1,) + sc.shape[1:]
    row = lambda b, i: (b, i, 0)
    return pl.pallas_call(
        kern,
        grid=(nb, nt),
        in_specs=[pl.BlockSpec((1, tm, dm), row),
                  pl.BlockSpec(mod_block, mod_map),
                  pl.BlockSpec(mod_block, mod_map),
                  _full_spec(g_pre1), _full_spec(wa), _full_spec(wkvt)],
        out_specs=[pl.BlockSpec((1, tm, pw), row),
                   pl.BlockSpec((1, tm, aw), row),
                   pl.BlockSpec((1, tm, 2 * dm), row),
                   pl.BlockSpec((1, tm, LANES), row),
                   pl.BlockSpec((1, 3 * KV_ROWS, tm), lambda b, i: (b, 0, i))],
        out_shape=[jax.ShapeDtypeStruct((nb, s, pw), F32),
                   jax.ShapeDtypeStruct((nb, s, aw), F32),
                   jax.ShapeDtypeStruct((nb, s, 2 * dm), F32),
                   jax.ShapeDtypeStruct((nb, s, LANES), F32),
                   jax.ShapeDtypeStruct((nb, 3 * KV_ROWS, s), F32)],
        compiler_params=_cparams("arbitrary", "arbitrary"),
        name="proj",
    )(x3, sc, sh, g_pre1, wa, wkvt)


def _prep_in_weights(w_in):
    dm = w_in.shape[0]
    pw = dm // 2
    aw = N_HEADS * HEAD_DIM
    off_kvc = pw + aw
    off_ng = off_kvc + 3 * KV_ROWS
    off_mg = off_ng + 3 * N_HEADS
    order = np.array([k * GQ + g for g in range(GQ) for k in range(N_KV_HEADS)])
    wq = w_in[:, pw:off_kvc].reshape(dm, N_HEADS, HEAD_DIM)[:, order].reshape(dm, aw) * (HEAD_DIM ** -0.5)
    wng = jnp.pad(w_in[:, off_ng:off_mg], ((0, 0), (0, LANES - 3 * N_HEADS)))
    wa = jnp.concatenate([w_in[:, :pw], wq, w_in[:, off_mg:], wng], axis=1).astype(BF16)
    wkvt = w_in[:, off_kvc:off_ng].T.astype(BF16)
    return wa, wkvt, pw, aw, order


def _compress_core(get_page, n_pages, perm_ref, wp_ref, w2t_ref, pe_ref, w1f_ref, b1_ref, s2_ref, out_ref):
    sub_pp = LANES // CMP_STRIDE
    n_sub = n_pages * sub_pp
    r_cmp = CMP_BLOCK // CMP_STRIDE
    hid = w1f_ref.shape[-1]

    def page_body(p, carry):
        pg = get_page(p).astype(BF16)
        t = _dot_nt(perm_ref[...], pg)
        row0 = pl.multiple_of(p * sub_pp, sub_pp)
        for s in range(2):
            for j in range(CMP_STRIDE):
                s2_ref[s, j // 2, pl.ds(row0, sub_pp), (j % 2) * HALF:(j % 2 + 1) * HALF] = (
                    t[sub_pp * j:sub_pp * (j + 1), s * HALF:(s + 1) * HALF])
        return carry

    lax.fori_loop(0, n_pages, page_body, 0, unroll=4)
    for s in range(2):
        acc = None
        for jp in range(CMP_STRIDE // 2):
            d = _dot(s2_ref[s, jp].astype(BF16), wp_ref[s, jp])
            acc = d if acc is None else acc + d
        bias = _dot(pe_ref[s], w1f_ref[s])[0:1] + b1_ref[s]
        bias = jnp.concatenate([bias] * N_KV_HEADS, axis=1)
        w = N_KV_HEADS * hid
        hpre = acc[:, :w] + bias
        for r in range(1, r_cmp):
            hpre = hpre + pltpu.roll(acc[:, r * w:(r + 1) * w], n_sub - r, axis=0)
        g = jax.nn.gelu(hpre).astype(BF16)
        out_ref[s * HALF:(s + 1) * HALF, :] = _dot_nt(w2t_ref[s], g)


def _compress_prompt_kernel(kv_ref, perm_ref, wp_ref, w2t_ref, pe_ref, w1f_ref, b1_ref, out_ref, s2_ref, *, n_pages):
    def get_page(p):
        return kv_ref[0, :, pl.ds(pl.multiple_of(p * LANES, LANES), LANES)]
    _compress_core(get_page, n_pages, perm_ref, wp_ref, w2t_ref, pe_ref, w1f_ref, b1_ref, s2_ref, out_ref.at[0])


def _page_fetch(pt_ref, cache_hbm, buf, sem, b, slot, n_pages):
    def start(p, c):
        pltpu.make_async_copy(cache_hbm.at[pt_ref[b * n_pages + p]], buf.at[slot, p], sem.at[slot]).start()
        return c
    lax.fori_loop(0, n_pages, start, 0)


def _page_wait(cache_hbm, buf, sem, slot, n_pages):
    def wait(p, c):
        pltpu.make_async_copy(cache_hbm.at[0], buf.at[slot, p], sem.at[slot]).wait()
        return c
    lax.fori_loop(0, n_pages, wait, 0)


def _paged_prologue(pt_ref, cache_hbm, buf, sem, n_pages):
    b = pl.program_id(0)
    slot = b % 2

    @pl.when(b == 0)
    def _():
        _page_fetch(pt_ref, cache_hbm, buf, sem, 0, 0, n_pages)

    @pl.when(b + 1 < pl.num_programs(0))
    def _():
        _page_fetch(pt_ref, cache_hbm, buf, sem, b + 1, 1 - slot, n_pages)

    _page_wait(cache_hbm, buf, sem, slot, n_pages)
    return slot


def _compress_paged_kernel(pt_ref, cache_hbm, perm_ref, wp_ref, w2t_ref, pe_ref, w1f_ref, b1_ref, out_ref,
                           buf, sem, s2_ref, *, n_pages):
    slot = _paged_prologue(pt_ref, cache_hbm, buf, sem, n_pages)
    _compress_core(lambda p: buf[slot, p], n_pages, perm_ref, wp_ref, w2t_ref, pe_ref, w1f_ref, b1_ref,
                   s2_ref, out_ref.at[0])


def _prep_cmp_weights(cmp_pe, cmp_w1, cmp_b1, cmp_w2):
    hid = cmp_w1.shape[-1]
    r_cmp = CMP_BLOCK // CMP_STRIDE
    eye = jnp.eye(N_KV_HEADS, dtype=F32)
    w1r = cmp_w1.reshape(2, r_cmp, CMP_STRIDE // 2, 2, HEAD_DIM, hid)
    wp = jnp.einsum('srpjdh,kc->spjkdrch', w1r, eye).reshape(
        2, CMP_STRIDE // 2, 2 * HALF, r_cmp * N_KV_HEADS * hid).astype(BF16)
    w2t = jnp.einsum('shd,kc->skdch', cmp_w2, eye).reshape(2, HALF, N_KV_HEADS * hid).astype(BF16)
    pe = jnp.broadcast_to(cmp_pe.reshape(2, 1, CMP_BLOCK * HEAD_DIM),
                          (2, SUBLANES, CMP_BLOCK * HEAD_DIM)).astype(BF16)
    w1f = cmp_w1.reshape(2, CMP_BLOCK * HEAD_DIM, hid).astype(BF16)
    b1 = cmp_b1.reshape(2, 1, hid)
    sub_pp = LANES // CMP_STRIDE
    x = np.arange(LANES)
    perm = np.zeros((LANES, LANES), np.float32)
    perm[x, (x % sub_pp) * CMP_STRIDE + x // sub_pp] = 1.0
    return (jnp.asarray(perm, BF16), wp, w2t, pe, w1f, b1)


def _compress_prompt(kvt, cw):
    nb, _, s = kvt.shape
    n_pages = s // LANES
    n_sub = s // CMP_STRIDE
    return pl.pallas_call(
        functools.partial(_compress_prompt_kernel, n_pages=n_pages),
        grid=(nb,),
        in_specs=[pl.BlockSpec((1, KV_ROWS, s), lambda b: (b, 0, 0))] + [_full_spec(w) for w in cw],
        out_specs=pl.BlockSpec((1, KV_ROWS, n_sub), lambda b: (b, 0, 0)),
        out_shape=jax.ShapeDtypeStruct((nb, KV_ROWS, n_sub), F32),
        scratch_shapes=[pltpu.VMEM((2, CMP_STRIDE // 2, n_sub, 2 * HALF), F32)],
        compiler_params=_cparams("arbitrary"),
        name="compress_prompt",
    )(kvt, *cw)


def _compress_paged(page_tab, n_pages, cache_t, cw):
    nb = page_tab.shape[0] // n_pages
    n_sub = n_pages * (LANES // CMP_STRIDE)
    return pl.pallas_call(
        functools.partial(_compress_paged_kernel, n_pages=n_pages),
        grid_spec=pltpu.PrefetchScalarGridSpec(
            num_scalar_prefetch=1,
            grid=(nb,),
            in_specs=[pl.BlockSpec(memory_space=pl.ANY)] + [_full_spec(w) for w in cw],
            out_specs=pl.BlockSpec((1, KV_ROWS, n_sub), lambda b, pt: (b, 0, 0)),
            scratch_shapes=[pltpu.VMEM((2, n_pages, KV_ROWS, LANES), F32),
                            pltpu.SemaphoreType.DMA((2,)),
                            pltpu.VMEM((2, CMP_STRIDE // 2, n_sub, 2 * HALF), F32)]),
        out_shape=jax.ShapeDtypeStruct((nb, KV_ROWS, n_sub), F32),
        compiler_params=_cparams("arbitrary"),
        name="compress_paged",
    )(page_tab, cache_t, *cw)


def _stack_heads(fn):
    return jnp.concatenate([fn(k, g) for k in range(N_KV_HEADS) for g in range(GQ)], axis=0)


def _alibi_slope(k, g):
    return 2.0 ** (-8.0 * (k * GQ + g + 1) / N_HEADS)


def _nsa_block(q, sgate, t0, ckvt_ref, mmat_ref, bmat_ref, sel_tile, n_sel_tiles, sel_tk, sel_last, win_tile,
               m_ref, acc_ref, out_ref, *, qt, nblk_pad, n_pick, active=None):
    t_pos = t0 + lax.broadcasted_iota(jnp.int32, (qt, 1), 0)
    lane_half = lax.broadcasted_iota(jnp.int32, (qt, LANES), 1) // HEAD_DIM

    qp = _stack_heads(lambda k, g: jnp.where(lane_half == k, q[:, g * LANES:(g + 1) * LANES], 0.0)).astype(BF16)

    def alibi(distf):
        return _stack_heads(lambda k, g: _alibi_slope(k, g) * distf)

    def stack_masks(mk):
        return jnp.concatenate([mk[k] for k in range(N_KV_HEADS) for _ in range(GQ)], axis=0) > 0.5

    nc = ckvt_ref.shape[-1]
    c_end = lax.broadcasted_iota(jnp.int32, (1, nc), 1) * CMP_STRIDE + (CMP_BLOCK - 1)
    dist_c = t_pos - c_end
    valid_c = jnp.where(dist_c >= 0, 1.0, 0.0)
    mask_c = stack_masks([valid_c] * N_KV_HEADS)
    s = _dot(qp, ckvt_ref[0:HALF, :].astype(BF16)) - alibi(dist_c.astype(F32))
    s = jnp.where(mask_c, s, NEG_BIG)
    p = jnp.where(mask_c, jnp.exp(s - jnp.max(s, axis=1, keepdims=True)), 0.0)
    p = p / jnp.maximum(jnp.sum(p, axis=1, keepdims=True), 1e-30)
    o_c = _dot_nt(p.astype(BF16), ckvt_ref[HALF:2 * HALF, :].astype(BF16))

    blk = lax.broadcasted_iota(jnp.int32, (qt, nblk_pad), 1)
    cur = t_pos // SEL_BLOCK
    forced = (blk == 0) | ((blk > cur - N_LOCAL) & (blk <= cur))
    free = (blk >= 1) & (blk <= cur - N_LOCAL)
    blkf = blk.astype(F32)
    sel = []
    for k in range(N_KV_HEADS):
        imp = p[k * GQ * qt:(k * GQ + 1) * qt]
        for g in range(1, GQ):
            imp = imp + p[(k * GQ + g) * qt:(k * GQ + g + 1) * qt]
        score = jnp.where(free, _dot_exact_rhs(imp, mmat_ref[...]), -jnp.inf)
        chosen = jnp.where(forced, 1.0, 0.0)
        for _ in range(n_pick):
            best = jnp.max(score, axis=1, keepdims=True)
            idx = jnp.min(jnp.where(score == best, blkf, float(nblk_pad)), axis=1, keepdims=True)
            hit = blkf == idx
            chosen = jnp.where(hit, 1.0, chosen)
            score = jnp.where(hit, -jnp.inf, score)
        sel.append(chosen)

    rk = GQ * qt
    lane = lax.broadcasted_iota(jnp.int32, (1, LANES), 1)
    t_rows = jnp.concatenate([t_pos] * GQ, axis=0)

    def q_feat(k):
        parts = []
        for g in range(GQ):
            sl = _alibi_slope(k, g)
            base = (1 - k) * HEAD_DIM
            ext = jnp.where(lane == base, SEL_BLOCK * sl, jnp.where(lane == base + 1, sl, 0.0))
            parts.append(jnp.where(lane_half == k, q[:, g * LANES:(g + 1) * LANES], ext))
        return jnp.concatenate(parts, axis=0).astype(BF16)

    qf = [q_feat(k) for k in range(N_KV_HEADS)]

    def k_aug(kt, pos, use_sel):
        tk = kt.shape[-1]
        hi = lax.shift_right_arithmetic(pos, SEL_BLOCK.bit_length() - 1)
        lo = pos & (SEL_BLOCK - 1)
        r = lax.broadcasted_iota(jnp.int32, (HEAD_DIM, tk), 0)
        ext = jnp.where(r == 0, hi.astype(F32), jnp.where(r == 1, lo.astype(F32), 0.0)).astype(BF16)
        if use_sel:
            onehot = jnp.where(lax.broadcasted_iota(jnp.int32, (nblk_pad, 1), 0) == hi, 1.0, 0.0).astype(BF16)
        else:
            onehot = jnp.zeros((nblk_pad, tk), BF16)
        kb = kt.astype(BF16)
        return [jnp.concatenate([onehot, kb[0:HEAD_DIM], ext], axis=0),
                jnp.concatenate([onehot, ext, kb[HEAD_DIM:HALF]], axis=0)]

    m_ref[...] = jnp.full(m_ref.shape, M_INIT, F32)
    acc_ref[...] = jnp.zeros(acc_ref.shape, F32)

    def flash(br, qa, kt, vt, pos, use_sel, cond):
        ka = k_aug(kt, pos, use_sel)
        va = jnp.concatenate([vt.astype(BF16), jnp.ones((2 * SUBLANES, kt.shape[-1]), BF16)], axis=0)
        for k in range(N_KV_HEADS):
            sc = _dot(qa[k], ka[k])
            if cond is not None:
                sc = jnp.where(cond, sc, NEG_BIG)
            m_old = m_ref[br, k]
            m_new = jnp.maximum(m_old, jnp.max(sc, axis=1, keepdims=True))
            pr = jnp.exp(sc - m_new).astype(BF16)
            acc_ref[br, k] = jnp.exp(m_old - m_new) * acc_ref[br, k] + _dot_nt(pr, va)
            m_ref[br, k] = m_new

    def result(br):
        out = []
        for k in range(N_KV_HEADS):
            a = acc_ref[br, k]
            out.append(a[:, 0:HALF] / jnp.maximum(a[:, HALF:HALF + 1], 1e-30))
        return out

    qa_w = [jnp.concatenate([jnp.zeros((rk, nblk_pad), BF16), qf[k]], axis=1) for k in range(N_KV_HEADS)]
    kt, vt, pos = win_tile
    dist = t_rows - pos
    flash(1, qa_w, kt, vt, pos, False, (dist >= 0) & (dist <= WINDOW))

    qa_s = []
    for k in range(N_KV_HEADS):
        selneg = jnp.where(sel[k] > 0.5, 0.0, NEG_BIG).astype(BF16)
        qa_s.append(jnp.concatenate([jnp.concatenate([selneg] * GQ, axis=0), qf[k]], axis=1))

    def sel_step(j):
        kt, vt = sel_tile(j)
        flash(0, qa_s, kt, vt, j * sel_tk + lax.broadcasted_iota(jnp.int32, (1, sel_tk), 1), True, None)

    if isinstance(n_sel_tiles, int):
        for j in range(n_sel_tiles):
            sel_step(j)
    else:
        act_ref, max_tiles = active
        tile_of_blk = lax.broadcasted_iota(jnp.int32, (nblk_pad, 1), 0) // (sel_tk // SEL_BLOCK)
        group = jnp.where(tile_of_blk == lax.broadcasted_iota(jnp.int32, (1, LANES), 1), 1.0, 0.0).astype(BF16)
        any_sel = jnp.max(jnp.maximum(sel[0], sel[1]), axis=0, keepdims=True)
        tile_any = _dot(jnp.broadcast_to(any_sel, (SUBLANES, nblk_pad)).astype(BF16), group)
        cnt = jnp.int32(0)
        for j in range(max_tiles):
            act_ref[cnt] = j
            keep = (tile_any[0, j] > 0.5) & (j < n_sel_tiles)
            cnt = cnt + keep.astype(jnp.int32)

        def pair_body(p, c):
            sel_step(act_ref[2 * p])
            sel_step(act_ref[2 * p + 1])
            return c

        lax.fori_loop(0, cnt // 2, pair_body, 0)

        @pl.when(cnt % 2 == 1)
        def _():
            sel_step(act_ref[cnt - 1])
    kt, vt, pos, use_sel = sel_last
    flash(0, qa_s, kt, vt, pos, use_sel, t_rows >= pos)
    o_s = result(0)
    o_w = result(1)

    o_c = [o_c[0:rk], o_c[rk:2 * rk]]
    g3 = _split3(sgate)
    for g in range(GQ):
        slab = None
        for c, o in enumerate((o_c, o_s, o_w)):
            bm = bmat_ref[g * 3 + c]
            gate = _dot(g3[0], bm) + _dot(g3[1], bm) + _dot(g3[2], bm)
            val = jnp.where(lane_half == 0, o[0][g * qt:(g + 1) * qt], o[1][g * qt:(g + 1) * qt])
            slab = gate * val if slab is None else slab + gate * val
        out_ref[:, g * LANES:(g + 1) * LANES] = slab


def _nsa_consts(nc, nblk_pad):
    j = np.arange(nc)[:, None]
    lo = SUB_PER_SEL * np.arange(nblk_pad)[None, :]
    mm = (np.where((j >= lo) & (j < lo + SUB_PER_SEL - 1), 2.0, 0.0)
          + np.where((j == lo - 1) | (j == lo + SUB_PER_SEL - 1), 1.0, 0.0))
    col = np.arange(LANES)[:, None]
    lane = np.arange(LANES)[None, :]
    bm = np.stack([(col == ((lane // HEAD_DIM) * GQ + g) * 3 + c)
                   for g in range(GQ) for c in range(3)]).astype(np.float32)
    return jnp.asarray(mm, BF16), jnp.asarray(bm, BF16)


def _lane_iota(n):
    return lax.broadcasted_iota(jnp.int32, (1, n), 1)


def _flash_scratch(qt, lead=()):
    rk = GQ * qt
    lead = (lead,) if isinstance(lead, int) else lead
    return [pltpu.VMEM(lead + (2, N_KV_HEADS, rk, 1), F32),
            pltpu.VMEM(lead + (2, N_KV_HEADS, rk, HALF + 2 * SUBLANES), F32)]


def _nsa_prompt_kernel(*refs, qt, tk, n_win, nblk_pad, n_pick):
    q_ref, ng_ref, ckvt_ref, kvs_ref = refs[:4]
    win_refs = refs[4:4 + n_win]
    mmat_ref, bmat_ref, out_ref, m_ref, acc_ref, act_ref = refs[4 + n_win:]
    i = pl.program_id(1)

    def sel_tile(j):
        off = pl.multiple_of(j * tk, LANES)
        return kvs_ref[0, 0:HALF, pl.ds(off, tk)], kvs_ref[0, HALF:2 * HALF, pl.ds(off, tk)]

    n_past = (i * qt) // tk
    sel_last = sel_tile(n_past) + (n_past * tk + _lane_iota(tk), True)
    pieces = [w[0] for w in win_refs]
    pos = []
    for jj in range(n_win):
        bi = i - (n_win - 1) + jj
        pos.append(jnp.where(bi >= 0, bi * qt, -FAR) + _lane_iota(qt))
    win = (jnp.concatenate([p[0:HALF] for p in pieces], axis=1),
           jnp.concatenate([p[HALF:2 * HALF] for p in pieces], axis=1), jnp.concatenate(pos, axis=1))
    _nsa_block(q_ref[0], ng_ref[0], i * qt, ckvt_ref.at[0], mmat_ref, bmat_ref, sel_tile, n_past, tk, sel_last, win,
               m_ref, acc_ref, out_ref.at[0], qt=qt, nblk_pad=nblk_pad, n_pick=n_pick,
               active=(act_ref, act_ref.shape[0]))


def _nsa_prompt(q, ng, ckvt, kvt, qt):
    nb, s, aw = q.shape
    nc = ckvt.shape[-1]
    nblk = -(-s // SEL_BLOCK)
    nblk_pad = -(-nblk // LANES) * LANES
    assert nblk_pad + HALF <= MXU_DEPTH, "selection mask + features must fit one MXU contraction pass"
    n_pick = max(min(N_SEL, nblk) - (N_LOCAL + 1), 0)
    n_win = WINDOW // qt + 1
    tk = min(4 * qt, s)
    assert s % tk == 0 and tk % qt == 0
    mmat, bmat = _nsa_consts(nc, nblk_pad)
    row = lambda b, i: (b, i, 0)
    win_specs = [pl.BlockSpec((1, KV_ROWS, qt), (lambda jj: (lambda b, i: (b, 2, jnp.maximum(i - (n_win - 1) + jj, 0))))(jj))
                 for jj in range(n_win)]
    return pl.pallas_call(
        functools.partial(_nsa_prompt_kernel, qt=qt, tk=tk, n_win=n_win, nblk_pad=nblk_pad, n_pick=n_pick),
        grid=(nb, s // qt),
        in_specs=[pl.BlockSpec((1, qt, aw), row),
                  pl.BlockSpec((1, qt, LANES), row),
                  pl.BlockSpec((1, KV_ROWS, nc), lambda b, i: (b, 0, 0)),
                  pl.BlockSpec((1, KV_ROWS, s), lambda b, i: (b, 1, 0))] + win_specs
                 + [_full_spec(mmat), _full_spec(bmat)],
        out_specs=pl.BlockSpec((1, qt, aw), row),
        out_shape=jax.ShapeDtypeStruct((nb, s, aw), F32),
        scratch_shapes=_flash_scratch(qt) + [pltpu.SMEM((s // tk,), jnp.int32)],
        compiler_params=_cparams("arbitrary", "arbitrary"),
        name="nsa_prompt",
    )(q, ng, ckvt, kvt, *([kvt] * n_win), mmat, bmat)


def _nsa_sample_kernel(pt_ref, q_ref, ng_ref, ckvt_ref, cache_hbm, ksn_ref, wst_ref, kwn_ref, mmat_ref, bmat_ref,
                       out_ref, buf, sem, m_ref, acc_ref, *, qt, nbs, n_pages, ppt, nblk_pad, n_pick):
    slot = _paged_prologue(pt_ref, cache_hbm, buf, sem, nbs * n_pages)
    past = n_pages * LANES
    wbuf = wst_ref.shape[-1]
    for bb in range(nbs):
        def sel_tile(j, bb=bb):
            p0 = bb * n_pages + j * ppt
            ks = [buf[slot, p0 + pp, 0:HALF, :] for pp in range(ppt)]
            vs = [buf[slot, p0 + pp, HALF:2 * HALF, :] for pp in range(ppt)]
            return jnp.concatenate(ks, axis=1), jnp.concatenate(vs, axis=1)

        sel_last = (ksn_ref[bb, 0:HALF, :], ksn_ref[bb, HALF:2 * HALF, :], past + _lane_iota(LANES), False)
        win = (jnp.concatenate([wst_ref[bb, 0:HALF, :], kwn_ref[bb, 0:HALF, :]], axis=1),
               jnp.concatenate([wst_ref[bb, HALF:2 * HALF, :], kwn_ref[bb, HALF:2 * HALF, :]], axis=1),
               past - wbuf + _lane_iota(wbuf + LANES))
        _nsa_block(q_ref[bb], ng_ref[bb], past, ckvt_ref.at[bb], mmat_ref, bmat_ref, sel_tile, n_pages // ppt,
                   ppt * LANES, sel_last, win, m_ref.at[bb], acc_ref.at[bb], out_ref.at[bb],
                   qt=qt, nblk_pad=nblk_pad, n_pick=n_pick)


def _nsa_sample(page_tab, n_pages, q, ng, ckvt, cache_t, ks_new, w_state, kw_new):
    nb, qt, aw = q.shape
    nc = ckvt.shape[-1]
    past = n_pages * LANES
    assert past % SEL_BLOCK == 0 and qt <= SEL_BLOCK
    nblk_pad = -(-(past // SEL_BLOCK) // LANES) * LANES
    assert nblk_pad + HALF <= MXU_DEPTH
    n_pick = max(min(N_SEL, past // SEL_BLOCK + 1) - (N_LOCAL + 1), 0)
    ppt = min(SAMPLE_TILE_PAGES, n_pages)
    nbs = 2 if nb % 2 == 0 else 1
    assert n_pages % ppt == 0
    mmat, bmat = _nsa_consts(nc, nblk_pad)
    b3 = lambda b, pt: (b, 0, 0)
    return pl.pallas_call(
        functools.partial(_nsa_sample_kernel, qt=qt, nbs=nbs, n_pages=n_pages, ppt=ppt, nblk_pad=nblk_pad,
                          n_pick=n_pick),
        grid_spec=pltpu.PrefetchScalarGridSpec(
            num_scalar_prefetch=1,
            grid=(nb // nbs,),
            in_specs=[pl.BlockSpec((nbs, qt, aw), b3),
                      pl.BlockSpec((nbs, qt, LANES), b3),
                      pl.BlockSpec((nbs, KV_ROWS, nc), b3),
                      pl.BlockSpec(memory_space=pl.ANY),
                      pl.BlockSpec((nbs, KV_ROWS, LANES), b3),
                      pl.BlockSpec((nbs, KV_ROWS, w_state.shape[-1]), b3),
                      pl.BlockSpec((nbs, KV_ROWS, LANES), b3),
                      _full_spec(mmat), _full_spec(bmat)],
            out_specs=pl.BlockSpec((nbs, qt, aw), b3),
            scratch_shapes=[pltpu.VMEM((2, nbs * n_pages, KV_ROWS, LANES), F32),
                            pltpu.SemaphoreType.DMA((2,))] + _flash_scratch(qt, nbs)),
        out_shape=jax.ShapeDtypeStruct((nb, qt, aw), F32),
        compiler_params=_cparams("arbitrary"),
        name="nsa_sample",
    )(page_tab, q, ng, ckvt, cache_t, ks_new, w_state, kw_new, mmat, bmat)


def _pool_kernel(hist_ref, cur_ref, pw_ref, ps_ref, out_ref, *, ts, pos_base, zero_first):
    i = pl.program_id(1)
    hist = hist_ref[0]
    if zero_first:
        hist = jnp.where(i == 0, 0.0, hist)
    cur = cur_ref[0]
    ext = jnp.concatenate([hist, cur], axis=0)
    pos = pos_base + i * ts + lax.broadcasted_iota(jnp.int32, (ts, 1), 0)
    ys = []
    for gi, w in enumerate(POOL_WINDOWS):
        lanes = slice(gi * LANES, (gi + 1) * LANES)
        acc = ext[:, lanes]
        step = 1
        while step < w:
            acc = acc + pltpu.roll(acc, step, axis=0)
            step *= 2
        cnt = jnp.minimum(pos + 1, w).astype(F32)
        dlt = acc[POOL_HIST:] / cnt - cur[:, lanes]
        ys.append(_dot(dlt.astype(BF16), pw_ref[gi].astype(BF16)))
    out_ref[0] = jnp.concatenate(ys, axis=1) * ps_ref[...]


def _pool(hist_arr, hist_map, u, pool_w, pool_scale, ts, pos_base, zero_first):
    nb, s, pw = u.shape
    assert pw == len(POOL_WINDOWS) * LANES and all(w & (w - 1) == 0 and w <= POOL_HIST for w in POOL_WINDOWS)
    row = lambda b, i: (b, i, 0)
    return pl.pallas_call(
        functools.partial(_pool_kernel, ts=ts, pos_base=pos_base, zero_first=zero_first),
        grid=(nb, s // ts),
        in_specs=[pl.BlockSpec((1, POOL_HIST, pw), hist_map),
                  pl.BlockSpec((1, ts, pw), row),
                  _full_spec(pool_w), _full_spec(pool_scale)],
        out_specs=pl.BlockSpec((1, ts, pw), row),
        out_shape=jax.ShapeDtypeStruct((nb, s, pw), F32),
        compiler_params=_cparams("arbitrary", "arbitrary"),
        name="pool",
    )(hist_arr, u, pool_w, pool_scale)


def _merge_kernel(x_ref, py_ref, ay_ref, mg_ref, gt1_ref, sh2_ref, sc2_ref, gpost1_ref, gpre2_ref,
                  wbp_ref, wba_ref, wo_ref, rwh_ref, rwl_ref, rb_ref, tri_ref, cnt0_ref,
                  x1_ref, h2_ref, route_ref, cnt_ref, *, dm):
    first = (pl.program_id(0) == 0) & (pl.program_id(1) == 0)

    @pl.when(first)
    def _():
        cnt_ref[...] = cnt0_ref[...]

    bp = _dot(py_ref[0].astype(BF16), wbp_ref[...])
    ba = _dot(ay_ref[0].astype(BF16), wba_ref[...])
    merged = mg_ref[0, :, 0:dm] * bp + mg_ref[0, :, dm:2 * dm] * ba
    mix = _dot(merged.astype(BF16), wo_ref[...])
    x1 = x_ref[0] + gt1_ref[0] * _rms(mix, gpost1_ref[...])
    h2 = _rms(x1, gpre2_ref[...]) * (1.0 + sc2_ref[0]) + sh2_ref[0]
    x1_ref[0] = x1
    h2_ref[0] = h2

    hh = h2.astype(BF16)
    hl = (h2 - hh.astype(F32)).astype(BF16)
    logits = _dot(hh, rwh_ref[...]) + _dot(hl, rwh_ref[...]) + _dot(hh, rwl_ref[...]) + rb_ref[...]
    tm = logits.shape[0]
    lane = lax.broadcasted_iota(jnp.int32, (tm, LANES), 1)
    lanef = lane.astype(F32)
    hits, vals, idxs = [], [], []
    for _ in range(TOP_K):
        best = jnp.max(logits, axis=1, keepdims=True)
        idx = jnp.min(jnp.where(logits == best, lanef, float(LANES)), axis=1, keepdims=True)
        hit = lanef == idx
        hits.append(hit)
        vals.append(best)
        idxs.append(idx)
        logits = jnp.where(hit, -jnp.inf, logits)
    ex = [jnp.exp(v - vals[0]) for v in vals]
    den = ex[0]
    for e in ex[1:]:
        den = den + e
    onehot = jnp.where(hits[0], 1.0, 0.0)
    for h in hits[1:]:
        onehot = onehot + jnp.where(h, 1.0, 0.0)
    before = _dot(tri_ref[...], onehot.astype(BF16)) + cnt_ref[...]
    route = jnp.zeros((tm, LANES), F32)
    for k in range(TOP_K):
        rank = jnp.sum(jnp.where(hits[k], before, 0.0), axis=1, keepdims=True)
        route = route + jnp.where(lane == k, idxs[k], 0.0)
        route = route + jnp.where(lane == TOP_K + k, ex[k] / den, 0.0)
        route = route + jnp.where(lane == 2 * TOP_K + k, rank, 0.0)
    route_ref[0] = route
    cnt_ref[...] = cnt_ref[...] + jnp.sum(onehot, axis=0, keepdims=True)


def _merge(x3, py, ay, mg, gt1, sh2, sc2, mod_map, weights, cnt0, tm):
    nb, s, dm = x3.shape
    row = lambda b, i: (b, i, 0)
    mod_block = (1,) + gt1.shape[1:]
    tri = jnp.asarray(np.tril(np.ones((tm, tm), np.float32), -1), BF16)
    consts = list(weights) + [tri, cnt0]
    return pl.pallas_call(
        functools.partial(_merge_kernel, dm=dm),
        grid=(nb, s // tm),
        in_specs=[pl.BlockSpec((1, tm, dm), row),
                  pl.BlockSpec((1, tm, py.shape[-1]), row),
                  pl.BlockSpec((1, tm, ay.shape[-1]), row),
                  pl.BlockSpec((1, tm, 2 * dm), row),
                  pl.BlockSpec(mod_block, mod_map), pl.BlockSpec(mod_block, mod_map), pl.BlockSpec(mod_block, mod_map)]
                 + [_full_spec(w) for w in consts],
        out_specs=[pl.BlockSpec((1, tm, dm), row), pl.BlockSpec((1, tm, dm), row),
                   pl.BlockSpec((1, tm, LANES), row), pl.BlockSpec((1, LANES), lambda b, i: (0, 0))],
        out_shape=[jax.ShapeDtypeStruct((nb, s, dm), F32), jax.ShapeDtypeStruct((nb, s, dm), F32),
                   jax.ShapeDtypeStruct((nb, s, LANES), F32), jax.ShapeDtypeStruct((1, LANES), F32)],
        compiler_params=_cparams("arbitrary", "arbitrary"),
        name="merge",
    )(x3, py, ay, mg, gt1, sh2, sc2, *consts)


def _scatter_rows(dest_ref, h_ref, xs_out, sem, tm):
    def row_copy(r, d):
        return pltpu.make_async_copy(h_ref.at[pl.ds(r, 1), :], xs_out.at[pl.ds(d, 1), :], sem)

    def start(r, c):
        for k in range(TOP_K):
            row_copy(r, dest_ref[r * TOP_K + k]).start()
        return c

    def wait(r, c):
        for k in range(TOP_K):
            row_copy(0, 0).wait()
        return c

    lax.fori_loop(0, tm, start, 0)
    lax.fori_loop(0, tm, wait, 0)


def _dispatch_kernel(clr_ref, nl_ref, dest_ref, ha_ref, hb_ref, xs_out, sem, zero_ref, zsem, *, td, nta, n_exp,
                     n_blocks):
    i = pl.program_id(0)

    @pl.when(i == 0)
    def _():
        zero_ref[...] = jnp.zeros(zero_ref.shape, F32)

        def zcopy(row0):
            return pltpu.make_async_copy(zero_ref, xs_out.at[pl.ds(pl.multiple_of(row0, MOE_ROWS), MOE_ROWS), :], zsem)

        def tail_start(j, c):
            zcopy(j * MOE_ROWS).start()
            return c

        def tail_wait(j, c):
            zcopy(0).wait()
            return c

        for e in range(n_exp):
            zcopy(clr_ref[e]).start()
        lax.fori_loop(nl_ref[0], n_blocks, tail_start, 0)
        for e in range(n_exp):
            zcopy(0).wait()
        lax.fori_loop(nl_ref[0], n_blocks, tail_wait, 0)

    @pl.when(i < nta)
    def _():
        _scatter_rows(dest_ref, ha_ref, xs_out, sem, td)

    @pl.when(i >= nta)
    def _():
        _scatter_rows(dest_ref, hb_ref, xs_out, sem, td)


def _dispatch(clear_rows, n_live, dest, h_a, h_b, n_blocks, td):
    ta, dm = h_a.shape
    tb = h_b.shape[0]
    n_exp = clear_rows.shape[0]
    nta = ta // td
    return pl.pallas_call(
        functools.partial(_dispatch_kernel, td=td, nta=nta, n_exp=n_exp, n_blocks=n_blocks),
        grid=(nta + tb // td,),
        in_specs=[pl.BlockSpec(memory_space=pltpu.SMEM),
                  pl.BlockSpec(memory_space=pltpu.SMEM),
                  pl.BlockSpec((td * TOP_K,), lambda i: (i,), memory_space=pltpu.SMEM),
                  pl.BlockSpec((td, dm), lambda i: (jnp.minimum(i, nta - 1), 0)),
                  pl.BlockSpec((td, dm), lambda i: (jnp.maximum(i - nta, 0), 0))],
        out_specs=pl.BlockSpec(memory_space=pl.ANY),
        out_shape=jax.ShapeDtypeStruct((n_blocks * MOE_ROWS, dm), F32),
        scratch_shapes=[pltpu.SemaphoreType.DMA(()), pltpu.VMEM((MOE_ROWS, dm), F32), pltpu.SemaphoreType.DMA(())],
        compiler_params=_cparams("arbitrary"),
        name="dispatch",
    )(clear_rows, n_live, dest, h_a, h_b)


def _expert_kernel(be_ref, nv_ref, x_ref, wgu_ref, bgu_ref, wd_ref, bd_ref, y_ref, wgu_bf, wd_bf, *, dff):
    i = pl.program_id(0)
    live = i < nv_ref[0]
    fresh = (i == 0) | (be_ref[i] != be_ref[jnp.maximum(i - 1, 0)])

    @pl.when(live & fresh)
    def _():
        wgu_bf[...] = wgu_ref[0].astype(BF16)
        wd_bf[...] = wd_ref[0].astype(BF16)

    @pl.when(live)
    def _():
        gu = _dot(x_ref[...].astype(BF16), wgu_bf[...]) + bgu_ref[0]
        gate = jnp.minimum(gu[:, :dff], SWIGLU_LIMIT)
        up = jnp.clip(gu[:, dff:], -SWIGLU_LIMIT, SWIGLU_LIMIT)
        act = (up + 1.0) * gate * jax.nn.sigmoid(SWIGLU_ALPHA * gate)
        y_ref[...] = _dot(act.astype(BF16), wd_bf[...]) + bd_ref[0]

    @pl.when(jnp.logical_not(live))
    def _():
        y_ref[...] = jnp.zeros(y_ref.shape, F32)


def _experts(blk_e, n_live, xs, w_gu, b_gu, w_down, b_down):
    rows, dm = xs.shape
    n_exp, _, dff2 = w_gu.shape
    dff = dff2 // 2
    nblk = rows // MOE_ROWS
    xmap = lambda i, be, nv: (jnp.minimum(i, nv[0] - 1), 0)
    emap = lambda i, be, nv: (be[i], 0, 0)
    return pl.pallas_call(
        functools.partial(_expert_kernel, dff=dff),
        grid_spec=pltpu.PrefetchScalarGridSpec(
            num_scalar_prefetch=2,
            grid=(nblk,),
            in_specs=[pl.BlockSpec((MOE_ROWS, dm), xmap),
                      pl.BlockSpec((1, dm, dff2), emap),
                      pl.BlockSpec((1, 1, dff2), emap),
                      pl.BlockSpec((1, dff, dm), emap),
                      pl.BlockSpec((1, 1, dm), emap)],
            out_specs=pl.BlockSpec((MOE_ROWS, dm), lambda i, be, nv: (i, 0)),
            scratch_shapes=[pltpu.VMEM((dm, dff2), BF16), pltpu.VMEM((dff, dm), BF16)]),
        out_shape=jax.ShapeDtypeStruct((rows, dm), F32),
        compiler_params=_cparams("arbitrary"),
        name="experts",
    )(blk_e, n_live, xs, w_gu, b_gu.reshape(n_exp, 1, dff2), w_down, b_down.reshape(n_exp, 1, dm))


def _final_kernel(dest_ref, dnext_ref, route_ref, x1_ref, gt2_ref, gpost2_ref, ys_hbm, out_ref, rows_buf, sem, *, tm):
    nt = pl.num_programs(1)
    step = pl.program_id(0) * nt + pl.program_id(1)
    slot = step % 2

    def row_copy(sl, r, k, d):
        return pltpu.make_async_copy(ys_hbm.at[pl.ds(d, 1), :], rows_buf.at[sl, k, pl.ds(r, 1), :], sem.at[sl])

    def fetch(idx_ref, sl):
        def start(r, c):
            for k in range(TOP_K):
                row_copy(sl, r, k, idx_ref[r * TOP_K + k]).start()
            return c
        lax.fori_loop(0, tm, start, 0)

    @pl.when(step == 0)
    def _():
        fetch(dest_ref, 0)

    @pl.when(step + 1 < pl.num_programs(0) * nt)
    def _():
        fetch(dnext_ref, 1 - slot)

    def wait(r, c):
        for k in range(TOP_K):
            row_copy(slot, 0, k, 0).wait()
        return c

    lax.fori_loop(0, tm, wait, 0)
    route = route_ref[0]
    lane = lax.broadcasted_iota(jnp.int32, route.shape, 1)
    f = None
    for k in range(TOP_K):
        gate = jnp.sum(jnp.where(lane == TOP_K + k, route, 0.0), axis=1, keepdims=True)
        term = gate * rows_buf[slot, k]
        f = term if f is None else f + term
    out_ref[0] = x1_ref[0] + gt2_ref[0] * _rms(f, gpost2_ref[...])


def _final(dest, route, x1, gt2, mod_map, g_post2, ys, tm):
    nb, s, dm = x1.shape
    nt = s // tm
    last = nb * nt - 1
    row = lambda b, i: (b, i, 0)
    mod_block = (1,) + gt2.shape[1:]
    return pl.pallas_call(
        functools.partial(_final_kernel, tm=tm),
        grid=(nb, nt),
        in_specs=[pl.BlockSpec((tm * TOP_K,), lambda b, i: (b * nt + i,), memory_space=pltpu.SMEM),
                  pl.BlockSpec((tm * TOP_K,), lambda b, i: (jnp.minimum(b * nt + i + 1, last),),
                               memory_space=pltpu.SMEM),
                  pl.BlockSpec((1, tm, LANES), row),
                  pl.BlockSpec((1, tm, dm), row),
                  pl.BlockSpec(mod_block, mod_map),
                  _full_spec(g_post2),
                  pl.BlockSpec(memory_space=pl.ANY)],
        out_specs=pl.BlockSpec((1, tm, dm), row),
        out_shape=jax.ShapeDtypeStruct((nb, s, dm), F32),
        scratch_shapes=[pltpu.VMEM((2, TOP_K, tm, dm), F32), pltpu.SemaphoreType.DMA((2,))],
        compiler_params=_cparams("arbitrary", "arbitrary"),
        name="final",
    )(dest, dest, route, x1, gt2, g_post2, ys)


def _transpose_rows(kv):
    lead = kv.shape[:-4]
    n = len(lead)
    perm = tuple(range(n)) + (n + 1, n + 2, n + 3, n)
    return kv.transpose(perm).reshape(lead + (KV_ROWS, kv.shape[-4]))


def _untranspose_rows(kvt):
    lead = kvt.shape[:-2]
    n = len(lead)
    x = kvt.reshape(lead + (2, N_KV_HEADS, HEAD_DIM, kvt.shape[-1]))
    return x.transpose(tuple(range(n)) + (n + 3, n, n + 1, n + 2))


def _layer(xp, xs, cache_cmp, cache_sel, st_win, st_pool, page_table, cp, cs, w):
    nb, s, dm = xp.shape
    bd, qn, _ = xs.shape
    ts_ = bd * qn
    n_pages = page_table.shape[1]
    assert cache_cmp.shape[1] == LANES, "page size must equal the lane count"
    past = n_pages * LANES
    tm_p = 256
    tm_s = min(256, ts_)
    assert s % tm_p == 0 and ts_ % tm_s == 0 and qn % SUBLANES == 0 and qn <= LANES
    n_exp = w['router_w'].shape[1]
    assert n_exp <= LANES

    c_all = jnp.concatenate([cp, cs], axis=0)
    c_all = jnp.pad(c_all, ((0, -c_all.shape[0] % SUBLANES), (0, 0)))
    mods = _ada(c_all, w['w_ada'], w['b_ada'][None])
    mp = mods[:nb].reshape(nb, 6, 1, dm)
    mod_p = [mp[:, j] for j in range(6)]
    ms = jnp.repeat(mods[nb:nb + bd], qn, axis=0).reshape(ts_ // tm_s, tm_s, 6, dm)
    mod_s = [ms[:, :, j] for j in range(6)]
    map_p = lambda b, i: (b, 0, 0)
    map_s = lambda b, i: (i, 0, 0)

    wa, wkvt, pw, aw, order = _prep_in_weights(w['w_in'])
    g_pre1 = w['g_pre1'][None]
    u_p, q_p, mg_p, ng_p, kvt_p = _proj(xp, mod_p[1], mod_p[0], map_p, g_pre1, wa, wkvt, tm_p, pw, aw)
    xs3 = xs.reshape(1, ts_, dm)
    u_s, q_s, mg_s, ng_s, kvt_s = _proj(xs3, mod_s[1], mod_s[0], map_s, g_pre1, wa, wkvt, tm_s, pw, aw)

    cw = _prep_cmp_weights(w['cmp_pe'], w['cmp_w1'], w['cmp_b1'], w['cmp_w2'])
    page_tab = page_table.reshape(-1).astype(jnp.int32)
    ckvt_p = _compress_prompt(kvt_p, cw)
    ckvt_s = _compress_paged(page_tab, n_pages, _transpose_rows(cache_cmp), cw)
    attn_p = _nsa_prompt(q_p, ng_p, ckvt_p, kvt_p, LANES)
    new_t = kvt_s[0].reshape(3 * KV_ROWS, bd, qn).transpose(1, 0, 2)
    new_pad = jnp.pad(new_t, ((0, 0), (0, 0), (0, LANES - qn)))
    attn_s = _nsa_sample(page_tab, n_pages, q_s.reshape(bd, qn, aw), ng_s.reshape(bd, qn, LANES), ckvt_s,
                         _transpose_rows(cache_sel), new_pad[:, KV_ROWS:2 * KV_ROWS], _transpose_rows(st_win),
                         new_pad[:, 2 * KV_ROWS:])

    pool_w, pool_scale = w['pool_w'], w['pool_scale'][None]
    hpt = tm_p // POOL_HIST
    py_p = _pool(u_p, lambda b, i: (b, jnp.maximum(i * hpt - 1, 0), 0), u_p, pool_w, pool_scale, tm_p, 0, True)
    p_buf = st_pool.shape[1]
    assert p_buf == max(POOL_WINDOWS) - 1
    u_sb = u_s.reshape(bd, qn, pw)
    hist_s = jnp.pad(st_pool, ((0, 0), (POOL_HIST - p_buf, 0), (0, 0)))
    py_s = _pool(hist_s, lambda b, i: (b, 0, 0), u_sb, pool_w, pool_scale, qn, past, False)

    rw = jnp.pad(w['router_w'], ((0, 0), (0, LANES - n_exp)))
    rwh = rw.astype(BF16)
    rwl = (rw - rwh.astype(F32)).astype(BF16)
    rb = jnp.pad(w['router_b'], (0, LANES - n_exp), constant_values=NEG_BIG)[None]
    wba = w['w_br_attn'].reshape(N_HEADS, HEAD_DIM, dm)[order].reshape(aw, dm).astype(BF16)
    mweights = [w['g_post1'][None], w['g_pre2'][None], w['w_br_pool'].astype(BF16), wba, w['w_o'].astype(BF16),
                rwh, rwl, rb]
    cnt0 = jnp.zeros((1, LANES), F32)
    x1_p, h2_p, route_p, cnt_p = _merge(xp, py_p, attn_p, mg_p, mod_p[2], mod_p[3], mod_p[4], map_p, mweights,
                                        cnt0, tm_p)
    x1_s, h2_s, route_s, cnt_a = _merge(xs3, py_s.reshape(1, ts_, pw), attn_s.reshape(1, ts_, aw), mg_s, mod_s[2],
                                        mod_s[3], mod_s[4], map_s, mweights, cnt_p, tm_s)

    tp_ = nb * s
    route = jnp.concatenate([route_p.reshape(tp_, LANES), route_s.reshape(ts_, LANES)], axis=0)
    eidx = route[:, 0:TOP_K].astype(jnp.int32)
    rank = route[:, 2 * TOP_K:3 * TOP_K].astype(jnp.int32)
    counts = cnt_a[0, :n_exp].astype(jnp.int32)
    padded = (counts + MOE_ROWS - 1) // MOE_ROWS * MOE_ROWS
    pend = jnp.cumsum(padded)
    dest = ((pend - padded)[eidx] + rank).reshape(-1)
    n_blocks = (tp_ + ts_) * TOP_K // MOE_ROWS + n_exp
    blk_start = jnp.arange(n_blocks, dtype=jnp.int32) * MOE_ROWS
    blk_e = jnp.minimum(jnp.sum((pend[None, :] <= blk_start[:, None]).astype(jnp.int32), axis=1), n_exp - 1)
    n_live = (pend[-1:] // MOE_ROWS).astype(jnp.int32)

    clear_rows = jnp.maximum(pend - MOE_ROWS, 0).astype(jnp.int32)
    assert tp_ % tm_s == 0
    xs_rows = _dispatch(clear_rows, n_live, dest, h2_p.reshape(tp_, dm), h2_s.reshape(ts_, dm), n_blocks, tm_s)
    ys = _experts(blk_e, n_live, xs_rows, w['w_gu'], w['b_gu'], w['w_down'], w['b_down'])
    g_post2 = w['g_post2'][None]
    y_p = _final(dest[:tp_ * TOP_K], route_p, x1_p, mod_p[5], map_p, g_post2, ys, tm_p)
    y_s = _final(dest[tp_ * TOP_K:], route_s, x1_s, mod_s[5], map_s, g_post2, ys, tm_s)

    wlen = min(WINDOW, s)
    kv_p = [_untranspose_rows(kvt_p[:, j * KV_ROWS:(j + 1) * KV_ROWS]) for j in range(2)]
    kvw_p = _untranspose_rows(kvt_p[:, 2 * KV_ROWS:, s - wlen:])
    pool_p = u_p[:, s - min(POOL_HIST - 1, s):]
    kv_s = [_untranspose_rows(new_t[:, j * KV_ROWS:(j + 1) * KV_ROWS]) for j in range(3)]
    w_buf = st_win.shape[1]
    kvw_s = jnp.concatenate([st_win, kv_s[2]], axis=1)[:, -w_buf:]
    pool_s = jnp.concatenate([st_pool, u_sb], axis=1)[:, -p_buf:]
    return y_p, y_s.reshape(bd, qn, dm), (kv_p[0], kv_p[1], kvw_p, pool_p, kv_s[0], kv_s[1], kvw_s, pool_s)


def kernel(x_prompt, x_sample, cache_kv_cmp, cache_kv_sel, state_kv_win, state_pool, page_table, c_prompt, c_sample,
           w_ada, b_ada, g_pre1, g_post1, g_pre2, g_post2, w_in, pool_w, pool_scale, cmp_pe, cmp_w1, cmp_b1, cmp_w2,
           w_br_pool, w_br_attn, w_o, router_w, router_b, w_gu, b_gu, w_down, b_down):
    names = ('w_ada', 'b_ada', 'g_pre1', 'g_post1', 'g_pre2', 'g_post2', 'w_in', 'pool_w', 'pool_scale', 'cmp_pe',
             'cmp_w1', 'cmp_b1', 'cmp_w2', 'w_br_pool', 'w_br_attn', 'w_o', 'router_w', 'router_b', 'w_gu', 'b_gu',
             'w_down', 'b_down')
    stacked = (w_ada, b_ada, g_pre1, g_post1, g_pre2, g_post2, w_in, pool_w, pool_scale, cmp_pe, cmp_w1, cmp_b1,
               cmp_w2, w_br_pool, w_br_attn, w_o, router_w, router_b, w_gu, b_gu, w_down, b_down)
    xp, xs = x_prompt, x_sample
    states = []
    for l in range(w_ada.shape[0]):
        w = {n: a[l] for n, a in zip(names, stacked)}
        xp, xs, st = _layer(xp, xs, cache_kv_cmp[l], cache_kv_sel[l], state_kv_win[l], state_pool[l], page_table,
                            c_prompt, c_sample, w)
        states.append(st)
    return (xp, xs) + tuple(jnp.stack([st[j] for st in states]) for j in range(8))
```

```python
import functools

import jax
import jax.numpy as jnp
import numpy as np
from jax import lax
from jax.experimental import pallas as pl
from jax.experimental.pallas import tpu as pltpu

F32 = jnp.float32
BF16 = jnp.bfloat16

POOL_WINDOWS = (2, 4, 8, 16)
N_HEADS = 8
N_KV_HEADS = 2
HEAD_DIM = 64
GQ = N_HEADS // N_KV_HEADS
CMP_BLOCK = 32
CMP_STRIDE = 16
SEL_BLOCK = 64
SUB_PER_SEL = SEL_BLOCK // CMP_STRIDE
N_SEL = 16
N_LOCAL = 2
WINDOW = 512
TOP_K = 4
SWIGLU_LIMIT = 7.0
SWIGLU_ALPHA = 1.702
RMS_EPS = 1e-6

LANES = 128
SUBLANES = 8
VMEM_LIMIT = 56 * 1024 * 1024
NEG_BIG = -1e30
M_INIT = -1e29
MXU_DEPTH = 256
FAR = 1 << 30
KV_ROWS = 2 * N_KV_HEADS * HEAD_DIM
HALF = N_KV_HEADS * HEAD_DIM
POOL_HIST = 16
MOE_ROWS = 512
PAGE_UNROLL = 16
SAMPLE_TILE_PAGES = 64


def _cparams(*sem):
    return pltpu.CompilerParams(dimension_semantics=sem, vmem_limit_bytes=VMEM_LIMIT)


def _full_spec(w):
    nd = w.ndim
    return pl.BlockSpec(w.shape, lambda *a: (0,) * nd)


def _dot(a, b):
    return jnp.dot(a, b, preferred_element_type=F32)


def _dot_nt(a, b):
    return lax.dot_general(a, b, (((1,), (1,)), ((), ())), preferred_element_type=F32)


def _split3(x):
    hi = x.astype(BF16)
    r1 = x - hi.astype(F32)
    mid = r1.astype(BF16)
    lo = (r1 - mid.astype(F32)).astype(BF16)
    return hi, mid, lo


def _dot_exact_rhs(x, m_bf16):
    hi, mid, lo = _split3(x)
    return _dot(hi, m_bf16) + _dot(mid, m_bf16) + _dot(lo, m_bf16)


def _mod_rows(ref, tm):
    v = ref[0]
    r, dm = v.shape
    if r in (1, tm):
        return v
    return jnp.broadcast_to(v[:, None, :], (r, tm // r, dm)).reshape(tm, dm)


def _rms(x, g):
    return x * lax.rsqrt(jnp.mean(x * x, axis=-1, keepdims=True) + RMS_EPS) * g


def _ada_kernel(c_ref, w_ref, b_ref, o_ref):
    o_ref[...] = _dot(c_ref[...].astype(BF16), w_ref[...].astype(BF16)) + b_ref[...]


def _ada(c_all, w_ada, b_ada):
    m, d = c_all.shape
    n = w_ada.shape[1]
    tn = 1024
    return pl.pallas_call(
        _ada_kernel,
        grid=(n // tn,),
        in_specs=[pl.BlockSpec((m, d), lambda j: (0, 0)),
                  pl.BlockSpec((d, tn), lambda j: (0, j)),
                  pl.BlockSpec((1, tn), lambda j: (0, j))],
        out_specs=pl.BlockSpec((m, tn), lambda j: (0, j)),
        out_shape=jax.ShapeDtypeStruct((m, n), F32),
        compiler_params=_cparams("arbitrary"),
        name="ada",
    )(c_all, w_ada, b_ada)


def _proj_kernel(x_ref, sc_ref, sh_ref, g_ref, wa_ref, wkv_ref,
                 u_ref, q_ref, mg_ref, ng_ref, kvt_ref, *, pw, aw, dm):
    tm = x_ref.shape[1]
    hb = (_rms(x_ref[0], g_ref[...]) * (1.0 + _mod_rows(sc_ref, tm)) + _mod_rows(sh_ref, tm)).astype(BF16)
    u_ref[0] = _dot(hb, wa_ref[:, 0:pw])
    q_ref[0] = _dot(hb, wa_ref[:, pw:pw + aw])
    o = pw + aw
    mg_ref[0] = jax.nn.sigmoid(_dot(hb, wa_ref[:, o:o + 2 * dm]))
    ng_ref[0] = jax.nn.sigmoid(_dot(hb, wa_ref[:, o + 2 * dm:o + 2 * dm + LANES]))
    kvt_ref[0] = _dot_nt(wkv_ref[...], hb)


def _proj(x3, sc, sh, mod_map, g_pre1, wa, wkvt, tm, pw, aw):
    nb, s, dm = x3.shape
    nt = s // tm
    kern = functools.partial(_proj_kernel, pw=pw, aw=aw, dm=dm)
    mod_block = (1,) + sc.shape[1:]
    row = lambda b, i: (b, i, 0)
    return pl.pallas_call(
        kern,
        grid=(nb, nt),
        in_specs=[pl.BlockSpec((1, tm, dm), row),
                  pl.BlockSpec(mod_block, mod_map),
                  pl.BlockSpec(mod_block, mod_map),
                  _full_spec(g_pre1), _full_spec(wa), _full_spec(wkvt)],
        out_specs=[pl.BlockSpec((1, tm, pw), row),
                   pl.BlockSpec((1, tm, aw), row),
                   pl.BlockSpec((1, tm, 2 * dm), row),
                   pl.BlockSpec((1, tm, LANES), row),
                   pl.BlockSpec((1, 3 * KV_ROWS, tm), lambda b, i: (b, 0, i))],
        out_shape=[jax.ShapeDtypeStruct((nb, s, pw), F32),
                   jax.ShapeDtypeStruct((nb, s, aw), F32),
                   jax.ShapeDtypeStruct((nb, s, 2 * dm), F32),
                   jax.ShapeDtypeStruct((nb, s, LANES), F32),
                   jax.ShapeDtypeStruct((nb, 3 * KV_ROWS, s), F32)],
        compiler_params=_cparams("arbitrary", "arbitrary"),
        name="proj",
    )(x3, sc, sh, g_pre1, wa, wkvt)


def _prep_in_weights(w_in):
    dm = w_in.shape[0]
    pw = dm // 2
    aw = N_HEADS * HEAD_DIM
    off_kvc = pw + aw
    off_ng = off_kvc + 3 * KV_ROWS
    off_mg = off_ng + 3 * N_HEADS
    order = np.array([k * GQ + g for g in range(GQ) for k in range(N_KV_HEADS)])
    wq = w_in[:, pw:off_kvc].reshape(dm, N_HEADS, HEAD_DIM)[:, order].reshape(dm, aw) * (HEAD_DIM ** -0.5)
    wng = jnp.pad(w_in[:, off_ng:off_mg], ((0, 0), (0, LANES - 3 * N_HEADS)))
    wa = jnp.concatenate([w_in[:, :pw], wq, w_in[:, off_mg:], wng], axis=1).astype(BF16)
    wkvt = w_in[:, off_kvc:off_ng].T.astype(BF16)
    return wa, wkvt, pw, aw, order


def _compress_core(get_page, n_pages, perm_ref, wp_ref, w2t_ref, pe_ref, w1f_ref, b1_ref, s2_ref, out_ref):
    sub_pp = LANES // CMP_STRIDE
    n_sub = n_pages * sub_pp
    r_cmp = CMP_BLOCK // CMP_STRIDE
    hid = w1f_ref.shape[-1]

    def page_body(p, carry):
        pg = get_page(p).astype(BF16)
        t = _dot_nt(perm_ref[...], pg)
        row0 = pl.multiple_of(p * sub_pp, sub_pp)
        for s in range(2):
            for j in range(CMP_STRIDE):
                s2_ref[s, j // 2, pl.ds(row0, sub_pp), (j % 2) * HALF:(j % 2 + 1) * HALF] = (
                    t[sub_pp * j:sub_pp * (j + 1), s * HALF:(s + 1) * HALF])
        return carry

    lax.fori_loop(0, n_pages, page_body, 0, unroll=PAGE_UNROLL)
    for s in range(2):
        acc = None
        for jp in range(CMP_STRIDE // 2):
            d = _dot(s2_ref[s, jp].astype(BF16), wp_ref[s, jp])
            acc = d if acc is None else acc + d
        bias = _dot(pe_ref[s], w1f_ref[s])[0:1] + b1_ref[s]
        bias = jnp.concatenate([bias] * N_KV_HEADS, axis=1)
        w = N_KV_HEADS * hid
        hpre = acc[:, :w] + bias
        for r in range(1, r_cmp):
            hpre = hpre + pltpu.roll(acc[:, r * w:(r + 1) * w], n_sub - r, axis=0)
        g = jax.nn.gelu(hpre).astype(BF16)
        out_ref[s * HALF:(s + 1) * HALF, :] = _dot_nt(w2t_ref[s], g)


def _compress_prompt_kernel(kv_ref, perm_ref, wp_ref, w2t_ref, pe_ref, w1f_ref, b1_ref, out_ref, s2_ref, *, n_pages):
    def get_page(p):
        return kv_ref[0, :, pl.ds(pl.multiple_of(p * LANES, LANES), LANES)]
    _compress_core(get_page, n_pages, perm_ref, wp_ref, w2t_ref, pe_ref, w1f_ref, b1_ref, s2_ref, out_ref.at[0])


def _page_fetch(pt_ref, cache_hbm, buf, sem, b, slot, n_pages):
    def start(p, c):
        pltpu.make_async_copy(cache_hbm.at[pt_ref[b * n_pages + p]], buf.at[slot, p], sem.at[slot]).start()
        return c
    lax.fori_loop(0, n_pages, start, 0)


def _page_wait(cache_hbm, buf, sem, slot, n_pages):
    def wait(p, c):
        pltpu.make_async_copy(cache_hbm.at[0], buf.at[slot, p], sem.at[slot]).wait()
        return c
    lax.fori_loop(0, n_pages, wait, 0)


def _paged_prologue(pt_ref, cache_hbm, buf, sem, n_pages):
    b = pl.program_id(0)
    slot = b % 2

    @pl.when(b == 0)
    def _():
        _page_fetch(pt_ref, cache_hbm, buf, sem, 0, 0, n_pages)

    @pl.when(b + 1 < pl.num_programs(0))
    def _():
        _page_fetch(pt_ref, cache_hbm, buf, sem, b + 1, 1 - slot, n_pages)

    _page_wait(cache_hbm, buf, sem, slot, n_pages)
    return slot


def _compress_paged_kernel(pt_ref, cache_hbm, perm_ref, wp_ref, w2t_ref, pe_ref, w1f_ref, b1_ref, out_ref,
                           buf, sem, s2_ref, *, n_pages):
    slot = _paged_prologue(pt_ref, cache_hbm, buf, sem, n_pages)
    _compress_core(lambda p: buf[slot, p], n_pages, perm_ref, wp_ref, w2t_ref, pe_ref, w1f_ref, b1_ref,
                   s2_ref, out_ref.at[0])


def _prep_cmp_weights(cmp_pe, cmp_w1, cmp_b1, cmp_w2):
    hid = cmp_w1.shape[-1]
    r_cmp = CMP_BLOCK // CMP_STRIDE
    eye = jnp.eye(N_KV_HEADS, dtype=F32)
    w1r = cmp_w1.reshape(2, r_cmp, CMP_STRIDE // 2, 2, HEAD_DIM, hid)
    wp = jnp.einsum('srpjdh,kc->spjkdrch', w1r, eye).reshape(
        2, CMP_STRIDE // 2, 2 * HALF, r_cmp * N_KV_HEADS * hid).astype(BF16)
    w2t = jnp.einsum('shd,kc->skdch', cmp_w2, eye).reshape(2, HALF, N_KV_HEADS * hid).astype(BF16)
    pe = jnp.broadcast_to(cmp_pe.reshape(2, 1, CMP_BLOCK * HEAD_DIM),
                          (2, SUBLANES, CMP_BLOCK * HEAD_DIM)).astype(BF16)
    w1f = cmp_w1.reshape(2, CMP_BLOCK * HEAD_DIM, hid).astype(BF16)
    b1 = cmp_b1.reshape(2, 1, hid)
    sub_pp = LANES // CMP_STRIDE
    x = np.arange(LANES)
    perm = np.zeros((LANES, LANES), np.float32)
    perm[x, (x % sub_pp) * CMP_STRIDE + x // sub_pp] = 1.0
    return (jnp.asarray(perm, BF16), wp, w2t, pe, w1f, b1)


def _compress_prompt(kvt, cw):
    nb, _, s = kvt.shape
    n_pages = s // LANES
    n_sub = s // CMP_STRIDE
    return pl.pallas_call(
        functools.partial(_compress_prompt_kernel, n_pages=n_pages),
        grid=(nb,),
        in_specs=[pl.BlockSpec((1, KV_ROWS, s), lambda b: (b, 0, 0))] + [_full_spec(w) for w in cw],
        out_specs=pl.BlockSpec((1, KV_ROWS, n_sub), lambda b: (b, 0, 0)),
        out_shape=jax.ShapeDtypeStruct((nb, KV_ROWS, n_sub), F32),
        scratch_shapes=[pltpu.VMEM((2, CMP_STRIDE // 2, n_sub, 2 * HALF), F32)],
        compiler_params=_cparams("arbitrary"),
        name="compress_prompt",
    )(kvt, *cw)


def _compress_paged(page_tab, n_pages, cache_t, cw):
    nb = page_tab.shape[0] // n_pages
    n_sub = n_pages * (LANES // CMP_STRIDE)
    return pl.pallas_call(
        functools.partial(_compress_paged_kernel, n_pages=n_pages),
        grid_spec=pltpu.PrefetchScalarGridSpec(
            num_scalar_prefetch=1,
            grid=(nb,),
            in_specs=[pl.BlockSpec(memory_space=pl.ANY)] + [_full_spec(w) for w in cw],
            out_specs=pl.BlockSpec((1, KV_ROWS, n_sub), lambda b, pt: (b, 0, 0)),
            scratch_shapes=[pltpu.VMEM((2, n_pages, KV_ROWS, LANES), F32),
                            pltpu.SemaphoreType.DMA((2,)),
                            pltpu.VMEM((2, CMP_STRIDE // 2, n_sub, 2 * HALF), F32)]),
        out_shape=jax.ShapeDtypeStruct((nb, KV_ROWS, n_sub), F32),
        compiler_params=_cparams("arbitrary"),
        name="compress_paged",
    )(page_tab, cache_t, *cw)


def _stack_heads(fn):
    return jnp.concatenate([fn(k, g) for k in range(N_KV_HEADS) for g in range(GQ)], axis=0)


def _alibi_slope(k, g):
    return 2.0 ** (-8.0 * (k * GQ + g + 1) / N_HEADS)


def _nsa_block(q, sgate, t0, ckvt_ref, mmat_ref, bmat_ref, sel_tile, n_sel_tiles, sel_tk, sel_last, win_tile,
               m_ref, acc_ref, out_ref, *, qt, nblk_pad, n_pick, active=None):
    t_pos = t0 + lax.broadcasted_iota(jnp.int32, (qt, 1), 0)
    lane_half = lax.broadcasted_iota(jnp.int32, (qt, LANES), 1) // HEAD_DIM

    qp = _stack_heads(lambda k, g: jnp.where(lane_half == k, q[:, g * LANES:(g + 1) * LANES], 0.0)).astype(BF16)

    def alibi(distf):
        return _stack_heads(lambda k, g: _alibi_slope(k, g) * distf)

    def stack_masks(mk):
        return jnp.concatenate([mk[k] for k in range(N_KV_HEADS) for _ in range(GQ)], axis=0) > 0.5

    nc = ckvt_ref.shape[-1]
    c_end = lax.broadcasted_iota(jnp.int32, (1, nc), 1) * CMP_STRIDE + (CMP_BLOCK - 1)
    dist_c = t_pos - c_end
    valid_c = jnp.where(dist_c >= 0, 1.0, 0.0)
    mask_c = stack_masks([valid_c] * N_KV_HEADS)
    s = _dot(qp, ckvt_ref[0:HALF, :].astype(BF16)) - alibi(dist_c.astype(F32))
    s = jnp.where(mask_c, s, NEG_BIG)
    p = jnp.where(mask_c, jnp.exp(s - jnp.max(s, axis=1, keepdims=True)), 0.0)
    p = p / jnp.maximum(jnp.sum(p, axis=1, keepdims=True), 1e-30)
    o_c = _dot_nt(p.astype(BF16), ckvt_ref[HALF:2 * HALF, :].astype(BF16))

    blk = lax.broadcasted_iota(jnp.int32, (qt, nblk_pad), 1)
    cur = t_pos // SEL_BLOCK
    forced = (blk == 0) | ((blk > cur - N_LOCAL) & (blk <= cur))
    free = (blk >= 1) & (blk <= cur - N_LOCAL)
    blkf = blk.astype(F32)
    sel = []
    for k in range(N_KV_HEADS):
        imp = p[k * GQ * qt:(k * GQ + 1) * qt]
        for g in range(1, GQ):
            imp = imp + p[(k * GQ + g) * qt:(k * GQ + g + 1) * qt]
        score = jnp.where(free, _dot_exact_rhs(imp, mmat_ref[...]), -jnp.inf)
        chosen = jnp.where(forced, 1.0, 0.0)
        for _ in range(n_pick):
            best = jnp.max(score, axis=1, keepdims=True)
            idx = jnp.min(jnp.where(score == best, blkf, float(nblk_pad)), axis=1, keepdims=True)
            hit = blkf == idx
            chosen = jnp.where(hit, 1.0, chosen)
            score = jnp.where(hit, -jnp.inf, score)
        sel.append(chosen)

    rk = GQ * qt
    lane = lax.broadcasted_iota(jnp.int32, (1, LANES), 1)
    t_rows = jnp.concatenate([t_pos] * GQ, axis=0)

    def q_feat(k):
        parts = []
        for g in range(GQ):
            sl = _alibi_slope(k, g)
            base = (1 - k) * HEAD_DIM
            ext = jnp.where(lane == base, SEL_BLOCK * sl, jnp.where(lane == base + 1, sl, 0.0))
            parts.append(jnp.where(lane_half == k, q[:, g * LANES:(g + 1) * LANES], ext))
        return jnp.concatenate(parts, axis=0).astype(BF16)

    qf = [q_feat(k) for k in range(N_KV_HEADS)]

    def k_aug(kt, pos, use_sel):
        tk = kt.shape[-1]
        hi = lax.shift_right_arithmetic(pos, SEL_BLOCK.bit_length() - 1)
        lo = pos & (SEL_BLOCK - 1)
        r = lax.broadcasted_iota(jnp.int32, (HEAD_DIM, tk), 0)
        ext = jnp.where(r == 0, hi.astype(F32), jnp.where(r == 1, lo.astype(F32), 0.0)).astype(BF16)
        if use_sel:
            onehot = jnp.where(lax.broadcasted_iota(jnp.int32, (nblk_pad, 1), 0) == hi, 1.0, 0.0).astype(BF16)
        else:
            onehot = jnp.zeros((nblk_pad, tk), BF16)
        kb = kt.astype(BF16)
        return [jnp.concatenate([onehot, kb[0:HEAD_DIM], ext], axis=0),
                jnp.concatenate([onehot, ext, kb[HEAD_DIM:HALF]], axis=0)]

    m_ref[...] = jnp.full(m_ref.shape, M_INIT, F32)
    acc_ref[...] = jnp.zeros(acc_ref.shape, F32)

    def flash(br, qa, kt, vt, pos, use_sel, cond):
        ka = k_aug(kt, pos, use_sel)
        va = jnp.concatenate([vt.astype(BF16), jnp.ones((2 * SUBLANES, kt.shape[-1]), BF16)], axis=0)
        for k in range(N_KV_HEADS):
            sc = _dot(qa[k], ka[k])
            if cond is not None:
                sc = jnp.where(cond, sc, NEG_BIG)
            m_old = m_ref[br, k]
            m_new = jnp.maximum(m_old, jnp.max(sc, axis=1, keepdims=True))
            pr = jnp.exp(sc - m_new).astype(BF16)
            acc_ref[br, k] = jnp.exp(m_old - m_new) * acc_ref[br, k] + _dot_nt(pr, va)
            m_ref[br, k] = m_new

    def result(br):
        out = []
        for k in range(N_KV_HEADS):
            a = acc_ref[br, k]
            out.append(a[:, 0:HALF] / jnp.maximum(a[:, HALF:HALF + 1], 1e-30))
        return out

    qa_w = [jnp.concatenate([jnp.zeros((rk, nblk_pad), BF16), qf[k]], axis=1) for k in range(N_KV_HEADS)]
    kt, vt, pos = win_tile
    dist = t_rows - pos
    flash(1, qa_w, kt, vt, pos, False, (dist >= 0) & (dist <= WINDOW))

    qa_s = []
    for k in range(N_KV_HEADS):
        selneg = jnp.where(sel[k] > 0.5, 0.0, NEG_BIG).astype(BF16)
        qa_s.append(jnp.concatenate([jnp.concatenate([selneg] * GQ, axis=0), qf[k]], axis=1))

    def sel_step(j):
        kt, vt = sel_tile(j)
        flash(0, qa_s, kt, vt, j * sel_tk + lax.broadcasted_iota(jnp.int32, (1, sel_tk), 1), True, None)

    if isinstance(n_sel_tiles, int):
        for j in range(n_sel_tiles):
            sel_step(j)
    else:
        act_ref, max_tiles = active
        tile_of_blk = lax.broadcasted_iota(jnp.int32, (nblk_pad, 1), 0) // (sel_tk // SEL_BLOCK)
        group = jnp.where(tile_of_blk == lax.broadcasted_iota(jnp.int32, (1, LANES), 1), 1.0, 0.0).astype(BF16)
        any_sel = jnp.max(jnp.maximum(sel[0], sel[1]), axis=0, keepdims=True)
        tile_any = _dot(jnp.broadcast_to(any_sel, (SUBLANES, nblk_pad)).astype(BF16), group)
        cnt = jnp.int32(0)
        for j in range(max_tiles):
            act_ref[cnt] = j
            keep = (tile_any[0, j] > 0.5) & (j < n_sel_tiles)
            cnt = cnt + keep.astype(jnp.int32)

        def pair_body(p, c):
            sel_step(act_ref[2 * p])
            sel_step(act_ref[2 * p + 1])
            return c

        lax.fori_loop(0, cnt // 2, pair_body, 0)

        @pl.when(cnt % 2 == 1)
        def _():
            sel_step(act_ref[cnt - 1])
    kt, vt, pos, use_sel = sel_last
    flash(0, qa_s, kt, vt, pos, use_sel, t_rows >= pos)
    o_s = result(0)
    o_w = result(1)

    o_c = [o_c[0:rk], o_c[rk:2 * rk]]
    g3 = _split3(sgate)
    for g in range(GQ):
        slab = None
        for c, o in enumerate((o_c, o_s, o_w)):
            bm = bmat_ref[g * 3 + c]
            gate = _dot(g3[0], bm) + _dot(g3[1], bm) + _dot(g3[2], bm)
            val = jnp.where(lane_half == 0, o[0][g * qt:(g + 1) * qt], o[1][g * qt:(g + 1) * qt])
            slab = gate * val if slab is None else slab + gate * val
        out_ref[:, g * LANES:(g + 1) * LANES] = slab


def _nsa_consts(nc, nblk_pad):
    j = np.arange(nc)[:, None]
    lo = SUB_PER_SEL * np.arange(nblk_pad)[None, :]
    mm = (np.where((j >= lo) & (j < lo + SUB_PER_SEL - 1), 2.0, 0.0)
          + np.where((j == lo - 1) | (j == lo + SUB_PER_SEL - 1), 1.0, 0.0))
    col = np.arange(LANES)[:, None]
    lane = np.arange(LANES)[None, :]
    bm = np.stack([(col == ((lane // HEAD_DIM) * GQ + g) * 3 + c)
                   for g in range(GQ) for c in range(3)]).astype(np.float32)
    return jnp.asarray(mm, BF16), jnp.asarray(bm, BF16)


def _lane_iota(n):
    return lax.broadcasted_iota(jnp.int32, (1, n), 1)


def _flash_scratch(qt, lead=()):
    rk = GQ * qt
    lead = (lead,) if isinstance(lead, int) else lead
    return [pltpu.VMEM(lead + (2, N_KV_HEADS, rk, 1), F32),
            pltpu.VMEM(lead + (2, N_KV_HEADS, rk, HALF + 2 * SUBLANES), F32)]


def _nsa_prompt_kernel(*refs, qt, tk, n_win, nblk_pad, n_pick):
    q_ref, ng_ref, ckvt_ref, kvs_ref = refs[:4]
    win_refs = refs[4:4 + n_win]
    mmat_ref, bmat_ref, out_ref, m_ref, acc_ref, act_ref = refs[4 + n_win:]
    i = pl.program_id(1)

    def sel_tile(j):
        off = pl.multiple_of(j * tk, LANES)
        return kvs_ref[0, 0:HALF, pl.ds(off, tk)], kvs_ref[0, HALF:2 * HALF, pl.ds(off, tk)]

    n_past = (i * qt) // tk
    sel_last = sel_tile(n_past) + (n_past * tk + _lane_iota(tk), True)
    pieces = [w[0] for w in win_refs]
    pos = []
    for jj in range(n_win):
        bi = i - (n_win - 1) + jj
        pos.append(jnp.where(bi >= 0, bi * qt, -FAR) + _lane_iota(qt))
    win = (jnp.concatenate([p[0:HALF] for p in pieces], axis=1),
           jnp.concatenate([p[HALF:2 * HALF] for p in pieces], axis=1), jnp.concatenate(pos, axis=1))
    _nsa_block(q_ref[0], ng_ref[0], i * qt, ckvt_ref.at[0], mmat_ref, bmat_ref, sel_tile, n_past, tk, sel_last, win,
               m_ref, acc_ref, out_ref.at[0], qt=qt, nblk_pad=nblk_pad, n_pick=n_pick,
               active=(act_ref, act_ref.shape[0]))


def _nsa_prompt(q, ng, ckvt, kvt, qt):
    nb, s, aw = q.shape
    nc = ckvt.shape[-1]
    nblk = -(-s // SEL_BLOCK)
    nblk_pad = -(-nblk // LANES) * LANES
    assert nblk_pad + HALF <= MXU_DEPTH, "selection mask + features must fit one MXU contraction pass"
    n_pick = max(min(N_SEL, nblk) - (N_LOCAL + 1), 0)
    n_win = WINDOW // qt + 1
    tk = min(4 * qt, s)
    assert s % tk == 0 and tk % qt == 0
    mmat, bmat = _nsa_consts(nc, nblk_pad)
    row = lambda b, i: (b, i, 0)
    win_specs = [pl.BlockSpec((1, KV_ROWS, qt), (lambda jj: (lambda b, i: (b, 2, jnp.maximum(i - (n_win - 1) + jj, 0))))(jj))
                 for jj in range(n_win)]
    return pl.pallas_call(
        functools.partial(_nsa_prompt_kernel, qt=qt, tk=tk, n_win=n_win, nblk_pad=nblk_pad, n_pick=n_pick),
        grid=(nb, s // qt),
        in_specs=[pl.BlockSpec((1, qt, aw), row),
                  pl.BlockSpec((1, qt, LANES), row),
                  pl.BlockSpec((1, KV_ROWS, nc), lambda b, i: (b, 0, 0)),
                  pl.BlockSpec((1, KV_ROWS, s), lambda b, i: (b, 1, 0))] + win_specs
                 + [_full_spec(mmat), _full_spec(bmat)],
        out_specs=pl.BlockSpec((1, qt, aw), row),
        out_shape=jax.ShapeDtypeStruct((nb, s, aw), F32),
        scratch_shapes=_flash_scratch(qt) + [pltpu.SMEM((s // tk,), jnp.int32)],
        compiler_params=_cparams("arbitrary", "arbitrary"),
        name="nsa_prompt",
    )(q, ng, ckvt, kvt, *([kvt] * n_win), mmat, bmat)


def _nsa_sample_kernel(pt_ref, q_ref, ng_ref, ckvt_ref, cache_hbm, ksn_ref, wst_ref, kwn_ref, mmat_ref, bmat_ref,
                       out_ref, buf, sem, m_ref, acc_ref, *, qt, nbs, n_pages, ppt, nblk_pad, n_pick):
    slot = _paged_prologue(pt_ref, cache_hbm, buf, sem, nbs * n_pages)
    past = n_pages * LANES
    wbuf = wst_ref.shape[-1]
    for bb in range(nbs):
        def sel_tile(j, bb=bb):
            p0 = bb * n_pages + j * ppt
            ks = [buf[slot, p0 + pp, 0:HALF, :] for pp in range(ppt)]
            vs = [buf[slot, p0 + pp, HALF:2 * HALF, :] for pp in range(ppt)]
            return jnp.concatenate(ks, axis=1), jnp.concatenate(vs, axis=1)

        sel_last = (ksn_ref[bb, 0:HALF, :], ksn_ref[bb, HALF:2 * HALF, :], past + _lane_iota(LANES), False)
        win = (jnp.concatenate([wst_ref[bb, 0:HALF, :], kwn_ref[bb, 0:HALF, :]], axis=1),
               jnp.concatenate([wst_ref[bb, HALF:2 * HALF, :], kwn_ref[bb, HALF:2 * HALF, :]], axis=1),
               past - wbuf + _lane_iota(wbuf + LANES))
        _nsa_block(q_ref[bb], ng_ref[bb], past, ckvt_ref.at[bb], mmat_ref, bmat_ref, sel_tile, n_pages // ppt,
                   ppt * LANES, sel_last, win, m_ref.at[bb], acc_ref.at[bb], out_ref.at[bb],
                   qt=qt, nblk_pad=nblk_pad, n_pick=n_pick)


def _nsa_sample(page_tab, n_pages, q, ng, ckvt, cache_t, ks_new, w_state, kw_new):
    nb, qt, aw = q.shape
    nc = ckvt.shape[-1]
    past = n_pages * LANES
    assert past % SEL_BLOCK == 0 and qt <= SEL_BLOCK
    nblk_pad = -(-(past // SEL_BLOCK) // LANES) * LANES
    assert nblk_pad + HALF <= MXU_DEPTH
    n_pick = max(min(N_SEL, past // SEL_BLOCK + 1) - (N_LOCAL + 1), 0)
    ppt = min(SAMPLE_TILE_PAGES, n_pages)
    nbs = 2 if nb % 2 == 0 else 1
    assert n_pages % ppt == 0
    mmat, bmat = _nsa_consts(nc, nblk_pad)
    b3 = lambda b, pt: (b, 0, 0)
    return pl.pallas_call(
        functools.partial(_nsa_sample_kernel, qt=qt, nbs=nbs, n_pages=n_pages, ppt=ppt, nblk_pad=nblk_pad,
                          n_pick=n_pick),
        grid_spec=pltpu.PrefetchScalarGridSpec(
            num_scalar_prefetch=1,
            grid=(nb // nbs,),
            in_specs=[pl.BlockSpec((nbs, qt, aw), b3),
                      pl.BlockSpec((nbs, qt, LANES), b3),
                      pl.BlockSpec((nbs, KV_ROWS, nc), b3),
                      pl.BlockSpec(memory_space=pl.ANY),
                      pl.BlockSpec((nbs, KV_ROWS, LANES), b3),
                      pl.BlockSpec((nbs, KV_ROWS, w_state.shape[-1]), b3),
                      pl.BlockSpec((nbs, KV_ROWS, LANES), b3),
                      _full_spec(mmat), _full_spec(bmat)],
            out_specs=pl.BlockSpec((nbs, qt, aw), b3),
            scratch_shapes=[pltpu.VMEM((2, nbs * n_pages, KV_ROWS, LANES), F32),
                            pltpu.SemaphoreType.DMA((2,))] + _flash_scratch(qt, nbs)),
        out_shape=jax.ShapeDtypeStruct((nb, qt, aw), F32),
        compiler_params=_cparams("arbitrary"),
        name="nsa_sample",
    )(page_tab, q, ng, ckvt, cache_t, ks_new, w_state, kw_new, mmat, bmat)


def _pool_kernel(hist_ref, cur_ref, pw_ref, ps_ref, out_ref, *, ts, pos_base, zero_first):
    i = pl.program_id(1)
    hist = hist_ref[0]
    if zero_first:
        hist = jnp.where(i == 0, 0.0, hist)
    cur = cur_ref[0]
    ext = jnp.concatenate([hist, cur], axis=0)
    pos = pos_base + i * ts + lax.broadcasted_iota(jnp.int32, (ts, 1), 0)
    ys = []
    for gi, w in enumerate(POOL_WINDOWS):
        lanes = slice(gi * LANES, (gi + 1) * LANES)
        acc = ext[:, lanes]
        step = 1
        while step < w:
            acc = acc + pltpu.roll(acc, step, axis=0)
            step *= 2
        cnt = jnp.minimum(pos + 1, w).astype(F32)
        dlt = acc[POOL_HIST:] / cnt - cur[:, lanes]
        ys.append(_dot(dlt.astype(BF16), pw_ref[gi].astype(BF16)))
    out_ref[0] = jnp.concatenate(ys, axis=1) * ps_ref[...]


def _pool(hist_arr, hist_map, u, pool_w, pool_scale, ts, pos_base, zero_first):
    nb, s, pw = u.shape
    assert pw == len(POOL_WINDOWS) * LANES and all(w & (w - 1) == 0 and w <= POOL_HIST for w in POOL_WINDOWS)
    row = lambda b, i: (b, i, 0)
    return pl.pallas_call(
        functools.partial(_pool_kernel, ts=ts, pos_base=pos_base, zero_first=zero_first),
        grid=(nb, s // ts),
        in_specs=[pl.BlockSpec((1, POOL_HIST, pw), hist_map),
                  pl.BlockSpec((1, ts, pw), row),
                  _full_spec(pool_w), _full_spec(pool_scale)],
        out_specs=pl.BlockSpec((1, ts, pw), row),
        out_shape=jax.ShapeDtypeStruct((nb, s, pw), F32),
        compiler_params=_cparams("arbitrary", "arbitrary"),
        name="pool",
    )(hist_arr, u, pool_w, pool_scale)


def _merge_kernel(x_ref, py_ref, ay_ref, mg_ref, gt1_ref, sh2_ref, sc2_ref, gpost1_ref, gpre2_ref,
                  wbp_ref, wba_ref, wo_ref, rwh_ref, rwl_ref, rb_ref, tri_ref, cnt0_ref,
                  x1_ref, h2_ref, route_ref, cnt_ref, *, dm):
    first = (pl.program_id(0) == 0) & (pl.program_id(1) == 0)

    @pl.when(first)
    def _():
        cnt_ref[...] = cnt0_ref[...]

    bp = _dot(py_ref[0].astype(BF16), wbp_ref[...])
    ba = _dot(ay_ref[0].astype(BF16), wba_ref[...])
    merged = mg_ref[0, :, 0:dm] * bp + mg_ref[0, :, dm:2 * dm] * ba
    mix = _dot(merged.astype(BF16), wo_ref[...])
    tm = x_ref.shape[1]
    x1 = x_ref[0] + _mod_rows(gt1_ref, tm) * _rms(mix, gpost1_ref[...])
    h2 = _rms(x1, gpre2_ref[...]) * (1.0 + _mod_rows(sc2_ref, tm)) + _mod_rows(sh2_ref, tm)
    x1_ref[0] = x1
    h2_ref[0] = h2

    hh = h2.astype(BF16)
    hl = (h2 - hh.astype(F32)).astype(BF16)
    logits = _dot(hh, rwh_ref[...]) + _dot(hl, rwh_ref[...]) + _dot(hh, rwl_ref[...]) + rb_ref[...]
    tm = logits.shape[0]
    lane = lax.broadcasted_iota(jnp.int32, (tm, LANES), 1)
    lanef = lane.astype(F32)
    hits, vals, idxs = [], [], []
    for _ in range(TOP_K):
        best = jnp.max(logits, axis=1, keepdims=True)
        idx = jnp.min(jnp.where(logits == best, lanef, float(LANES)), axis=1, keepdims=True)
        hit = lanef == idx
        hits.append(hit)
        vals.append(best)
        idxs.append(idx)
        logits = jnp.where(hit, -jnp.inf, logits)
    ex = [jnp.exp(v - vals[0]) for v in vals]
    den = ex[0]
    for e in ex[1:]:
        den = den + e
    onehot = jnp.where(hits[0], 1.0, 0.0)
    for h in hits[1:]:
        onehot = onehot + jnp.where(h, 1.0, 0.0)
    before = _dot(tri_ref[...], onehot.astype(BF16)) + cnt_ref[...]
    route = jnp.zeros((tm, LANES), F32)
    for k in range(TOP_K):
        rank = jnp.sum(jnp.where(hits[k], before, 0.0), axis=1, keepdims=True)
        route = route + jnp.where(lane == k, idxs[k], 0.0)
        route = route + jnp.where(lane == TOP_K + k, ex[k] / den, 0.0)
        route = route + jnp.where(lane == 2 * TOP_K + k, rank, 0.0)
    route_ref[0] = route
    cnt_ref[...] = cnt_ref[...] + jnp.sum(onehot, axis=0, keepdims=True)


def _merge(x3, py, ay, mg, gt1, sh2, sc2, mod_map, weights, cnt0, tm):
    nb, s, dm = x3.shape
    row = lambda b, i: (b, i, 0)
    mod_block = (1,) + gt1.shape[1:]
    tri = jnp.asarray(np.tril(np.ones((tm, tm), np.float32), -1), BF16)
    consts = list(weights) + [tri, cnt0]
    return pl.pallas_call(
        functools.partial(_merge_kernel, dm=dm),
        grid=(nb, s // tm),
        in_specs=[pl.BlockSpec((1, tm, dm), row),
                  pl.BlockSpec((1, tm, py.shape[-1]), row),
                  pl.BlockSpec((1, tm, ay.shape[-1]), row),
                  pl.BlockSpec((1, tm, 2 * dm), row),
                  pl.BlockSpec(mod_block, mod_map), pl.BlockSpec(mod_block, mod_map), pl.BlockSpec(mod_block, mod_map)]
                 + [_full_spec(w) for w in consts],
        out_specs=[pl.BlockSpec((1, tm, dm), row), pl.BlockSpec((1, tm, dm), row),
                   pl.BlockSpec((1, tm, LANES), row), pl.BlockSpec((1, LANES), lambda b, i: (0, 0))],
        out_shape=[jax.ShapeDtypeStruct((nb, s, dm), F32), jax.ShapeDtypeStruct((nb, s, dm), F32),
                   jax.ShapeDtypeStruct((nb, s, LANES), F32), jax.ShapeDtypeStruct((1, LANES), F32)],
        compiler_params=_cparams("arbitrary", "arbitrary"),
        name="merge",
    )(x3, py, ay, mg, gt1, sh2, sc2, *consts)


def _scatter_rows(dest_ref, h_ref, xs_out, sem, tm):
    def row_copy(r, d):
        return pltpu.make_async_copy(h_ref.at[pl.ds(r, 1), :], xs_out.at[pl.ds(d, 1), :], sem)

    def start(r, c):
        for k in range(TOP_K):
            row_copy(r, dest_ref[r * TOP_K + k]).start()
        return c

    def wait(r, c):
        for k in range(TOP_K):
            row_copy(0, 0).wait()
        return c

    lax.fori_loop(0, tm, start, 0)
    lax.fori_loop(0, tm, wait, 0)


def _dispatch_kernel(clr_ref, nl_ref, dest_ref, ha_ref, hb_ref, xs_out, sem, zero_ref, zsem, *, td, nta, n_exp,
                     n_blocks):
    i = pl.program_id(0)

    @pl.when(i == 0)
    def _():
        zero_ref[...] = jnp.zeros(zero_ref.shape, F32)

        def zcopy(row0):
            return pltpu.make_async_copy(zero_ref, xs_out.at[pl.ds(pl.multiple_of(row0, MOE_ROWS), MOE_ROWS), :], zsem)

        def tail_start(j, c):
            zcopy(j * MOE_ROWS).start()
            return c

        def tail_wait(j, c):
            zcopy(0).wait()
            return c

        for e in range(n_exp):
            zcopy(clr_ref[e]).start()
        lax.fori_loop(nl_ref[0], n_blocks, tail_start, 0)
        for e in range(n_exp):
            zcopy(0).wait()
        lax.fori_loop(nl_ref[0], n_blocks, tail_wait, 0)

    @pl.when(i < nta)
    def _():
        _scatter_rows(dest_ref, ha_ref, xs_out, sem, td)

    @pl.when(i >= nta)
    def _():
        _scatter_rows(dest_ref, hb_ref, xs_out, sem, td)


def _dispatch(clear_rows, n_live, dest, h_a, h_b, n_blocks, td):
    ta, dm = h_a.shape
    tb = h_b.shape[0]
    n_exp = clear_rows.shape[0]
    nta = ta // td
    return pl.pallas_call(
        functools.partial(_dispatch_kernel, td=td, nta=nta, n_exp=n_exp, n_blocks=n_blocks),
        grid=(nta + tb // td,),
        in_specs=[pl.BlockSpec(memory_space=pltpu.SMEM),
                  pl.BlockSpec(memory_space=pltpu.SMEM),
                  pl.BlockSpec((td * TOP_K,), lambda i: (i,), memory_space=pltpu.SMEM),
                  pl.BlockSpec((td, dm), lambda i: (jnp.minimum(i, nta - 1), 0)),
                  pl.BlockSpec((td, dm), lambda i: (jnp.maximum(i - nta, 0), 0))],
        out_specs=pl.BlockSpec(memory_space=pl.ANY),
        out_shape=jax.ShapeDtypeStruct((n_blocks * MOE_ROWS, dm), F32),
        scratch_shapes=[pltpu.SemaphoreType.DMA(()), pltpu.VMEM((MOE_ROWS, dm), F32), pltpu.SemaphoreType.DMA(())],
        compiler_params=_cparams("arbitrary"),
        name="dispatch",
    )(clear_rows, n_live, dest, h_a, h_b)


def _expert_kernel(be_ref, nv_ref, x_ref, wgu_ref, bgu_ref, wd_ref, bd_ref, y_ref, wgu_bf, wd_bf, *, dff):
    i = pl.program_id(0)
    live = i < nv_ref[0]
    fresh = (i == 0) | (be_ref[i] != be_ref[jnp.maximum(i - 1, 0)])

    @pl.when(live & fresh)
    def _():
        wgu_bf[...] = wgu_ref[0].astype(BF16)
        wd_bf[...] = wd_ref[0].astype(BF16)

    @pl.when(live)
    def _():
        gu = _dot(x_ref[...].astype(BF16), wgu_bf[...]) + bgu_ref[0]
        gate = jnp.minimum(gu[:, :dff], SWIGLU_LIMIT)
        up = jnp.clip(gu[:, dff:], -SWIGLU_LIMIT, SWIGLU_LIMIT)
        act = (up + 1.0) * gate * jax.nn.sigmoid(SWIGLU_ALPHA * gate)
        y_ref[...] = _dot(act.astype(BF16), wd_bf[...]) + bd_ref[0]

    @pl.when(jnp.logical_not(live))
    def _():
        y_ref[...] = jnp.zeros(y_ref.shape, F32)


def _experts(blk_e, n_live, xs, w_gu, b_gu, w_down, b_down):
    rows, dm = xs.shape
    n_exp, _, dff2 = w_gu.shape
    dff = dff2 // 2
    nblk = rows // MOE_ROWS
    xmap = lambda i, be, nv: (jnp.minimum(i, nv[0] - 1), 0)
    emap = lambda i, be, nv: (be[i], 0, 0)
    return pl.pallas_call(
        functools.partial(_expert_kernel, dff=dff),
        grid_spec=pltpu.PrefetchScalarGridSpec(
            num_scalar_prefetch=2,
            grid=(nblk,),
            in_specs=[pl.BlockSpec((MOE_ROWS, dm), xmap),
                      pl.BlockSpec((1, dm, dff2), emap),
                      pl.BlockSpec((1, 1, dff2), emap),
                      pl.BlockSpec((1, dff, dm), emap),
                      pl.BlockSpec((1, 1, dm), emap)],
            out_specs=pl.BlockSpec((MOE_ROWS, dm), lambda i, be, nv: (i, 0)),
            scratch_shapes=[pltpu.VMEM((dm, dff2), BF16), pltpu.VMEM((dff, dm), BF16)]),
        out_shape=jax.ShapeDtypeStruct((rows, dm), F32),
        compiler_params=_cparams("arbitrary"),
        name="experts",
    )(blk_e, n_live, xs, w_gu, b_gu.reshape(n_exp, 1, dff2), w_down, b_down.reshape(n_exp, 1, dm))


def _final_kernel(dest_ref, dnext_ref, route_ref, x1_ref, gt2_ref, gpost2_ref, ys_hbm, out_ref, rows_buf, sem, *, tm):
    nt = pl.num_programs(1)
    step = pl.program_id(0) * nt + pl.program_id(1)
    slot = step % 2

    def row_copy(sl, r, k, d):
        return pltpu.make_async_copy(ys_hbm.at[pl.ds(d, 1), :], rows_buf.at[sl, k, pl.ds(r, 1), :], sem.at[sl])

    def fetch(idx_ref, sl):
        def start(r, c):
            for k in range(TOP_K):
                row_copy(sl, r, k, idx_ref[r * TOP_K + k]).start()
            return c
        lax.fori_loop(0, tm, start, 0)

    @pl.when(step == 0)
    def _():
        fetch(dest_ref, 0)

    @pl.when(step + 1 < pl.num_programs(0) * nt)
    def _():
        fetch(dnext_ref, 1 - slot)

    def wait(r, c):
        for k in range(TOP_K):
            row_copy(slot, 0, k, 0).wait()
        return c

    lax.fori_loop(0, tm, wait, 0)
    route = route_ref[0]
    lane = lax.broadcasted_iota(jnp.int32, route.shape, 1)
    f = None
    for k in range(TOP_K):
        gate = jnp.sum(jnp.where(lane == TOP_K + k, route, 0.0), axis=1, keepdims=True)
        term = gate * rows_buf[slot, k]
        f = term if f is None else f + term
    out_ref[0] = x1_ref[0] + _mod_rows(gt2_ref, tm) * _rms(f, gpost2_ref[...])


def _final(dest, route, x1, gt2, mod_map, g_post2, ys, tm):
    nb, s, dm = x1.shape
    nt = s // tm
    last = nb * nt - 1
    row = lambda b, i: (b, i, 0)
    mod_block = (1,) + gt2.shape[1:]
    return pl.pallas_call(
        functools.partial(_final_kernel, tm=tm),
        grid=(nb, nt),
        in_specs=[pl.BlockSpec((tm * TOP_K,), lambda b, i: (b * nt + i,), memory_space=pltpu.SMEM),
                  pl.BlockSpec((tm * TOP_K,), lambda b, i: (jnp.minimum(b * nt + i + 1, last),),
                               memory_space=pltpu.SMEM),
                  pl.BlockSpec((1, tm, LANES), row),
                  pl.BlockSpec((1, tm, dm), row),
                  pl.BlockSpec(mod_block, mod_map),
                  _full_spec(g_post2),
                  pl.BlockSpec(memory_space=pl.ANY)],
        out_specs=pl.BlockSpec((1, tm, dm), row),
        out_shape=jax.ShapeDtypeStruct((nb, s, dm), F32),
        scratch_shapes=[pltpu.VMEM((2, TOP_K, tm, dm), F32), pltpu.SemaphoreType.DMA((2,))],
        compiler_params=_cparams("arbitrary", "arbitrary"),
        name="final",
    )(dest, dest, route, x1, gt2, g_post2, ys)


def _transpose_rows(kv):
    lead = kv.shape[:-4]
    n = len(lead)
    perm = tuple(range(n)) + (n + 1, n + 2, n + 3, n)
    return kv.transpose(perm).reshape(lead + (KV_ROWS, kv.shape[-4]))


def _untranspose_rows(kvt):
    lead = kvt.shape[:-2]
    n = len(lead)
    x = kvt.reshape(lead + (2, N_KV_HEADS, HEAD_DIM, kvt.shape[-1]))
    return x.transpose(tuple(range(n)) + (n + 3, n, n + 1, n + 2))


def _layer(xp, xs, cache_cmp, cache_sel, st_win, st_pool, page_table, cp, cs, w):
    nb, s, dm = xp.shape
    bd, qn, _ = xs.shape
    ts_ = bd * qn
    n_pages = page_table.shape[1]
    assert cache_cmp.shape[1] == LANES, "page size must equal the lane count"
    past = n_pages * LANES
    tm_p = 256
    tm_s = min(256, ts_)
    assert s % tm_p == 0 and ts_ % tm_s == 0 and qn % SUBLANES == 0 and qn <= LANES
    n_exp = w['router_w'].shape[1]
    assert n_exp <= LANES

    c_all = jnp.concatenate([cp, cs], axis=0)
    c_all = jnp.pad(c_all, ((0, -c_all.shape[0] % SUBLANES), (0, 0)))
    mods = _ada(c_all, w['w_ada'], w['b_ada'][None])
    mp = mods[:nb].reshape(nb, 6, 1, dm)
    mod_p = [mp[:, j] for j in range(6)]
    ms = mods[nb:nb + bd].reshape(ts_ // tm_s, tm_s // qn, 6, dm)
    mod_s = [ms[:, :, j] for j in range(6)]
    map_p = lambda b, i: (b, 0, 0)
    map_s = lambda b, i: (i, 0, 0)

    wa, wkvt, pw, aw, order = _prep_in_weights(w['w_in'])
    g_pre1 = w['g_pre1'][None]
    u_p, q_p, mg_p, ng_p, kvt_p = _proj(xp, mod_p[1], mod_p[0], map_p, g_pre1, wa, wkvt, tm_p, pw, aw)
    xs3 = xs.reshape(1, ts_, dm)
    u_s, q_s, mg_s, ng_s, kvt_s = _proj(xs3, mod_s[1], mod_s[0], map_s, g_pre1, wa, wkvt, tm_s, pw, aw)

    cw = _prep_cmp_weights(w['cmp_pe'], w['cmp_w1'], w['cmp_b1'], w['cmp_w2'])
    page_tab = page_table.reshape(-1).astype(jnp.int32)
    ckvt_p = _compress_prompt(kvt_p, cw)
    ckvt_s = _compress_paged(page_tab, n_pages, _transpose_rows(cache_cmp), cw)
    attn_p = _nsa_prompt(q_p, ng_p, ckvt_p, kvt_p, LANES)
    new_t = kvt_s[0].reshape(3 * KV_ROWS, bd, qn).transpose(1, 0, 2)
    new_pad = jnp.pad(new_t, ((0, 0), (0, 0), (0, LANES - qn)))
    attn_s = _nsa_sample(page_tab, n_pages, q_s.reshape(bd, qn, aw), ng_s.reshape(bd, qn, LANES), ckvt_s,
                         _transpose_rows(cache_sel), new_pad[:, KV_ROWS:2 * KV_ROWS], _transpose_rows(st_win),
                         new_pad[:, 2 * KV_ROWS:])

    pool_w, pool_scale = w['pool_w'], w['pool_scale'][None]
    hpt = tm_p // POOL_HIST
    py_p = _pool(u_p, lambda b, i: (b, jnp.maximum(i * hpt - 1, 0), 0), u_p, pool_w, pool_scale, tm_p, 0, True)
    p_buf = st_pool.shape[1]
    assert p_buf == max(POOL_WINDOWS) - 1
    u_sb = u_s.reshape(bd, qn, pw)
    hist_s = jnp.pad(st_pool, ((0, 0), (POOL_HIST - p_buf, 0), (0, 0)))
    py_s = _pool(hist_s, lambda b, i: (b, 0, 0), u_sb, pool_w, pool_scale, qn, past, False)

    rw = jnp.pad(w['router_w'], ((0, 0), (0, LANES - n_exp)))
    rwh = rw.astype(BF16)
    rwl = (rw - rwh.astype(F32)).astype(BF16)
    rb = jnp.pad(w['router_b'], (0, LANES - n_exp), constant_values=NEG_BIG)[None]
    wba = w['w_br_attn'].reshape(N_HEADS, HEAD_DIM, dm)[order].reshape(aw, dm).astype(BF16)
    mweights = [w['g_post1'][None], w['g_pre2'][None], w['w_br_pool'].astype(BF16), wba, w['w_o'].astype(BF16),
                rwh, rwl, rb]
    cnt0 = jnp.zeros((1, LANES), F32)
    x1_p, h2_p, route_p, cnt_p = _merge(xp, py_p, attn_p, mg_p, mod_p[2], mod_p[3], mod_p[4], map_p, mweights,
                                        cnt0, tm_p)
    x1_s, h2_s, route_s, cnt_a = _merge(xs3, py_s.reshape(1, ts_, pw), attn_s.reshape(1, ts_, aw), mg_s, mod_s[2],
                                        mod_s[3], mod_s[4], map_s, mweights, cnt_p, tm_s)

    tp_ = nb * s
    route = jnp.concatenate([route_p.reshape(tp_, LANES), route_s.reshape(ts_, LANES)], axis=0)
    eidx = route[:, 0:TOP_K].astype(jnp.int32)
    rank = route[:, 2 * TOP_K:3 * TOP_K].astype(jnp.int32)
    counts = cnt_a[0, :n_exp].astype(jnp.int32)
    padded = (counts + MOE_ROWS - 1) // MOE_ROWS * MOE_ROWS
    pend = jnp.cumsum(padded)
    first = jnp.where(eidx[:, :, None] == jnp.arange(n_exp, dtype=jnp.int32), pend - padded, 0).sum(-1)
    dest = (first + rank).reshape(-1)
    n_blocks = (tp_ + ts_) * TOP_K // MOE_ROWS + n_exp
    blk_start = jnp.arange(n_blocks, dtype=jnp.int32) * MOE_ROWS
    blk_e = jnp.minimum(jnp.sum((pend[None, :] <= blk_start[:, None]).astype(jnp.int32), axis=1), n_exp - 1)
    n_live = (pend[-1:] // MOE_ROWS).astype(jnp.int32)

    clear_rows = jnp.maximum(pend - MOE_ROWS, 0).astype(jnp.int32)
    assert tp_ % tm_s == 0
    xs_rows = _dispatch(clear_rows, n_live, dest, h2_p.reshape(tp_, dm), h2_s.reshape(ts_, dm), n_blocks, tm_s)
    ys = _experts(blk_e, n_live, xs_rows, w['w_gu'], w['b_gu'], w['w_down'], w['b_down'])
    g_post2 = w['g_post2'][None]
    y_p = _final(dest[:tp_ * TOP_K], route_p, x1_p, mod_p[5], map_p, g_post2, ys, tm_p)
    y_s = _final(dest[tp_ * TOP_K:], route_s, x1_s, mod_s[5], map_s, g_post2, ys, tm_s)

    wlen = min(WINDOW, s)
    kv_p = [_untranspose_rows(kvt_p[:, j * KV_ROWS:(j + 1) * KV_ROWS]) for j in range(2)]
    kvw_p = _untranspose_rows(kvt_p[:, 2 * KV_ROWS:, s - wlen:])
    pool_p = u_p[:, s - min(POOL_HIST - 1, s):]
    kv_s = [_untranspose_rows(new_t[:, j * KV_ROWS:(j + 1) * KV_ROWS]) for j in range(3)]
    w_buf = st_win.shape[1]
    kvw_s = jnp.concatenate([st_win, kv_s[2]], axis=1)[:, -w_buf:]
    pool_s = jnp.concatenate([st_pool, u_sb], axis=1)[:, -p_buf:]
    return y_p, y_s.reshape(bd, qn, dm), (kv_p[0], kv_p[1], kvw_p, pool_p, kv_s[0], kv_s[1], kvw_s, pool_s)


def kernel(x_prompt, x_sample, cache_kv_cmp, cache_kv_sel, state_kv_win, state_pool, page_table, c_prompt, c_sample,
           w_ada, b_ada, g_pre1, g_post1, g_pre2, g_post2, w_in, pool_w, pool_scale, cmp_pe, cmp_w1, cmp_b1, cmp_w2,
           w_br_pool, w_br_attn, w_o, router_w, router_b, w_gu, b_gu, w_down, b_down):
    names = ('w_ada', 'b_ada', 'g_pre1', 'g_post1', 'g_pre2', 'g_post2', 'w_in', 'pool_w', 'pool_scale', 'cmp_pe',
             'cmp_w1', 'cmp_b1', 'cmp_w2', 'w_br_pool', 'w_br_attn', 'w_o', 'router_w', 'router_b', 'w_gu', 'b_gu',
             'w_down', 'b_down')
    stacked = (w_ada, b_ada, g_pre1, g_post1, g_pre2, g_post2, w_in, pool_w, pool_scale, cmp_pe, cmp_w1, cmp_b1,
               cmp_w2, w_br_pool, w_br_attn, w_o, router_w, router_b, w_gu, b_gu, w_down, b_down)
    xp, xs = x_prompt, x_sample
    states = []
    for l in range(w_ada.shape[0]):
        w = {n: a[l] for n, a in zip(names, stacked)}
        xp, xs, st = _layer(xp, xs, cache_kv_cmp[l], cache_kv_sel[l], state_kv_win[l], state_pool[l], page_table,
                            c_prompt, c_sample, w)
        states.append(st)
    return (xp, xs) + tuple(jnp.stack([st[j] for st in states]) for j in range(8))
```

```python
import functools

import jax
import jax.numpy as jnp
import numpy as np
from jax import lax
from jax.experimental import pallas as pl
from jax.experimental.pallas import tpu as pltpu

F32 = jnp.float32
BF16 = jnp.bfloat16

POOL_WINDOWS = (2, 4, 8, 16)
N_HEADS = 8
N_KV_HEADS = 2
HEAD_DIM = 64
GQ = N_HEADS // N_KV_HEADS
CMP_BLOCK = 32
CMP_STRIDE = 16
SEL_BLOCK = 64
SUB_PER_SEL = SEL_BLOCK // CMP_STRIDE
N_SEL = 16
N_LOCAL = 2
WINDOW = 512
TOP_K = 4
SWIGLU_LIMIT = 7.0
SWIGLU_ALPHA = 1.702
RMS_EPS = 1e-6

LANES = 128
SUBLANES = 8
VMEM_LIMIT = 56 * 1024 * 1024
NEG_BIG = -1e30
M_INIT = -1e29
MXU_DEPTH = 256
FAR = 1 << 30
KV_ROWS = 2 * N_KV_HEADS * HEAD_DIM
HALF = N_KV_HEADS * HEAD_DIM
POOL_HIST = 16
MOE_ROWS = 512
PAGE_UNROLL = 32
SAMPLE_TILE_PAGES = 64


def _cparams(*sem):
    return pltpu.CompilerParams(dimension_semantics=sem, vmem_limit_bytes=VMEM_LIMIT)


def _full_spec(w):
    nd = w.ndim
    return pl.BlockSpec(w.shape, lambda *a: (0,) * nd)


def _dot(a, b):
    return jnp.dot(a, b, preferred_element_type=F32)


def _dot_nt(a, b):
    return lax.dot_general(a, b, (((1,), (1,)), ((), ())), preferred_element_type=F32)


def _split3(x):
    hi = x.astype(BF16)
    r1 = x - hi.astype(F32)
    mid = r1.astype(BF16)
    lo = (r1 - mid.astype(F32)).astype(BF16)
    return hi, mid, lo


def _dot_exact_rhs(x, m_bf16):
    hi, mid, lo = _split3(x)
    return _dot(hi, m_bf16) + _dot(mid, m_bf16) + _dot(lo, m_bf16)


def _mod_rows(ref, tm):
    v = ref[0]
    r, dm = v.shape
    if r in (1, tm):
        return v
    return jnp.broadcast_to(v[:, None, :], (r, tm // r, dm)).reshape(tm, dm)


def _rms(x, g):
    return x * lax.rsqrt(jnp.mean(x * x, axis=-1, keepdims=True) + RMS_EPS) * g


def _ada_kernel(c_ref, w_ref, b_ref, o_ref):
    o_ref[...] = _dot(c_ref[...].astype(BF16), w_ref[...].astype(BF16)) + b_ref[...]


def _ada(c_all, w_ada, b_ada):
    m, d = c_all.shape
    n = w_ada.shape[1]
    tn = 1024
    return pl.pallas_call(
        _ada_kernel,
        grid=(n // tn,),
        in_specs=[pl.BlockSpec((m, d), lambda j: (0, 0)),
                  pl.BlockSpec((d, tn), lambda j: (0, j)),
                  pl.BlockSpec((1, tn), lambda j: (0, j))],
        out_specs=pl.BlockSpec((m, tn), lambda j: (0, j)),
        out_shape=jax.ShapeDtypeStruct((m, n), F32),
        compiler_params=_cparams("arbitrary"),
        name="ada",
    )(c_all, w_ada, b_ada)


def _proj_kernel(x_ref, sc_ref, sh_ref, g_ref, wa_ref, wkv_ref,
                 u_ref, q_ref, mg_ref, ng_ref, kvc_ref, kvs_ref, kvw_ref, *, pw, aw, dm):
    tm = x_ref.shape[1]
    hb = (_rms(x_ref[0], g_ref[...]) * (1.0 + _mod_rows(sc_ref, tm)) + _mod_rows(sh_ref, tm)).astype(BF16)
    u_ref[0] = _dot(hb, wa_ref[:, 0:pw])
    q_ref[0] = _dot(hb, wa_ref[:, pw:pw + aw])
    o = pw + aw
    mg_ref[0] = jax.nn.sigmoid(_dot(hb, wa_ref[:, o:o + 2 * dm])).astype(BF16)
    ng_ref[0] = jax.nn.sigmoid(_dot(hb, wa_ref[:, o + 2 * dm:o + 2 * dm + LANES]))
    kvt = _dot_nt(wkv_ref[...], hb)
    for j, ref in enumerate((kvc_ref, kvs_ref, kvw_ref)):
        ref[0] = kvt[j * KV_ROWS:(j + 1) * KV_ROWS]


def _proj(x3, sc, sh, mod_map, g_pre1, wa, wkvt, tm, pw, aw):
    nb, s, dm = x3.shape
    nt = s // tm
    kern = functools.partial(_proj_kernel, pw=pw, aw=aw, dm=dm)
    mod_block = (1,) + sc.shape[1:]
    row = lambda b, i: (b, i, 0)
    return pl.pallas_call(
        kern,
        grid=(nb, nt),
        in_specs=[pl.BlockSpec((1, tm, dm), row),
                  pl.BlockSpec(mod_block, mod_map),
                  pl.BlockSpec(mod_block, mod_map),
                  _full_spec(g_pre1), _full_spec(wa), _full_spec(wkvt)],
        out_specs=[pl.BlockSpec((1, tm, pw), row),
                   pl.BlockSpec((1, tm, aw), row),
                   pl.BlockSpec((1, tm, 2 * dm), row),
                   pl.BlockSpec((1, tm, LANES), row),
                   ] + [pl.BlockSpec((1, KV_ROWS, tm), lambda b, i: (b, 0, i))] * 3,
        out_shape=[jax.ShapeDtypeStruct((nb, s, pw), F32),
                   jax.ShapeDtypeStruct((nb, s, aw), F32),
                   jax.ShapeDtypeStruct((nb, s, 2 * dm), BF16),
                   jax.ShapeDtypeStruct((nb, s, LANES), F32),
                   ] + [jax.ShapeDtypeStruct((nb, KV_ROWS, s), F32)] * 3,
        compiler_params=_cparams("arbitrary", "arbitrary"),
        name="proj",
    )(x3, sc, sh, g_pre1, wa, wkvt)


def _prep_in_weights(w_in):
    dm = w_in.shape[0]
    pw = dm // 2
    aw = N_HEADS * HEAD_DIM
    off_kvc = pw + aw
    off_ng = off_kvc + 3 * KV_ROWS
    off_mg = off_ng + 3 * N_HEADS
    order = np.array([k * GQ + g for g in range(GQ) for k in range(N_KV_HEADS)])
    wq = w_in[:, pw:off_kvc].reshape(dm, N_HEADS, HEAD_DIM)[:, order].reshape(dm, aw) * (HEAD_DIM ** -0.5)
    wng = jnp.pad(w_in[:, off_ng:off_mg], ((0, 0), (0, LANES - 3 * N_HEADS)))
    wa = jnp.concatenate([w_in[:, :pw], wq, w_in[:, off_mg:], wng], axis=1).astype(BF16)
    wkvt = w_in[:, off_kvc:off_ng].T.astype(BF16)
    return wa, wkvt, pw, aw, order


def _compress_core(get_page, n_pages, perm_ref, wp_ref, w2t_ref, pe_ref, w1f_ref, b1_ref, s2_ref, out_ref):
    sub_pp = LANES // CMP_STRIDE
    n_sub = n_pages * sub_pp
    r_cmp = CMP_BLOCK // CMP_STRIDE
    hid = w1f_ref.shape[-1]

    def page_body(p, carry):
        pg = get_page(p).astype(BF16)
        t = _dot_nt(perm_ref[...], pg)
        row0 = pl.multiple_of(p * sub_pp, sub_pp)
        for s in range(2):
            for j in range(CMP_STRIDE):
                s2_ref[s, j // 2, pl.ds(row0, sub_pp), (j % 2) * HALF:(j % 2 + 1) * HALF] = (
                    t[sub_pp * j:sub_pp * (j + 1), s * HALF:(s + 1) * HALF])
        return carry

    lax.fori_loop(0, n_pages, page_body, 0, unroll=PAGE_UNROLL)
    for s in range(2):
        acc = None
        for jp in range(CMP_STRIDE // 2):
            d = _dot(s2_ref[s, jp].astype(BF16), wp_ref[s, jp])
            acc = d if acc is None else acc + d
        bias = _dot(pe_ref[s], w1f_ref[s])[0:1] + b1_ref[s]
        bias = jnp.concatenate([bias] * N_KV_HEADS, axis=1)
        w = N_KV_HEADS * hid
        hpre = acc[:, :w] + bias
        for r in range(1, r_cmp):
            hpre = hpre + pltpu.roll(acc[:, r * w:(r + 1) * w], n_sub - r, axis=0)
        g = jax.nn.gelu(hpre).astype(BF16)
        out_ref[s * HALF:(s + 1) * HALF, :] = _dot_nt(w2t_ref[s], g)


def _compress_prompt_kernel(kv_ref, perm_ref, wp_ref, w2t_ref, pe_ref, w1f_ref, b1_ref, out_ref, s2_ref, *, n_pages):
    def get_page(p):
        return kv_ref[0, :, pl.ds(pl.multiple_of(p * LANES, LANES), LANES)]
    _compress_core(get_page, n_pages, perm_ref, wp_ref, w2t_ref, pe_ref, w1f_ref, b1_ref, s2_ref, out_ref.at[0])


def _page_fetch(pt_ref, cache_hbm, buf, sem, b, slot, n_pages):
    def start(p, c):
        pltpu.make_async_copy(cache_hbm.at[pt_ref[b * n_pages + p]], buf.at[slot, p], sem.at[slot]).start()
        return c
    lax.fori_loop(0, n_pages, start, 0)


def _page_wait(cache_hbm, buf, sem, slot, n_pages):
    def wait(p, c):
        pltpu.make_async_copy(cache_hbm.at[0], buf.at[slot, p], sem.at[slot]).wait()
        return c
    lax.fori_loop(0, n_pages, wait, 0)


def _paged_prologue(pt_ref, cache_hbm, buf, sem, n_pages):
    b = pl.program_id(0)
    slot = b % 2

    @pl.when(b == 0)
    def _():
        _page_fetch(pt_ref, cache_hbm, buf, sem, 0, 0, n_pages)

    @pl.when(b + 1 < pl.num_programs(0))
    def _():
        _page_fetch(pt_ref, cache_hbm, buf, sem, b + 1, 1 - slot, n_pages)

    _page_wait(cache_hbm, buf, sem, slot, n_pages)
    return slot


def _compress_paged_kernel(pt_ref, cache_hbm, perm_ref, wp_ref, w2t_ref, pe_ref, w1f_ref, b1_ref, out_ref,
                           buf, sem, s2_ref, *, n_pages):
    slot = _paged_prologue(pt_ref, cache_hbm, buf, sem, n_pages)
    _compress_core(lambda p: buf[slot, p], n_pages, perm_ref, wp_ref, w2t_ref, pe_ref, w1f_ref, b1_ref,
                   s2_ref, out_ref.at[0])


def _prep_cmp_weights(cmp_pe, cmp_w1, cmp_b1, cmp_w2):
    hid = cmp_w1.shape[-1]
    r_cmp = CMP_BLOCK // CMP_STRIDE
    eye = jnp.eye(N_KV_HEADS, dtype=F32)
    w1r = cmp_w1.reshape(2, r_cmp, CMP_STRIDE // 2, 2, HEAD_DIM, hid)
    wp = jnp.einsum('srpjdh,kc->spjkdrch', w1r, eye).reshape(
        2, CMP_STRIDE // 2, 2 * HALF, r_cmp * N_KV_HEADS * hid).astype(BF16)
    w2t = jnp.einsum('shd,kc->skdch', cmp_w2, eye).reshape(2, HALF, N_KV_HEADS * hid).astype(BF16)
    pe = jnp.broadcast_to(cmp_pe.reshape(2, 1, CMP_BLOCK * HEAD_DIM),
                          (2, SUBLANES, CMP_BLOCK * HEAD_DIM)).astype(BF16)
    w1f = cmp_w1.reshape(2, CMP_BLOCK * HEAD_DIM, hid).astype(BF16)
    b1 = cmp_b1.reshape(2, 1, hid)
    sub_pp = LANES // CMP_STRIDE
    x = np.arange(LANES)
    perm = np.zeros((LANES, LANES), np.float32)
    perm[x, (x % sub_pp) * CMP_STRIDE + x // sub_pp] = 1.0
    return (jnp.asarray(perm, BF16), wp, w2t, pe, w1f, b1)


def _compress_prompt(kvt, cw):
    nb, _, s = kvt.shape
    n_pages = s // LANES
    n_sub = s // CMP_STRIDE
    return pl.pallas_call(
        functools.partial(_compress_prompt_kernel, n_pages=n_pages),
        grid=(nb,),
        in_specs=[pl.BlockSpec((1, KV_ROWS, s), lambda b: (b, 0, 0))] + [_full_spec(w) for w in cw],
        out_specs=pl.BlockSpec((1, KV_ROWS, n_sub), lambda b: (b, 0, 0)),
        out_shape=jax.ShapeDtypeStruct((nb, KV_ROWS, n_sub), F32),
        scratch_shapes=[pltpu.VMEM((2, CMP_STRIDE // 2, n_sub, 2 * HALF), F32)],
        compiler_params=_cparams("arbitrary"),
        name="compress_prompt",
    )(kvt, *cw)


def _compress_paged(page_tab, n_pages, cache_t, cw):
    nb = page_tab.shape[0] // n_pages
    n_sub = n_pages * (LANES // CMP_STRIDE)
    return pl.pallas_call(
        functools.partial(_compress_paged_kernel, n_pages=n_pages),
        grid_spec=pltpu.PrefetchScalarGridSpec(
            num_scalar_prefetch=1,
            grid=(nb,),
            in_specs=[pl.BlockSpec(memory_space=pl.ANY)] + [_full_spec(w) for w in cw],
            out_specs=pl.BlockSpec((1, KV_ROWS, n_sub), lambda b, pt: (b, 0, 0)),
            scratch_shapes=[pltpu.VMEM((2, n_pages, KV_ROWS, LANES), F32),
                            pltpu.SemaphoreType.DMA((2,)),
                            pltpu.VMEM((2, CMP_STRIDE // 2, n_sub, 2 * HALF), F32)]),
        out_shape=jax.ShapeDtypeStruct((nb, KV_ROWS, n_sub), F32),
        compiler_params=_cparams("arbitrary"),
        name="compress_paged",
    )(page_tab, cache_t, *cw)


def _stack_heads(fn):
    return jnp.concatenate([fn(k, g) for k in range(N_KV_HEADS) for g in range(GQ)], axis=0)


def _alibi_slope(k, g):
    return 2.0 ** (-8.0 * (k * GQ + g + 1) / N_HEADS)


def _nsa_block(q, sgate, t0, ckvt_ref, mmat_ref, bmat_ref, sel_tile, n_sel_tiles, sel_tk, sel_last, win_tile,
               m_ref, acc_ref, out_ref, *, qt, nblk_pad, n_pick, active=None):
    t_pos = t0 + lax.broadcasted_iota(jnp.int32, (qt, 1), 0)
    lane_half = lax.broadcasted_iota(jnp.int32, (qt, LANES), 1) // HEAD_DIM

    qp = _stack_heads(lambda k, g: jnp.where(lane_half == k, q[:, g * LANES:(g + 1) * LANES], 0.0)).astype(BF16)

    def alibi(distf):
        return _stack_heads(lambda k, g: _alibi_slope(k, g) * distf)

    def stack_masks(mk):
        return jnp.concatenate([mk[k] for k in range(N_KV_HEADS) for _ in range(GQ)], axis=0) > 0.5

    nc = ckvt_ref.shape[-1]
    c_end = lax.broadcasted_iota(jnp.int32, (1, nc), 1) * CMP_STRIDE + (CMP_BLOCK - 1)
    dist_c = t_pos - c_end
    valid_c = jnp.where(dist_c >= 0, 1.0, 0.0)
    mask_c = stack_masks([valid_c] * N_KV_HEADS)
    s = _dot(qp, ckvt_ref[0:HALF, :].astype(BF16)) - alibi(dist_c.astype(F32))
    s = jnp.where(mask_c, s, NEG_BIG)
    p = jnp.where(mask_c, jnp.exp(s - jnp.max(s, axis=1, keepdims=True)), 0.0)
    p = p / jnp.maximum(jnp.sum(p, axis=1, keepdims=True), 1e-30)
    o_c = _dot_nt(p.astype(BF16), ckvt_ref[HALF:2 * HALF, :].astype(BF16))

    blk = lax.broadcasted_iota(jnp.int32, (qt, nblk_pad), 1)
    cur = t_pos // SEL_BLOCK
    forced = (blk == 0) | ((blk > cur - N_LOCAL) & (blk <= cur))
    free = (blk >= 1) & (blk <= cur - N_LOCAL)
    blkf = blk.astype(F32)
    sel = []
    for k in range(N_KV_HEADS):
        imp = p[k * GQ * qt:(k * GQ + 1) * qt]
        for g in range(1, GQ):
            imp = imp + p[(k * GQ + g) * qt:(k * GQ + g + 1) * qt]
        score = jnp.where(free, _dot_exact_rhs(imp, mmat_ref[...]), -jnp.inf)
        chosen = jnp.where(forced, 1.0, 0.0)
        for _ in range(n_pick):
            best = jnp.max(score, axis=1, keepdims=True)
            idx = jnp.min(jnp.where(score == best, blkf, float(nblk_pad)), axis=1, keepdims=True)
            hit = blkf == idx
            chosen = jnp.where(hit, 1.0, chosen)
            score = jnp.where(hit, -jnp.inf, score)
        sel.append(chosen)

    rk = GQ * qt
    lane = lax.broadcasted_iota(jnp.int32, (1, LANES), 1)
    t_rows = jnp.concatenate([t_pos] * GQ, axis=0)

    def q_feat(k):
        parts = []
        for g in range(GQ):
            sl = _alibi_slope(k, g)
            base = (1 - k) * HEAD_DIM
            ext = jnp.where(lane == base, SEL_BLOCK * sl, jnp.where(lane == base + 1, sl, 0.0))
            parts.append(jnp.where(lane_half == k, q[:, g * LANES:(g + 1) * LANES], ext))
        return jnp.concatenate(parts, axis=0).astype(BF16)

    qf = [q_feat(k) for k in range(N_KV_HEADS)]

    def k_aug(kt, pos, use_sel):
        tk = kt.shape[-1]
        hi = lax.shift_right_arithmetic(pos, SEL_BLOCK.bit_length() - 1)
        lo = pos & (SEL_BLOCK - 1)
        r = lax.broadcasted_iota(jnp.int32, (HEAD_DIM, tk), 0)
        ext = jnp.where(r == 0, hi.astype(F32), jnp.where(r == 1, lo.astype(F32), 0.0)).astype(BF16)
        if use_sel:
            onehot = jnp.where(lax.broadcasted_iota(jnp.int32, (nblk_pad, 1), 0) == hi, 1.0, 0.0).astype(BF16)
        else:
            onehot = jnp.zeros((nblk_pad, tk), BF16)
        kb = kt.astype(BF16)
        return [jnp.concatenate([onehot, kb[0:HEAD_DIM], ext], axis=0),
                jnp.concatenate([onehot, ext, kb[HEAD_DIM:HALF]], axis=0)]

    m_ref[...] = jnp.full(m_ref.shape, M_INIT, F32)
    acc_ref[...] = jnp.zeros(acc_ref.shape, F32)

    def flash(br, qa, kt, vt, pos, use_sel, cond):
        ka = k_aug(kt, pos, use_sel)
        va = jnp.concatenate([vt.astype(BF16), jnp.ones((2 * SUBLANES, kt.shape[-1]), BF16)], axis=0)
        for k in range(N_KV_HEADS):
            sc = _dot(qa[k], ka[k])
            if cond is not None:
                sc = jnp.where(cond, sc, NEG_BIG)
            m_old = m_ref[br, k]
            m_new = jnp.maximum(m_old, jnp.max(sc, axis=1, keepdims=True))
            pr = jnp.exp(sc - m_new).astype(BF16)
            acc_ref[br, k] = jnp.exp(m_old - m_new) * acc_ref[br, k] + _dot_nt(pr, va)
            m_ref[br, k] = m_new

    def result(br):
        out = []
        for k in range(N_KV_HEADS):
            a = acc_ref[br, k]
            out.append(a[:, 0:HALF] / jnp.maximum(a[:, HALF:HALF + 1], 1e-30))
        return out

    qa_w = [jnp.concatenate([jnp.zeros((rk, nblk_pad), BF16), qf[k]], axis=1) for k in range(N_KV_HEADS)]
    kt, vt, pos = win_tile
    dist = t_rows - pos
    flash(1, qa_w, kt, vt, pos, False, (dist >= 0) & (dist <= WINDOW))

    qa_s = []
    for k in range(N_KV_HEADS):
        selneg = jnp.where(sel[k] > 0.5, 0.0, NEG_BIG).astype(BF16)
        qa_s.append(jnp.concatenate([jnp.concatenate([selneg] * GQ, axis=0), qf[k]], axis=1))

    def sel_step(j):
        kt, vt = sel_tile(j)
        flash(0, qa_s, kt, vt, j * sel_tk + lax.broadcasted_iota(jnp.int32, (1, sel_tk), 1), True, None)

    if isinstance(n_sel_tiles, int):
        for j in range(n_sel_tiles):
            sel_step(j)
    else:
        act_ref, max_tiles = active
        tile_of_blk = lax.broadcasted_iota(jnp.int32, (nblk_pad, 1), 0) // (sel_tk // SEL_BLOCK)
        group = jnp.where(tile_of_blk == lax.broadcasted_iota(jnp.int32, (1, LANES), 1), 1.0, 0.0).astype(BF16)
        any_sel = jnp.max(jnp.maximum(sel[0], sel[1]), axis=0, keepdims=True)
        tile_any = _dot(jnp.broadcast_to(any_sel, (SUBLANES, nblk_pad)).astype(BF16), group)
        cnt = jnp.int32(0)
        for j in range(max_tiles):
            act_ref[cnt] = j
            keep = (tile_any[0, j] > 0.5) & (j < n_sel_tiles)
            cnt = cnt + keep.astype(jnp.int32)

        def pair_body(p, c):
            sel_step(act_ref[2 * p])
            sel_step(act_ref[2 * p + 1])
            return c

        lax.fori_loop(0, cnt // 2, pair_body, 0)

        @pl.when(cnt % 2 == 1)
        def _():
            sel_step(act_ref[cnt - 1])
    kt, vt, pos, use_sel = sel_last
    flash(0, qa_s, kt, vt, pos, use_sel, t_rows >= pos)
    o_s = result(0)
    o_w = result(1)

    o_c = [o_c[0:rk], o_c[rk:2 * rk]]
    g3 = _split3(sgate)
    for g in range(GQ):
        slab = None
        for c, o in enumerate((o_c, o_s, o_w)):
            bm = bmat_ref[g * 3 + c]
            gate = _dot(g3[0], bm) + _dot(g3[1], bm) + _dot(g3[2], bm)
            val = jnp.where(lane_half == 0, o[0][g * qt:(g + 1) * qt], o[1][g * qt:(g + 1) * qt])
            slab = gate * val if slab is None else slab + gate * val
        out_ref[:, g * LANES:(g + 1) * LANES] = slab


def _nsa_consts(nc, nblk_pad):
    j = np.arange(nc)[:, None]
    lo = SUB_PER_SEL * np.arange(nblk_pad)[None, :]
    mm = (np.where((j >= lo) & (j < lo + SUB_PER_SEL - 1), 2.0, 0.0)
          + np.where((j == lo - 1) | (j == lo + SUB_PER_SEL - 1), 1.0, 0.0))
    col = np.arange(LANES)[:, None]
    lane = np.arange(LANES)[None, :]
    bm = np.stack([(col == ((lane // HEAD_DIM) * GQ + g) * 3 + c)
                   for g in range(GQ) for c in range(3)]).astype(np.float32)
    return jnp.asarray(mm, BF16), jnp.asarray(bm, BF16)


def _lane_iota(n):
    return lax.broadcasted_iota(jnp.int32, (1, n), 1)


def _flash_scratch(qt, lead=()):
    rk = GQ * qt
    lead = (lead,) if isinstance(lead, int) else lead
    return [pltpu.VMEM(lead + (2, N_KV_HEADS, rk, 1), F32),
            pltpu.VMEM(lead + (2, N_KV_HEADS, rk, HALF + 2 * SUBLANES), F32)]


def _nsa_prompt_kernel(*refs, qt, tk, n_win, nblk_pad, n_pick):
    q_ref, ng_ref, ckvt_ref, kvs_ref = refs[:4]
    win_refs = refs[4:4 + n_win]
    mmat_ref, bmat_ref, out_ref, m_ref, acc_ref, act_ref = refs[4 + n_win:]
    i = pl.program_id(1)

    def sel_tile(j):
        off = pl.multiple_of(j * tk, LANES)
        return kvs_ref[0, 0:HALF, pl.ds(off, tk)], kvs_ref[0, HALF:2 * HALF, pl.ds(off, tk)]

    n_past = (i * qt) // tk
    sel_last = sel_tile(n_past) + (n_past * tk + _lane_iota(tk), True)
    pieces = [w[0] for w in win_refs]
    pos = []
    for jj in range(n_win):
        bi = i - (n_win - 1) + jj
        pos.append(jnp.where(bi >= 0, bi * qt, -FAR) + _lane_iota(qt))
    win = (jnp.concatenate([p[0:HALF] for p in pieces], axis=1),
           jnp.concatenate([p[HALF:2 * HALF] for p in pieces], axis=1), jnp.concatenate(pos, axis=1))
    _nsa_block(q_ref[0], ng_ref[0], i * qt, ckvt_ref.at[0], mmat_ref, bmat_ref, sel_tile, n_past, tk, sel_last, win,
               m_ref, acc_ref, out_ref.at[0], qt=qt, nblk_pad=nblk_pad, n_pick=n_pick,
               active=(act_ref, act_ref.shape[0]))


def _nsa_prompt(q, ng, ckvt, kvs_t, kvw_t, qt):
    nb, s, aw = q.shape
    nc = ckvt.shape[-1]
    nblk = -(-s // SEL_BLOCK)
    nblk_pad = -(-nblk // LANES) * LANES
    assert nblk_pad + HALF <= MXU_DEPTH, "selection mask + features must fit one MXU contraction pass"
    n_pick = max(min(N_SEL, nblk) - (N_LOCAL + 1), 0)
    n_win = WINDOW // qt + 1
    tk = min(4 * qt, s)
    assert s % tk == 0 and tk % qt == 0
    mmat, bmat = _nsa_consts(nc, nblk_pad)
    row = lambda b, i: (b, i, 0)
    win_specs = [pl.BlockSpec((1, KV_ROWS, qt), (lambda jj: (lambda b, i: (b, 0, jnp.maximum(i - (n_win - 1) + jj, 0))))(jj))
                 for jj in range(n_win)]
    return pl.pallas_call(
        functools.partial(_nsa_prompt_kernel, qt=qt, tk=tk, n_win=n_win, nblk_pad=nblk_pad, n_pick=n_pick),
        grid=(nb, s // qt),
        in_specs=[pl.BlockSpec((1, qt, aw), row),
                  pl.BlockSpec((1, qt, LANES), row),
                  pl.BlockSpec((1, KV_ROWS, nc), lambda b, i: (b, 0, 0)),
                  pl.BlockSpec((1, KV_ROWS, s), lambda b, i: (b, 0, 0))] + win_specs
                 + [_full_spec(mmat), _full_spec(bmat)],
        out_specs=pl.BlockSpec((1, qt, aw), row),
        out_shape=jax.ShapeDtypeStruct((nb, s, aw), F32),
        scratch_shapes=_flash_scratch(qt) + [pltpu.SMEM((s // tk,), jnp.int32)],
        compiler_params=_cparams("arbitrary", "arbitrary"),
        name="nsa_prompt",
    )(q, ng, ckvt, kvs_t, *([kvw_t] * n_win), mmat, bmat)


def _nsa_sample_kernel(pt_ref, q_ref, ng_ref, ckvt_ref, cache_hbm, ksn_ref, wst_ref, kwn_ref, mmat_ref, bmat_ref,
                       out_ref, buf, sem, m_ref, acc_ref, *, qt, nbs, n_pages, ppt, nblk_pad, n_pick):
    slot = _paged_prologue(pt_ref, cache_hbm, buf, sem, nbs * n_pages)
    past = n_pages * LANES
    wbuf = wst_ref.shape[-1]
    for bb in range(nbs):
        def sel_tile(j, bb=bb):
            p0 = bb * n_pages + j * ppt
            ks = [buf[slot, p0 + pp, 0:HALF, :] for pp in range(ppt)]
            vs = [buf[slot, p0 + pp, HALF:2 * HALF, :] for pp in range(ppt)]
            return jnp.concatenate(ks, axis=1), jnp.concatenate(vs, axis=1)

        sel_last = (ksn_ref[bb, 0:HALF, :], ksn_ref[bb, HALF:2 * HALF, :], past + _lane_iota(LANES), False)
        win = (jnp.concatenate([wst_ref[bb, 0:HALF, :], kwn_ref[bb, 0:HALF, :]], axis=1),
               jnp.concatenate([wst_ref[bb, HALF:2 * HALF, :], kwn_ref[bb, HALF:2 * HALF, :]], axis=1),
               past - wbuf + _lane_iota(wbuf + LANES))
        _nsa_block(q_ref[bb], ng_ref[bb], past, ckvt_ref.at[bb], mmat_ref, bmat_ref, sel_tile, n_pages // ppt,
                   ppt * LANES, sel_last, win, m_ref.at[bb], acc_ref.at[bb], out_ref.at[bb],
                   qt=qt, nblk_pad=nblk_pad, n_pick=n_pick)


def _nsa_sample(page_tab, n_pages, q, ng, ckvt, cache_t, ks_new, w_state, kw_new):
    nb, qt, aw = q.shape
    nc = ckvt.shape[-1]
    past = n_pages * LANES
    assert past % SEL_BLOCK == 0 and qt <= SEL_BLOCK
    nblk_pad = -(-(past // SEL_BLOCK) // LANES) * LANES
    assert nblk_pad + HALF <= MXU_DEPTH
    n_pick = max(min(N_SEL, past // SEL_BLOCK + 1) - (N_LOCAL + 1), 0)
    ppt = min(SAMPLE_TILE_PAGES, n_pages)
    nbs = 2 if nb % 2 == 0 else 1
    assert n_pages % ppt == 0
    mmat, bmat = _nsa_consts(nc, nblk_pad)
    b3 = lambda b, pt: (b, 0, 0)
    return pl.pallas_call(
        functools.partial(_nsa_sample_kernel, qt=qt, nbs=nbs, n_pages=n_pages, ppt=ppt, nblk_pad=nblk_pad,
                          n_pick=n_pick),
        grid_spec=pltpu.PrefetchScalarGridSpec(
            num_scalar_prefetch=1,
            grid=(nb // nbs,),
            in_specs=[pl.BlockSpec((nbs, qt, aw), b3),
                      pl.BlockSpec((nbs, qt, LANES), b3),
                      pl.BlockSpec((nbs, KV_ROWS, nc), b3),
                      pl.BlockSpec(memory_space=pl.ANY),
                      pl.BlockSpec((nbs, KV_ROWS, LANES), b3),
                      pl.BlockSpec((nbs, KV_ROWS, w_state.shape[-1]), b3),
                      pl.BlockSpec((nbs, KV_ROWS, LANES), b3),
                      _full_spec(mmat), _full_spec(bmat)],
            out_specs=pl.BlockSpec((nbs, qt, aw), b3),
            scratch_shapes=[pltpu.VMEM((2, nbs * n_pages, KV_ROWS, LANES), F32),
                            pltpu.SemaphoreType.DMA((2,))] + _flash_scratch(qt, nbs)),
        out_shape=jax.ShapeDtypeStruct((nb, qt, aw), F32),
        compiler_params=_cparams("arbitrary"),
        name="nsa_sample",
    )(page_tab, q, ng, ckvt, cache_t, ks_new, w_state, kw_new, mmat, bmat)


def _pool_kernel(hist_ref, cur_ref, pw_ref, ps_ref, out_ref, *, ts, pos_base, zero_first):
    i = pl.program_id(1)
    hist = hist_ref[0]
    if zero_first:
        hist = jnp.where(i == 0, 0.0, hist)
    cur = cur_ref[0]
    ext = jnp.concatenate([hist, cur], axis=0)
    pos = pos_base + i * ts + lax.broadcasted_iota(jnp.int32, (ts, 1), 0)
    ys = []
    for gi, w in enumerate(POOL_WINDOWS):
        lanes = slice(gi * LANES, (gi + 1) * LANES)
        acc = ext[:, lanes]
        step = 1
        while step < w:
            acc = acc + pltpu.roll(acc, step, axis=0)
            step *= 2
        cnt = jnp.minimum(pos + 1, w).astype(F32)
        dlt = acc[POOL_HIST:] / cnt - cur[:, lanes]
        ys.append(_dot(dlt.astype(BF16), pw_ref[gi].astype(BF16)))
    out_ref[0] = jnp.concatenate(ys, axis=1) * ps_ref[...]


def _pool(hist_arr, hist_map, u, pool_w, pool_scale, ts, pos_base, zero_first):
    nb, s, pw = u.shape
    assert pw == len(POOL_WINDOWS) * LANES and all(w & (w - 1) == 0 and w <= POOL_HIST for w in POOL_WINDOWS)
    row = lambda b, i: (b, i, 0)
    return pl.pallas_call(
        functools.partial(_pool_kernel, ts=ts, pos_base=pos_base, zero_first=zero_first),
        grid=(nb, s // ts),
        in_specs=[pl.BlockSpec((1, POOL_HIST, pw), hist_map),
                  pl.BlockSpec((1, ts, pw), row),
                  _full_spec(pool_w), _full_spec(pool_scale)],
        out_specs=pl.BlockSpec((1, ts, pw), row),
        out_shape=jax.ShapeDtypeStruct((nb, s, pw), F32),
        compiler_params=_cparams("arbitrary", "arbitrary"),
        name="pool",
    )(hist_arr, u, pool_w, pool_scale)


def _merge_kernel(x_ref, py_ref, ay_ref, mg_ref, gt1_ref, sh2_ref, sc2_ref, gpost1_ref, gpre2_ref,
                  wbp_ref, wba_ref, wo_ref, rwh_ref, rwl_ref, rb_ref, tri_ref, cnt0_ref,
                  x1_ref, h2_ref, route_ref, cnt_ref, *, dm):
    first = (pl.program_id(0) == 0) & (pl.program_id(1) == 0)

    @pl.when(first)
    def _():
        cnt_ref[...] = cnt0_ref[...]

    bp = _dot(py_ref[0].astype(BF16), wbp_ref[...])
    ba = _dot(ay_ref[0].astype(BF16), wba_ref[...])
    merged = mg_ref[0, :, 0:dm] * bp + mg_ref[0, :, dm:2 * dm] * ba
    mix = _dot(merged.astype(BF16), wo_ref[...])
    tm = x_ref.shape[1]
    x1 = x_ref[0] + _mod_rows(gt1_ref, tm) * _rms(mix, gpost1_ref[...])
    h2 = _rms(x1, gpre2_ref[...]) * (1.0 + _mod_rows(sc2_ref, tm)) + _mod_rows(sh2_ref, tm)
    x1_ref[0] = x1
    h2_ref[0] = h2

    hh = h2.astype(BF16)
    hl = (h2 - hh.astype(F32)).astype(BF16)
    logits = _dot(hh, rwh_ref[...]) + _dot(hl, rwh_ref[...]) + _dot(hh, rwl_ref[...]) + rb_ref[...]
    tm = logits.shape[0]
    lane = lax.broadcasted_iota(jnp.int32, (tm, LANES), 1)
    lanef = lane.astype(F32)
    hits, vals, idxs = [], [], []
    for _ in range(TOP_K):
        best = jnp.max(logits, axis=1, keepdims=True)
        idx = jnp.min(jnp.where(logits == best, lanef, float(LANES)), axis=1, keepdims=True)
        hit = lanef == idx
        hits.append(hit)
        vals.append(best)
        idxs.append(idx)
        logits = jnp.where(hit, -jnp.inf, logits)
    ex = [jnp.exp(v - vals[0]) for v in vals]
    den = ex[0]
    for e in ex[1:]:
        den = den + e
    onehot = jnp.where(hits[0], 1.0, 0.0)
    for h in hits[1:]:
        onehot = onehot + jnp.where(h, 1.0, 0.0)
    before = _dot(tri_ref[...], onehot.astype(BF16)) + cnt_ref[...]
    route = jnp.zeros((tm, LANES), F32)
    for k in range(TOP_K):
        rank = jnp.sum(jnp.where(hits[k], before, 0.0), axis=1, keepdims=True)
        route = route + jnp.where(lane == k, idxs[k], 0.0)
        route = route + jnp.where(lane == TOP_K + k, ex[k] / den, 0.0)
        route = route + jnp.where(lane == 2 * TOP_K + k, rank, 0.0)
    route_ref[0] = route
    cnt_ref[...] = cnt_ref[...] + jnp.sum(onehot, axis=0, keepdims=True)


def _merge(x3, py, ay, mg, gt1, sh2, sc2, mod_map, weights, cnt0, tm):
    nb, s, dm = x3.shape
    row = lambda b, i: (b, i, 0)
    mod_block = (1,) + gt1.shape[1:]
    tri = jnp.asarray(np.tril(np.ones((tm, tm), np.float32), -1), BF16)
    consts = list(weights) + [tri, cnt0]
    return pl.pallas_call(
        functools.partial(_merge_kernel, dm=dm),
        grid=(nb, s // tm),
        in_specs=[pl.BlockSpec((1, tm, dm), row),
                  pl.BlockSpec((1, tm, py.shape[-1]), row),
                  pl.BlockSpec((1, tm, ay.shape[-1]), row),
                  pl.BlockSpec((1, tm, 2 * dm), row),
                  pl.BlockSpec(mod_block, mod_map), pl.BlockSpec(mod_block, mod_map), pl.BlockSpec(mod_block, mod_map)]
                 + [_full_spec(w) for w in consts],
        out_specs=[pl.BlockSpec((1, tm, dm), row), pl.BlockSpec((1, tm, dm), row),
                   pl.BlockSpec((1, tm, LANES), row), pl.BlockSpec((1, LANES), lambda b, i: (0, 0))],
        out_shape=[jax.ShapeDtypeStruct((nb, s, dm), F32), jax.ShapeDtypeStruct((nb, s, dm), F32),
                   jax.ShapeDtypeStruct((nb, s, LANES), F32), jax.ShapeDtypeStruct((1, LANES), F32)],
        compiler_params=_cparams("arbitrary", "arbitrary"),
        name="merge",
    )(x3, py, ay, mg, gt1, sh2, sc2, *consts)


def _scatter_rows(dest_ref, h_ref, xs_out, sem, tm):
    def row_copy(r, d):
        return pltpu.make_async_copy(h_ref.at[pl.ds(r, 1), :], xs_out.at[pl.ds(d, 1), :], sem)

    def start(r, c):
        for k in range(TOP_K):
            row_copy(r, dest_ref[r * TOP_K + k]).start()
        return c

    def wait(r, c):
        for k in range(TOP_K):
            row_copy(0, 0).wait()
        return c

    lax.fori_loop(0, tm, start, 0)
    lax.fori_loop(0, tm, wait, 0)


def _dispatch_kernel(clr_ref, nl_ref, dest_ref, ha_ref, hb_ref, xs_out, sem, zero_ref, zsem, *, td, nta, n_exp,
                     n_blocks):
    i = pl.program_id(0)

    @pl.when(i == 0)
    def _():
        zero_ref[...] = jnp.zeros(zero_ref.shape, F32)

        def zcopy(row0):
            return pltpu.make_async_copy(zero_ref, xs_out.at[pl.ds(pl.multiple_of(row0, MOE_ROWS), MOE_ROWS), :], zsem)

        def tail_start(j, c):
            zcopy(j * MOE_ROWS).start()
            return c

        def tail_wait(j, c):
            zcopy(0).wait()
            return c

        for e in range(n_exp):
            zcopy(clr_ref[e]).start()
        lax.fori_loop(nl_ref[0], n_blocks, tail_start, 0)
        for e in range(n_exp):
            zcopy(0).wait()
        lax.fori_loop(nl_ref[0], n_blocks, tail_wait, 0)

    @pl.when(i < nta)
    def _():
        _scatter_rows(dest_ref, ha_ref, xs_out, sem, td)

    @pl.when(i >= nta)
    def _():
        _scatter_rows(dest_ref, hb_ref, xs_out, sem, td)


def _dispatch(clear_rows, n_live, dest, h_a, h_b, n_blocks, td):
    ta, dm = h_a.shape
    tb = h_b.shape[0]
    n_exp = clear_rows.shape[0]
    nta = ta // td
    return pl.pallas_call(
        functools.partial(_dispatch_kernel, td=td, nta=nta, n_exp=n_exp, n_blocks=n_blocks),
        grid=(nta + tb // td,),
        in_specs=[pl.BlockSpec(memory_space=pltpu.SMEM),
                  pl.BlockSpec(memory_space=pltpu.SMEM),
                  pl.BlockSpec((td * TOP_K,), lambda i: (i,), memory_space=pltpu.SMEM),
                  pl.BlockSpec((td, dm), lambda i: (jnp.minimum(i, nta - 1), 0)),
                  pl.BlockSpec((td, dm), lambda i: (jnp.maximum(i - nta, 0), 0))],
        out_specs=pl.BlockSpec(memory_space=pl.ANY),
        out_shape=jax.ShapeDtypeStruct((n_blocks * MOE_ROWS, dm), F32),
        scratch_shapes=[pltpu.SemaphoreType.DMA(()), pltpu.VMEM((MOE_ROWS, dm), F32), pltpu.SemaphoreType.DMA(())],
        compiler_params=_cparams("arbitrary"),
        name="dispatch",
    )(clear_rows, n_live, dest, h_a, h_b)


def _expert_kernel(be_ref, nv_ref, x_ref, wgu_ref, bgu_ref, wd_ref, bd_ref, y_ref, wgu_bf, wd_bf, *, dff):
    i = pl.program_id(0)
    live = i < nv_ref[0]
    fresh = (i == 0) | (be_ref[i] != be_ref[jnp.maximum(i - 1, 0)])

    @pl.when(live & fresh)
    def _():
        wgu_bf[...] = wgu_ref[0].astype(BF16)
        wd_bf[...] = wd_ref[0].astype(BF16)

    @pl.when(live)
    def _():
        gu = _dot(x_ref[...].astype(BF16), wgu_bf[...]) + bgu_ref[0]
        gate = jnp.minimum(gu[:, :dff], SWIGLU_LIMIT)
        up = jnp.clip(gu[:, dff:], -SWIGLU_LIMIT, SWIGLU_LIMIT)
        act = (up + 1.0) * gate * jax.nn.sigmoid(SWIGLU_ALPHA * gate)
        y_ref[...] = _dot(act.astype(BF16), wd_bf[...]) + bd_ref[0]

    @pl.when(jnp.logical_not(live))
    def _():
        y_ref[...] = jnp.zeros(y_ref.shape, F32)


def _experts(blk_e, n_live, xs, w_gu, b_gu, w_down, b_down):
    rows, dm = xs.shape
    n_exp, _, dff2 = w_gu.shape
    dff = dff2 // 2
    nblk = rows // MOE_ROWS
    xmap = lambda i, be, nv: (jnp.minimum(i, nv[0] - 1), 0)
    emap = lambda i, be, nv: (be[i], 0, 0)
    return pl.pallas_call(
        functools.partial(_expert_kernel, dff=dff),
        grid_spec=pltpu.PrefetchScalarGridSpec(
            num_scalar_prefetch=2,
            grid=(nblk,),
            in_specs=[pl.BlockSpec((MOE_ROWS, dm), xmap),
                      pl.BlockSpec((1, dm, dff2), emap),
                      pl.BlockSpec((1, 1, dff2), emap),
                      pl.BlockSpec((1, dff, dm), emap),
                      pl.BlockSpec((1, 1, dm), emap)],
            out_specs=pl.BlockSpec((MOE_ROWS, dm), lambda i, be, nv: (i, 0)),
            scratch_shapes=[pltpu.VMEM((dm, dff2), BF16), pltpu.VMEM((dff, dm), BF16)]),
        out_shape=jax.ShapeDtypeStruct((rows, dm), F32),
        compiler_params=_cparams("arbitrary"),
        name="experts",
    )(blk_e, n_live, xs, w_gu, b_gu.reshape(n_exp, 1, dff2), w_down, b_down.reshape(n_exp, 1, dm))


def _final_kernel(dest_ref, dnext_ref, route_ref, x1_ref, gt2_ref, gpost2_ref, ys_hbm, out_ref, rows_buf, sem, *, tm):
    nt = pl.num_programs(1)
    step = pl.program_id(0) * nt + pl.program_id(1)
    slot = step % 2

    def row_copy(sl, r, k, d):
        return pltpu.make_async_copy(ys_hbm.at[pl.ds(d, 1), :], rows_buf.at[sl, k, pl.ds(r, 1), :], sem.at[sl])

    def fetch(idx_ref, sl):
        def start(r, c):
            for k in range(TOP_K):
                row_copy(sl, r, k, idx_ref[r * TOP_K + k]).start()
            return c
        lax.fori_loop(0, tm, start, 0)

    @pl.when(step == 0)
    def _():
        fetch(dest_ref, 0)

    @pl.when(step + 1 < pl.num_programs(0) * nt)
    def _():
        fetch(dnext_ref, 1 - slot)

    def wait(r, c):
        for k in range(TOP_K):
            row_copy(slot, 0, k, 0).wait()
        return c

    lax.fori_loop(0, tm, wait, 0)
    route = route_ref[0]
    lane = lax.broadcasted_iota(jnp.int32, route.shape, 1)
    f = None
    for k in range(TOP_K):
        gate = jnp.sum(jnp.where(lane == TOP_K + k, route, 0.0), axis=1, keepdims=True)
        term = gate * rows_buf[slot, k]
        f = term if f is None else f + term
    out_ref[0] = x1_ref[0] + _mod_rows(gt2_ref, tm) * _rms(f, gpost2_ref[...])


def _final(dest, route, x1, gt2, mod_map, g_post2, ys, tm):
    nb, s, dm = x1.shape
    nt = s // tm
    last = nb * nt - 1
    row = lambda b, i: (b, i, 0)
    mod_block = (1,) + gt2.shape[1:]
    return pl.pallas_call(
        functools.partial(_final_kernel, tm=tm),
        grid=(nb, nt),
        in_specs=[pl.BlockSpec((tm * TOP_K,), lambda b, i: (b * nt + i,), memory_space=pltpu.SMEM),
                  pl.BlockSpec((tm * TOP_K,), lambda b, i: (jnp.minimum(b * nt + i + 1, last),),
                               memory_space=pltpu.SMEM),
                  pl.BlockSpec((1, tm, LANES), row),
                  pl.BlockSpec((1, tm, dm), row),
                  pl.BlockSpec(mod_block, mod_map),
                  _full_spec(g_post2),
                  pl.BlockSpec(memory_space=pl.ANY)],
        out_specs=pl.BlockSpec((1, tm, dm), row),
        out_shape=jax.ShapeDtypeStruct((nb, s, dm), F32),
        scratch_shapes=[pltpu.VMEM((2, TOP_K, tm, dm), F32), pltpu.SemaphoreType.DMA((2,))],
        compiler_params=_cparams("arbitrary", "arbitrary"),
        name="final",
    )(dest, dest, route, x1, gt2, g_post2, ys)


def _transpose_rows(kv):
    lead = kv.shape[:-4]
    n = len(lead)
    perm = tuple(range(n)) + (n + 1, n + 2, n + 3, n)
    return kv.transpose(perm).reshape(lead + (KV_ROWS, kv.shape[-4]))


def _untranspose_rows(kvt):
    lead = kvt.shape[:-2]
    n = len(lead)
    x = kvt.reshape(lead + (2, N_KV_HEADS, HEAD_DIM, kvt.shape[-1]))
    return x.transpose(tuple(range(n)) + (n + 3, n, n + 1, n + 2))


def _layer(xp, xs, cache_cmp, cache_sel, st_win, st_pool, page_table, cp, cs, w):
    nb, s, dm = xp.shape
    bd, qn, _ = xs.shape
    ts_ = bd * qn
    n_pages = page_table.shape[1]
    assert cache_cmp.shape[1] == LANES, "page size must equal the lane count"
    past = n_pages * LANES
    tm_p = 256
    tm_s = min(256, ts_)
    assert s % tm_p == 0 and ts_ % tm_s == 0 and qn % SUBLANES == 0 and qn <= LANES
    n_exp = w['router_w'].shape[1]
    assert n_exp <= LANES

    c_all = jnp.concatenate([cp, cs], axis=0)
    c_all = jnp.pad(c_all, ((0, -c_all.shape[0] % SUBLANES), (0, 0)))
    mods = _ada(c_all, w['w_ada'], w['b_ada'][None])
    mp = mods[:nb].reshape(nb, 6, 1, dm)
    mod_p = [mp[:, j] for j in range(6)]
    ms = mods[nb:nb + bd].reshape(ts_ // tm_s, tm_s // qn, 6, dm)
    mod_s = [ms[:, :, j] for j in range(6)]
    map_p = lambda b, i: (b, 0, 0)
    map_s = lambda b, i: (i, 0, 0)

    wa, wkvt, pw, aw, order = _prep_in_weights(w['w_in'])
    g_pre1 = w['g_pre1'][None]
    u_p, q_p, mg_p, ng_p, kvc_p, kvs_p, kvw_p = _proj(xp, mod_p[1], mod_p[0], map_p, g_pre1, wa, wkvt, tm_p, pw, aw)
    xs3 = xs.reshape(1, ts_, dm)
    u_s, q_s, mg_s, ng_s, *kv_new = _proj(xs3, mod_s[1], mod_s[0], map_s, g_pre1, wa, wkvt, tm_s, pw, aw)

    cw = _prep_cmp_weights(w['cmp_pe'], w['cmp_w1'], w['cmp_b1'], w['cmp_w2'])
    page_tab = page_table.reshape(-1).astype(jnp.int32)
    ckvt_p = _compress_prompt(kvc_p, cw)
    ckvt_s = _compress_paged(page_tab, n_pages, _transpose_rows(cache_cmp), cw)
    attn_p = _nsa_prompt(q_p, ng_p, ckvt_p, kvs_p, kvw_p, LANES)
    new_t = jnp.concatenate(kv_new, axis=1)[0].reshape(3 * KV_ROWS, bd, qn).transpose(1, 0, 2)
    new_pad = jnp.pad(new_t, ((0, 0), (0, 0), (0, LANES - qn)))
    attn_s = _nsa_sample(page_tab, n_pages, q_s.reshape(bd, qn, aw), ng_s.reshape(bd, qn, LANES), ckvt_s,
                         _transpose_rows(cache_sel), new_pad[:, KV_ROWS:2 * KV_ROWS], _transpose_rows(st_win),
                         new_pad[:, 2 * KV_ROWS:])

    pool_w, pool_scale = w['pool_w'], w['pool_scale'][None]
    hpt = tm_p // POOL_HIST
    py_p = _pool(u_p, lambda b, i: (b, jnp.maximum(i * hpt - 1, 0), 0), u_p, pool_w, pool_scale, tm_p, 0, True)
    p_buf = st_pool.shape[1]
    assert p_buf == max(POOL_WINDOWS) - 1
    u_sb = u_s.reshape(bd, qn, pw)
    hist_s = jnp.pad(st_pool, ((0, 0), (POOL_HIST - p_buf, 0), (0, 0)))
    py_s = _pool(hist_s, lambda b, i: (b, 0, 0), u_sb, pool_w, pool_scale, qn, past, False)

    rw = jnp.pad(w['router_w'], ((0, 0), (0, LANES - n_exp)))
    rwh = rw.astype(BF16)
    rwl = (rw - rwh.astype(F32)).astype(BF16)
    rb = jnp.pad(w['router_b'], (0, LANES - n_exp), constant_values=NEG_BIG)[None]
    wba = w['w_br_attn'].reshape(N_HEADS, HEAD_DIM, dm)[order].reshape(aw, dm).astype(BF16)
    mweights = [w['g_post1'][None], w['g_pre2'][None], w['w_br_pool'].astype(BF16), wba, w['w_o'].astype(BF16),
                rwh, rwl, rb]
    cnt0 = jnp.zeros((1, LANES), F32)
    x1_p, h2_p, route_p, cnt_p = _merge(xp, py_p, attn_p, mg_p, mod_p[2], mod_p[3], mod_p[4], map_p, mweights,
                                        cnt0, tm_p)
    x1_s, h2_s, route_s, cnt_a = _merge(xs3, py_s.reshape(1, ts_, pw), attn_s.reshape(1, ts_, aw), mg_s, mod_s[2],
                                        mod_s[3], mod_s[4], map_s, mweights, cnt_p, tm_s)

    tp_ = nb * s
    route = jnp.concatenate([route_p.reshape(tp_, LANES), route_s.reshape(ts_, LANES)], axis=0)
    eidx = route[:, 0:TOP_K].astype(jnp.int32)
    rank = route[:, 2 * TOP_K:3 * TOP_K].astype(jnp.int32)
    counts = cnt_a[0, :n_exp].astype(jnp.int32)
    padded = (counts + MOE_ROWS - 1) // MOE_ROWS * MOE_ROWS
    pend = jnp.cumsum(padded)
    first = jnp.where(eidx[:, :, None] == jnp.arange(n_exp, dtype=jnp.int32), pend - padded, 0).sum(-1)
    dest = (first + rank).reshape(-1)
    n_blocks = (tp_ + ts_) * TOP_K // MOE_ROWS + n_exp
    blk_start = jnp.arange(n_blocks, dtype=jnp.int32) * MOE_ROWS
    blk_e = jnp.minimum(jnp.sum((pend[None, :] <= blk_start[:, None]).astype(jnp.int32), axis=1), n_exp - 1)
    n_live = (pend[-1:] // MOE_ROWS).astype(jnp.int32)

    clear_rows = jnp.maximum(pend - MOE_ROWS, 0).astype(jnp.int32)
    assert tp_ % tm_s == 0
    xs_rows = _dispatch(clear_rows, n_live, dest, h2_p.reshape(tp_, dm), h2_s.reshape(ts_, dm), n_blocks, tm_s)
    ys = _experts(blk_e, n_live, xs_rows, w['w_gu'], w['b_gu'], w['w_down'], w['b_down'])
    g_post2 = w['g_post2'][None]
    y_p = _final(dest[:tp_ * TOP_K], route_p, x1_p, mod_p[5], map_p, g_post2, ys, tm_p)
    y_s = _final(dest[tp_ * TOP_K:], route_s, x1_s, mod_s[5], map_s, g_post2, ys, tm_s)

    wlen = min(WINDOW, s)
    kv_p = [_untranspose_rows(kvc_p), _untranspose_rows(kvs_p)]
    kvw_p = _untranspose_rows(kvw_p[:, :, s - wlen:])
    pool_p = u_p[:, s - min(POOL_HIST - 1, s):]
    kv_s = [_untranspose_rows(new_t[:, j * KV_ROWS:(j + 1) * KV_ROWS]) for j in range(3)]
    w_buf = st_win.shape[1]
    kvw_s = jnp.concatenate([st_win, kv_s[2]], axis=1)[:, -w_buf:]
    pool_s = jnp.concatenate([st_pool, u_sb], axis=1)[:, -p_buf:]
    return y_p, y_s.reshape(bd, qn, dm), (kv_p[0], kv_p[1], kvw_p, pool_p, kv_s[0], kv_s[1], kvw_s, pool_s)


def kernel(x_prompt, x_sample, cache_kv_cmp, cache_kv_sel, state_kv_win, state_pool, page_table, c_prompt, c_sample,
           w_ada, b_ada, g_pre1, g_post1, g_pre2, g_post2, w_in, pool_w, pool_scale, cmp_pe, cmp_w1, cmp_b1, cmp_w2,
           w_br_pool, w_br_attn, w_o, router_w, router_b, w_gu, b_gu, w_down, b_down):
    names = ('w_ada', 'b_ada', 'g_pre1', 'g_post1', 'g_pre2', 'g_post2', 'w_in', 'pool_w', 'pool_scale', 'cmp_pe',
             'cmp_w1', 'cmp_b1', 'cmp_w2', 'w_br_pool', 'w_br_attn', 'w_o', 'router_w', 'router_b', 'w_gu', 'b_gu',
             'w_down', 'b_down')
    stacked = (w_ada, b_ada, g_pre1, g_post1, g_pre2, g_post2, w_in, pool_w, pool_scale, cmp_pe, cmp_w1, cmp_b1,
               cmp_w2, w_br_pool, w_br_attn, w_o, router_w, router_b, w_gu, b_gu, w_down, b_down)
    xp, xs = x_prompt, x_sample
    states = []
    for l in range(w_ada.shape[0]):
        w = {n: a[l] for n, a in zip(names, stacked)}
        xp, xs, st = _layer(xp, xs, cache_kv_cmp[l], cache_kv_sel[l], state_kv_win[l], state_pool[l], page_table,
                            c_prompt, c_sample, w)
        states.append(st)
    return (xp, xs) + tuple(jnp.stack([st[j] for st in states]) for j in range(8))
```

```python
import functools

import jax
import jax.numpy as jnp
import numpy as np
from jax import lax
from jax.experimental import pallas as pl
from jax.experimental.pallas import tpu as pltpu

F32 = jnp.float32
BF16 = jnp.bfloat16

POOL_WINDOWS = (2, 4, 8, 16)
N_HEADS = 8
N_KV_HEADS = 2
HEAD_DIM = 64
GQ = N_HEADS // N_KV_HEADS
CMP_BLOCK = 32
CMP_STRIDE = 16
SEL_BLOCK = 64
SUB_PER_SEL = SEL_BLOCK // CMP_STRIDE
N_SEL = 16
N_LOCAL = 2
WINDOW = 512
TOP_K = 4
SWIGLU_LIMIT = 7.0
SWIGLU_ALPHA = 1.702
RMS_EPS = 1e-6

LANES = 128
SUBLANES = 8
VMEM_LIMIT = 56 * 1024 * 1024
NEG_BIG = -1e30
M_INIT = -1e29
MXU_DEPTH = 256
FAR = 1 << 30
KV_ROWS = 2 * N_KV_HEADS * HEAD_DIM
HALF = N_KV_HEADS * HEAD_DIM
POOL_HIST = 16
MOE_ROWS = 512
PAGE_UNROLL = 32
SAMPLE_TILE_PAGES = 64


def _cparams(*sem):
    return pltpu.CompilerParams(dimension_semantics=sem, vmem_limit_bytes=VMEM_LIMIT)


def _full_spec(w):
    nd = w.ndim
    return pl.BlockSpec(w.shape, lambda *a: (0,) * nd)


def _dot(a, b):
    return jnp.dot(a, b, preferred_element_type=F32)


def _dot_nt(a, b):
    return lax.dot_general(a, b, (((1,), (1,)), ((), ())), preferred_element_type=F32)


def _split3(x):
    hi = x.astype(BF16)
    r1 = x - hi.astype(F32)
    mid = r1.astype(BF16)
    lo = (r1 - mid.astype(F32)).astype(BF16)
    return hi, mid, lo


def _dot_exact_rhs(x, m_bf16):
    hi, mid, lo = _split3(x)
    return _dot(hi, m_bf16) + _dot(mid, m_bf16) + _dot(lo, m_bf16)


def _mod_rows(ref, tm):
    v = ref[0]
    r, dm = v.shape
    if r in (1, tm):
        return v
    return jnp.broadcast_to(v[:, None, :], (r, tm // r, dm)).reshape(tm, dm)


def _rms(x, g):
    return x * lax.rsqrt(jnp.mean(x * x, axis=-1, keepdims=True) + RMS_EPS) * g


def _ada_kernel(c_ref, w_ref, b_ref, o_ref):
    o_ref[...] = _dot(c_ref[...].astype(BF16), w_ref[...].astype(BF16)) + b_ref[...]


def _ada(c_all, w_ada, b_ada):
    m, d = c_all.shape
    n = w_ada.shape[1]
    tn = 1024
    return pl.pallas_call(
        _ada_kernel,
        grid=(n // tn,),
        in_specs=[pl.BlockSpec((m, d), lambda j: (0, 0)),
                  pl.BlockSpec((d, tn), lambda j: (0, j)),
                  pl.BlockSpec((1, tn), lambda j: (0, j))],
        out_specs=pl.BlockSpec((m, tn), lambda j: (0, j)),
        out_shape=jax.ShapeDtypeStruct((m, n), F32),
        compiler_params=_cparams("arbitrary"),
        name="ada",
    )(c_all, w_ada, b_ada)


def _proj_kernel(x_ref, sc_ref, sh_ref, g_ref, wa_ref, wkv_ref,
                 u_ref, q_ref, mg_ref, ng_ref, kvc_ref, kvs_ref, kvw_ref, *, pw, aw, dm):
    tm = x_ref.shape[1]
    hb = (_rms(x_ref[0], g_ref[...]) * (1.0 + _mod_rows(sc_ref, tm)) + _mod_rows(sh_ref, tm)).astype(BF16)
    u_ref[0] = _dot(hb, wa_ref[:, 0:pw])
    q_ref[0] = _dot(hb, wa_ref[:, pw:pw + aw])
    o = pw + aw
    mg_ref[0] = jax.nn.sigmoid(_dot(hb, wa_ref[:, o:o + 2 * dm])).astype(BF16)
    ng_ref[0] = jax.nn.sigmoid(_dot(hb, wa_ref[:, o + 2 * dm:o + 2 * dm + LANES]))
    kvt = _dot_nt(wkv_ref[...], hb)
    for j, ref in enumerate((kvc_ref, kvs_ref, kvw_ref)):
        ref[0] = kvt[j * KV_ROWS:(j + 1) * KV_ROWS]


def _proj(x3, sc, sh, mod_map, g_pre1, wa, wkvt, tm, pw, aw):
    nb, s, dm = x3.shape
    nt = s // tm
    kern = functools.partial(_proj_kernel, pw=pw, aw=aw, dm=dm)
    mod_block = (1,) + sc.shape[1:]
    row = lambda b, i: (b, i, 0)
    return pl.pallas_call(
        kern,
        grid=(nb, nt),
        in_specs=[pl.BlockSpec((1, tm, dm), row),
                  pl.BlockSpec(mod_block, mod_map),
                  pl.BlockSpec(mod_block, mod_map),
                  _full_spec(g_pre1), _full_spec(wa), _full_spec(wkvt)],
        out_specs=[pl.BlockSpec((1, tm, pw), row),
                   pl.BlockSpec((1, tm, aw), row),
                   pl.BlockSpec((1, tm, 2 * dm), row),
                   pl.BlockSpec((1, tm, LANES), row),
                   ] + [pl.BlockSpec((1, KV_ROWS, tm), lambda b, i: (b, 0, i))] * 3,
        out_shape=[jax.ShapeDtypeStruct((nb, s, pw), F32),
                   jax.ShapeDtypeStruct((nb, s, aw), F32),
                   jax.ShapeDtypeStruct((nb, s, 2 * dm), BF16),
                   jax.ShapeDtypeStruct((nb, s, LANES), F32),
                   ] + [jax.ShapeDtypeStruct((nb, KV_ROWS, s), F32)] * 3,
        compiler_params=_cparams("arbitrary", "arbitrary"),
        name="proj",
    )(x3, sc, sh, g_pre1, wa, wkvt)


def _prep_in_weights(w_in):
    dm = w_in.shape[0]
    pw = dm // 2
    aw = N_HEADS * HEAD_DIM
    off_kvc = pw + aw
    off_ng = off_kvc + 3 * KV_ROWS
    off_mg = off_ng + 3 * N_HEADS
    order = np.array([k * GQ + g for g in range(GQ) for k in range(N_KV_HEADS)])
    wq = w_in[:, pw:off_kvc].reshape(dm, N_HEADS, HEAD_DIM)[:, order].reshape(dm, aw) * (HEAD_DIM ** -0.5)
    wng = jnp.pad(w_in[:, off_ng:off_mg], ((0, 0), (0, LANES - 3 * N_HEADS)))
    wa = jnp.concatenate([w_in[:, :pw], wq, w_in[:, off_mg:], wng], axis=1).astype(BF16)
    wkvt = w_in[:, off_kvc:off_ng].T.astype(BF16)
    return wa, wkvt, pw, aw, order


def _compress_core(get_page, n_pages, perm_ref, wp_ref, w2t_ref, pe_ref, w1f_ref, b1_ref, s2_ref, out_ref):
    sub_pp = LANES // CMP_STRIDE
    n_sub = n_pages * sub_pp
    r_cmp = CMP_BLOCK // CMP_STRIDE
    hid = w1f_ref.shape[-1]

    def page_body(p, carry):
        pg = get_page(p).astype(BF16)
        t = _dot_nt(perm_ref[...], pg)
        row0 = pl.multiple_of(p * sub_pp, sub_pp)
        for s in range(2):
            for j in range(CMP_STRIDE):
                s2_ref[s, j // 2, pl.ds(row0, sub_pp), (j % 2) * HALF:(j % 2 + 1) * HALF] = (
                    t[sub_pp * j:sub_pp * (j + 1), s * HALF:(s + 1) * HALF])
        return carry

    lax.fori_loop(0, n_pages, page_body, 0, unroll=PAGE_UNROLL)
    for s in range(2):
        acc = None
        for jp in range(CMP_STRIDE // 2):
            d = _dot(s2_ref[s, jp].astype(BF16), wp_ref[s, jp])
            acc = d if acc is None else acc + d
        bias = _dot(pe_ref[s], w1f_ref[s])[0:1] + b1_ref[s]
        bias = jnp.concatenate([bias] * N_KV_HEADS, axis=1)
        w = N_KV_HEADS * hid
        hpre = acc[:, :w] + bias
        for r in range(1, r_cmp):
            hpre = hpre + pltpu.roll(acc[:, r * w:(r + 1) * w], n_sub - r, axis=0)
        g = jax.nn.gelu(hpre).astype(BF16)
        out_ref[s * HALF:(s + 1) * HALF, :] = _dot_nt(w2t_ref[s], g)


def _compress_prompt_kernel(kv_ref, perm_ref, wp_ref, w2t_ref, pe_ref, w1f_ref, b1_ref, out_ref, s2_ref, *, n_pages):
    def get_page(p):
        return kv_ref[0, :, pl.ds(pl.multiple_of(p * LANES, LANES), LANES)]
    _compress_core(get_page, n_pages, perm_ref, wp_ref, w2t_ref, pe_ref, w1f_ref, b1_ref, s2_ref, out_ref.at[0])


def _page_fetch(pt_ref, cache_hbm, buf, sem, b, slot, n_pages):
    def start(p, c):
        pltpu.make_async_copy(cache_hbm.at[pt_ref[b * n_pages + p]], buf.at[slot, p], sem.at[slot]).start()
        return c
    lax.fori_loop(0, n_pages, start, 0)


def _page_wait(cache_hbm, buf, sem, slot, n_pages):
    def wait(p, c):
        pltpu.make_async_copy(cache_hbm.at[0], buf.at[slot, p], sem.at[slot]).wait()
        return c
    lax.fori_loop(0, n_pages, wait, 0)


def _paged_prologue(pt_ref, cache_hbm, buf, sem, n_pages):
    b = pl.program_id(0)
    slot = b % 2

    @pl.when(b == 0)
    def _():
        _page_fetch(pt_ref, cache_hbm, buf, sem, 0, 0, n_pages)

    @pl.when(b + 1 < pl.num_programs(0))
    def _():
        _page_fetch(pt_ref, cache_hbm, buf, sem, b + 1, 1 - slot, n_pages)

    _page_wait(cache_hbm, buf, sem, slot, n_pages)
    return slot


def _compress_paged_kernel(pt_ref, cache_hbm, perm_ref, wp_ref, w2t_ref, pe_ref, w1f_ref, b1_ref, out_ref,
                           buf, sem, s2_ref, *, n_pages):
    slot = _paged_prologue(pt_ref, cache_hbm, buf, sem, n_pages)
    _compress_core(lambda p: buf[slot, p], n_pages, perm_ref, wp_ref, w2t_ref, pe_ref, w1f_ref, b1_ref,
                   s2_ref, out_ref.at[0])


def _prep_cmp_weights(cmp_pe, cmp_w1, cmp_b1, cmp_w2):
    hid = cmp_w1.shape[-1]
    r_cmp = CMP_BLOCK // CMP_STRIDE
    eye = jnp.eye(N_KV_HEADS, dtype=F32)
    w1r = cmp_w1.reshape(2, r_cmp, CMP_STRIDE // 2, 2, HEAD_DIM, hid)
    wp = jnp.einsum('srpjdh,kc->spjkdrch', w1r, eye).reshape(
        2, CMP_STRIDE // 2, 2 * HALF, r_cmp * N_KV_HEADS * hid).astype(BF16)
    w2t = jnp.einsum('shd,kc->skdch', cmp_w2, eye).reshape(2, HALF, N_KV_HEADS * hid).astype(BF16)
    pe = jnp.broadcast_to(cmp_pe.reshape(2, 1, CMP_BLOCK * HEAD_DIM),
                          (2, SUBLANES, CMP_BLOCK * HEAD_DIM)).astype(BF16)
    w1f = cmp_w1.reshape(2, CMP_BLOCK * HEAD_DIM, hid).astype(BF16)
    b1 = cmp_b1.reshape(2, 1, hid)
    sub_pp = LANES // CMP_STRIDE
    x = np.arange(LANES)
    perm = np.zeros((LANES, LANES), np.float32)
    perm[x, (x % sub_pp) * CMP_STRIDE + x // sub_pp] = 1.0
    return (jnp.asarray(perm, BF16), wp, w2t, pe, w1f, b1)


def _compress_prompt(kvt, cw):
    nb, _, s = kvt.shape
    n_pages = s // LANES
    n_sub = s // CMP_STRIDE
    return pl.pallas_call(
        functools.partial(_compress_prompt_kernel, n_pages=n_pages),
        grid=(nb,),
        in_specs=[pl.BlockSpec((1, KV_ROWS, s), lambda b: (b, 0, 0))] + [_full_spec(w) for w in cw],
        out_specs=pl.BlockSpec((1, KV_ROWS, n_sub), lambda b: (b, 0, 0)),
        out_shape=jax.ShapeDtypeStruct((nb, KV_ROWS, n_sub), F32),
        scratch_shapes=[pltpu.VMEM((2, CMP_STRIDE // 2, n_sub, 2 * HALF), F32)],
        compiler_params=_cparams("arbitrary"),
        name="compress_prompt",
    )(kvt, *cw)


def _compress_paged(page_tab, n_pages, cache_t, cw):
    nb = page_tab.shape[0] // n_pages
    n_sub = n_pages * (LANES // CMP_STRIDE)
    return pl.pallas_call(
        functools.partial(_compress_paged_kernel, n_pages=n_pages),
        grid_spec=pltpu.PrefetchScalarGridSpec(
            num_scalar_prefetch=1,
            grid=(nb,),
            in_specs=[pl.BlockSpec(memory_space=pl.ANY)] + [_full_spec(w) for w in cw],
            out_specs=pl.BlockSpec((1, KV_ROWS, n_sub), lambda b, pt: (b, 0, 0)),
            scratch_shapes=[pltpu.VMEM((2, n_pages, KV_ROWS, LANES), F32),
                            pltpu.SemaphoreType.DMA((2,)),
                            pltpu.VMEM((2, CMP_STRIDE // 2, n_sub, 2 * HALF), F32)]),
        out_shape=jax.ShapeDtypeStruct((nb, KV_ROWS, n_sub), F32),
        compiler_params=_cparams("arbitrary"),
        name="compress_paged",
    )(page_tab, cache_t, *cw)


def _stack_heads(fn):
    return jnp.concatenate([fn(k, g) for k in range(N_KV_HEADS) for g in range(GQ)], axis=0)


def _alibi_slope(k, g):
    return 2.0 ** (-8.0 * (k * GQ + g + 1) / N_HEADS)


def _nsa_block(q, sgate, t0, ckvt_ref, mmat_ref, bmat_ref, sel_tile, n_sel_tiles, sel_tk, sel_last, win_tile,
               m_ref, acc_ref, out_ref, *, qt, nblk_pad, n_pick, active=None):
    t_pos = t0 + lax.broadcasted_iota(jnp.int32, (qt, 1), 0)
    lane_half = lax.broadcasted_iota(jnp.int32, (qt, LANES), 1) // HEAD_DIM

    qp = _stack_heads(lambda k, g: jnp.where(lane_half == k, q[:, g * LANES:(g + 1) * LANES], 0.0)).astype(BF16)

    def alibi(distf):
        return _stack_heads(lambda k, g: _alibi_slope(k, g) * distf)

    def stack_masks(mk):
        return jnp.concatenate([mk[k] for k in range(N_KV_HEADS) for _ in range(GQ)], axis=0) > 0.5

    nc = ckvt_ref.shape[-1]
    c_end = lax.broadcasted_iota(jnp.int32, (1, nc), 1) * CMP_STRIDE + (CMP_BLOCK - 1)
    dist_c = t_pos - c_end
    valid_c = jnp.where(dist_c >= 0, 1.0, 0.0)
    mask_c = stack_masks([valid_c] * N_KV_HEADS)
    s = _dot(qp, ckvt_ref[0:HALF, :].astype(BF16)) - alibi(dist_c.astype(F32))
    s = jnp.where(mask_c, s, NEG_BIG)
    p = jnp.where(mask_c, jnp.exp(s - jnp.max(s, axis=1, keepdims=True)), 0.0)
    p = p / jnp.maximum(jnp.sum(p, axis=1, keepdims=True), 1e-30)
    o_c = _dot_nt(p.astype(BF16), ckvt_ref[HALF:2 * HALF, :].astype(BF16))

    blk = lax.broadcasted_iota(jnp.int32, (qt, nblk_pad), 1)
    cur = t_pos // SEL_BLOCK
    forced = (blk == 0) | ((blk > cur - N_LOCAL) & (blk <= cur))
    free = (blk >= 1) & (blk <= cur - N_LOCAL)
    blkf = blk.astype(F32)
    sel = []
    for k in range(N_KV_HEADS):
        imp = p[k * GQ * qt:(k * GQ + 1) * qt]
        for g in range(1, GQ):
            imp = imp + p[(k * GQ + g) * qt:(k * GQ + g + 1) * qt]
        score = jnp.where(free, _dot_exact_rhs(imp, mmat_ref[...]), -jnp.inf)
        chosen = jnp.where(forced, 1.0, 0.0)
        for _ in range(n_pick):
            best = jnp.max(score, axis=1, keepdims=True)
            idx = jnp.min(jnp.where(score == best, blkf, float(nblk_pad)), axis=1, keepdims=True)
            hit = blkf == idx
            chosen = jnp.where(hit, 1.0, chosen)
            score = jnp.where(hit, -jnp.inf, score)
        sel.append(chosen)

    rk = GQ * qt
    lane = lax.broadcasted_iota(jnp.int32, (1, LANES), 1)
    t_rows = jnp.concatenate([t_pos] * GQ, axis=0)

    def q_feat(k):
        parts = []
        for g in range(GQ):
            sl = _alibi_slope(k, g)
            base = (1 - k) * HEAD_DIM
            ext = jnp.where(lane == base, SEL_BLOCK * sl, jnp.where(lane == base + 1, sl, 0.0))
            parts.append(jnp.where(lane_half == k, q[:, g * LANES:(g + 1) * LANES], ext))
        return jnp.concatenate(parts, axis=0).astype(BF16)

    qf = [q_feat(k) for k in range(N_KV_HEADS)]

    def k_aug(kt, pos, use_sel):
        tk = kt.shape[-1]
        hi = lax.shift_right_arithmetic(pos, SEL_BLOCK.bit_length() - 1)
        lo = pos & (SEL_BLOCK - 1)
        r = lax.broadcasted_iota(jnp.int32, (HEAD_DIM, tk), 0)
        ext = jnp.where(r == 0, hi.astype(F32), jnp.where(r == 1, lo.astype(F32), 0.0)).astype(BF16)
        if use_sel:
            onehot = jnp.where(lax.broadcasted_iota(jnp.int32, (nblk_pad, 1), 0) == hi, 1.0, 0.0).astype(BF16)
        else:
            onehot = jnp.zeros((nblk_pad, tk), BF16)
        kb = kt.astype(BF16)
        return [jnp.concatenate([onehot, kb[0:HEAD_DIM], ext], axis=0),
                jnp.concatenate([onehot, ext, kb[HEAD_DIM:HALF]], axis=0)]

    m_ref[...] = jnp.full(m_ref.shape, M_INIT, F32)
    acc_ref[...] = jnp.zeros(acc_ref.shape, F32)

    def flash(br, qa, kt, vt, pos, use_sel, cond):
        ka = k_aug(kt, pos, use_sel)
        va = jnp.concatenate([vt.astype(BF16), jnp.ones((2 * SUBLANES, kt.shape[-1]), BF16)], axis=0)
        for k in range(N_KV_HEADS):
            sc = _dot(qa[k], ka[k])
            if cond is not None:
                sc = jnp.where(cond, sc, NEG_BIG)
            m_old = m_ref[br, k]
            m_new = jnp.maximum(m_old, jnp.max(sc, axis=1, keepdims=True))
            pr = jnp.exp(sc - m_new).astype(BF16)
            acc_ref[br, k] = jnp.exp(m_old - m_new) * acc_ref[br, k] + _dot_nt(pr, va)
            m_ref[br, k] = m_new

    def result(br):
        out = []
        for k in range(N_KV_HEADS):
            a = acc_ref[br, k]
            out.append(a[:, 0:HALF] / jnp.maximum(a[:, HALF:HALF + 1], 1e-30))
        return out

    qa_w = [jnp.concatenate([jnp.zeros((rk, nblk_pad), BF16), qf[k]], axis=1) for k in range(N_KV_HEADS)]
    kt, vt, pos = win_tile
    dist = t_rows - pos
    flash(1, qa_w, kt, vt, pos, False, (dist >= 0) & (dist <= WINDOW))

    qa_s = []
    for k in range(N_KV_HEADS):
        selneg = jnp.where(sel[k] > 0.5, 0.0, NEG_BIG).astype(BF16)
        qa_s.append(jnp.concatenate([jnp.concatenate([selneg] * GQ, axis=0), qf[k]], axis=1))

    def sel_step(j):
        kt, vt = sel_tile(j)
        flash(0, qa_s, kt, vt, j * sel_tk + lax.broadcasted_iota(jnp.int32, (1, sel_tk), 1), True, None)

    if isinstance(n_sel_tiles, int):
        for j in range(n_sel_tiles):
            sel_step(j)
    else:
        act_ref, max_tiles = active
        tile_of_blk = lax.broadcasted_iota(jnp.int32, (nblk_pad, 1), 0) // (sel_tk // SEL_BLOCK)
        group = jnp.where(tile_of_blk == lax.broadcasted_iota(jnp.int32, (1, LANES), 1), 1.0, 0.0).astype(BF16)
        any_sel = jnp.max(jnp.maximum(sel[0], sel[1]), axis=0, keepdims=True)
        tile_any = _dot(jnp.broadcast_to(any_sel, (SUBLANES, nblk_pad)).astype(BF16), group)
        cnt = jnp.int32(0)
        for j in range(max_tiles):
            act_ref[cnt] = j
            keep = (tile_any[0, j] > 0.5) & (j < n_sel_tiles)
            cnt = cnt + keep.astype(jnp.int32)

        def pair_body(p, c):
            sel_step(act_ref[2 * p])
            sel_step(act_ref[2 * p + 1])
            return c

        lax.fori_loop(0, cnt // 2, pair_body, 0)

        @pl.when(cnt % 2 == 1)
        def _():
            sel_step(act_ref[cnt - 1])
    kt, vt, pos, use_sel = sel_last
    flash(0, qa_s, kt, vt, pos, use_sel, t_rows >= pos)
    o_s = result(0)
    o_w = result(1)

    o_c = [o_c[0:rk], o_c[rk:2 * rk]]
    g3 = _split3(sgate)
    for g in range(GQ):
        slab = None
        for c, o in enumerate((o_c, o_s, o_w)):
            bm = bmat_ref[g * 3 + c]
            gate = _dot(g3[0], bm) + _dot(g3[1], bm) + _dot(g3[2], bm)
            val = jnp.where(lane_half == 0, o[0][g * qt:(g + 1) * qt], o[1][g * qt:(g + 1) * qt])
            slab = gate * val if slab is None else slab + gate * val
        out_ref[:, g * LANES:(g + 1) * LANES] = slab


def _nsa_consts(nc, nblk_pad):
    j = np.arange(nc)[:, None]
    lo = SUB_PER_SEL * np.arange(nblk_pad)[None, :]
    mm = (np.where((j >= lo) & (j < lo + SUB_PER_SEL - 1), 2.0, 0.0)
          + np.where((j == lo - 1) | (j == lo + SUB_PER_SEL - 1), 1.0, 0.0))
    col = np.arange(LANES)[:, None]
    lane = np.arange(LANES)[None, :]
    bm = np.stack([(col == ((lane // HEAD_DIM) * GQ + g) * 3 + c)
                   for g in range(GQ) for c in range(3)]).astype(np.float32)
    return jnp.asarray(mm, BF16), jnp.asarray(bm, BF16)


def _lane_iota(n):
    return lax.broadcasted_iota(jnp.int32, (1, n), 1)


def _flash_scratch(qt, lead=()):
    rk = GQ * qt
    lead = (lead,) if isinstance(lead, int) else lead
    return [pltpu.VMEM(lead + (2, N_KV_HEADS, rk, 1), F32),
            pltpu.VMEM(lead + (2, N_KV_HEADS, rk, HALF + 2 * SUBLANES), F32)]


def _nsa_prompt_kernel(*refs, qt, tk, n_win, nblk_pad, n_pick):
    q_ref, ng_ref, ckvt_ref, kvs_ref = refs[:4]
    win_refs = refs[4:4 + n_win]
    mmat_ref, bmat_ref, out_ref, m_ref, acc_ref, act_ref = refs[4 + n_win:]
    i = pl.program_id(1)

    def sel_tile(j):
        off = pl.multiple_of(j * tk, LANES)
        return kvs_ref[0, 0:HALF, pl.ds(off, tk)], kvs_ref[0, HALF:2 * HALF, pl.ds(off, tk)]

    n_past = (i * qt) // tk
    sel_last = sel_tile(n_past) + (n_past * tk + _lane_iota(tk), True)
    pieces = [w[0] for w in win_refs]
    pos = []
    for jj in range(n_win):
        bi = i - (n_win - 1) + jj
        pos.append(jnp.where(bi >= 0, bi * qt, -FAR) + _lane_iota(qt))
    win = (jnp.concatenate([p[0:HALF] for p in pieces], axis=1),
           jnp.concatenate([p[HALF:2 * HALF] for p in pieces], axis=1), jnp.concatenate(pos, axis=1))
    _nsa_block(q_ref[0], ng_ref[0], i * qt, ckvt_ref.at[0], mmat_ref, bmat_ref, sel_tile, n_past, tk, sel_last, win,
               m_ref, acc_ref, out_ref.at[0], qt=qt, nblk_pad=nblk_pad, n_pick=n_pick,
               active=(act_ref, act_ref.shape[0]))


def _nsa_prompt(q, ng, ckvt, kvs_t, kvw_t, qt):
    nb, s, aw = q.shape
    nc = ckvt.shape[-1]
    nblk = -(-s // SEL_BLOCK)
    nblk_pad = -(-nblk // LANES) * LANES
    assert nblk_pad + HALF <= MXU_DEPTH, "selection mask + features must fit one MXU contraction pass"
    n_pick = max(min(N_SEL, nblk) - (N_LOCAL + 1), 0)
    n_win = WINDOW // qt + 1
    tk = min(4 * qt, s)
    assert s % tk == 0 and tk % qt == 0
    mmat, bmat = _nsa_consts(nc, nblk_pad)
    row = lambda b, i: (b, i, 0)
    win_specs = [pl.BlockSpec((1, KV_ROWS, qt), (lambda jj: (lambda b, i: (b, 0, jnp.maximum(i - (n_win - 1) + jj, 0))))(jj))
                 for jj in range(n_win)]
    return pl.pallas_call(
        functools.partial(_nsa_prompt_kernel, qt=qt, tk=tk, n_win=n_win, nblk_pad=nblk_pad, n_pick=n_pick),
        grid=(nb, s // qt),
        in_specs=[pl.BlockSpec((1, qt, aw), row),
                  pl.BlockSpec((1, qt, LANES), row),
                  pl.BlockSpec((1, KV_ROWS, nc), lambda b, i: (b, 0, 0)),
                  pl.BlockSpec((1, KV_ROWS, s), lambda b, i: (b, 0, 0))] + win_specs
                 + [_full_spec(mmat), _full_spec(bmat)],
        out_specs=pl.BlockSpec((1, qt, aw), row),
        out_shape=jax.ShapeDtypeStruct((nb, s, aw), F32),
        scratch_shapes=_flash_scratch(qt) + [pltpu.SMEM((s // tk,), jnp.int32)],
        compiler_params=_cparams("arbitrary", "arbitrary"),
        name="nsa_prompt",
    )(q, ng, ckvt, kvs_t, *([kvw_t] * n_win), mmat, bmat)


def _nsa_sample_kernel(pt_ref, q_ref, ng_ref, ckvt_ref, cache_hbm, ksn_ref, wst_ref, kwn_ref, mmat_ref, bmat_ref,
                       out_ref, buf, sem, m_ref, acc_ref, *, qt, nbs, n_pages, ppt, nblk_pad, n_pick):
    slot = _paged_prologue(pt_ref, cache_hbm, buf, sem, nbs * n_pages)
    past = n_pages * LANES
    wbuf = wst_ref.shape[-1]
    for bb in range(nbs):
        def sel_tile(j, bb=bb):
            p0 = bb * n_pages + j * ppt
            ks = [buf[slot, p0 + pp, 0:HALF, :] for pp in range(ppt)]
            vs = [buf[slot, p0 + pp, HALF:2 * HALF, :] for pp in range(ppt)]
            return jnp.concatenate(ks, axis=1), jnp.concatenate(vs, axis=1)

        sel_last = (ksn_ref[bb, 0:HALF, :], ksn_ref[bb, HALF:2 * HALF, :], past + _lane_iota(LANES), False)
        win = (jnp.concatenate([wst_ref[bb, 0:HALF, :], kwn_ref[bb, 0:HALF, :]], axis=1),
               jnp.concatenate([wst_ref[bb, HALF:2 * HALF, :], kwn_ref[bb, HALF:2 * HALF, :]], axis=1),
               past - wbuf + _lane_iota(wbuf + LANES))
        _nsa_block(q_ref[bb], ng_ref[bb], past, ckvt_ref.at[bb], mmat_ref, bmat_ref, sel_tile, n_pages // ppt,
                   ppt * LANES, sel_last, win, m_ref.at[bb], acc_ref.at[bb], out_ref.at[bb],
                   qt=qt, nblk_pad=nblk_pad, n_pick=n_pick)


def _nsa_sample(page_tab, n_pages, q, ng, ckvt, cache_t, ks_new, w_state, kw_new):
    nb, qt, aw = q.shape
    nc = ckvt.shape[-1]
    past = n_pages * LANES
    assert past % SEL_BLOCK == 0 and qt <= SEL_BLOCK
    nblk_pad = -(-(past // SEL_BLOCK) // LANES) * LANES
    assert nblk_pad + HALF <= MXU_DEPTH
    n_pick = max(min(N_SEL, past // SEL_BLOCK + 1) - (N_LOCAL + 1), 0)
    ppt = min(SAMPLE_TILE_PAGES, n_pages)
    nbs = 2 if nb % 2 == 0 else 1
    assert n_pages % ppt == 0
    mmat, bmat = _nsa_consts(nc, nblk_pad)
    b3 = lambda b, pt: (b, 0, 0)
    return pl.pallas_call(
        functools.partial(_nsa_sample_kernel, qt=qt, nbs=nbs, n_pages=n_pages, ppt=ppt, nblk_pad=nblk_pad,
                          n_pick=n_pick),
        grid_spec=pltpu.PrefetchScalarGridSpec(
            num_scalar_prefetch=1,
            grid=(nb // nbs,),
            in_specs=[pl.BlockSpec((nbs, qt, aw), b3),
                      pl.BlockSpec((nbs, qt, LANES), b3),
                      pl.BlockSpec((nbs, KV_ROWS, nc), b3),
                      pl.BlockSpec(memory_space=pl.ANY),
                      pl.BlockSpec((nbs, KV_ROWS, LANES), b3),
                      pl.BlockSpec((nbs, KV_ROWS, w_state.shape[-1]), b3),
                      pl.BlockSpec((nbs, KV_ROWS, LANES), b3),
                      _full_spec(mmat), _full_spec(bmat)],
            out_specs=pl.BlockSpec((nbs, qt, aw), b3),
            scratch_shapes=[pltpu.VMEM((2, nbs * n_pages, KV_ROWS, LANES), F32),
                            pltpu.SemaphoreType.DMA((2,))] + _flash_scratch(qt, nbs)),
        out_shape=jax.ShapeDtypeStruct((nb, qt, aw), F32),
        compiler_params=_cparams("arbitrary"),
        name="nsa_sample",
    )(page_tab, q, ng, ckvt, cache_t, ks_new, w_state, kw_new, mmat, bmat)


def _pool_kernel(hist_ref, cur_ref, pw_ref, ps_ref, out_ref, *, ts, pos_base, zero_first):
    i = pl.program_id(1)
    hist = hist_ref[0]
    if zero_first:
        hist = jnp.where(i == 0, 0.0, hist)
    cur = cur_ref[0]
    ext = jnp.concatenate([hist, cur], axis=0)
    pos = pos_base + i * ts + lax.broadcasted_iota(jnp.int32, (ts, 1), 0)
    ys = []
    for gi, w in enumerate(POOL_WINDOWS):
        lanes = slice(gi * LANES, (gi + 1) * LANES)
        acc = ext[:, lanes]
        step = 1
        while step < w:
            acc = acc + pltpu.roll(acc, step, axis=0)
            step *= 2
        cnt = jnp.minimum(pos + 1, w).astype(F32)
        dlt = acc[POOL_HIST:] / cnt - cur[:, lanes]
        ys.append(_dot(dlt.astype(BF16), pw_ref[gi].astype(BF16)))
    out_ref[0] = jnp.concatenate(ys, axis=1) * ps_ref[...]


def _pool(hist_arr, hist_map, u, pool_w, pool_scale, ts, pos_base, zero_first):
    nb, s, pw = u.shape
    assert pw == len(POOL_WINDOWS) * LANES and all(w & (w - 1) == 0 and w <= POOL_HIST for w in POOL_WINDOWS)
    row = lambda b, i: (b, i, 0)
    return pl.pallas_call(
        functools.partial(_pool_kernel, ts=ts, pos_base=pos_base, zero_first=zero_first),
        grid=(nb, s // ts),
        in_specs=[pl.BlockSpec((1, POOL_HIST, pw), hist_map),
                  pl.BlockSpec((1, ts, pw), row),
                  _full_spec(pool_w), _full_spec(pool_scale)],
        out_specs=pl.BlockSpec((1, ts, pw), row),
        out_shape=jax.ShapeDtypeStruct((nb, s, pw), F32),
        compiler_params=_cparams("arbitrary", "arbitrary"),
        name="pool",
    )(hist_arr, u, pool_w, pool_scale)


def _merge_kernel(x_ref, py_ref, ay_ref, mg_ref, gt1_ref, sh2_ref, sc2_ref, gpost1_ref, gpre2_ref,
                  wbp_ref, wba_ref, wo_ref, rwh_ref, rwl_ref, rb_ref, tri_ref, cnt0_ref,
                  x1_ref, h2_ref, route_ref, cnt_ref, *, dm):
    first = (pl.program_id(0) == 0) & (pl.program_id(1) == 0)

    @pl.when(first)
    def _():
        cnt_ref[...] = cnt0_ref[...]

    bp = _dot(py_ref[0].astype(BF16), wbp_ref[...])
    ba = _dot(ay_ref[0].astype(BF16), wba_ref[...])
    merged = mg_ref[0, :, 0:dm] * bp + mg_ref[0, :, dm:2 * dm] * ba
    mix = _dot(merged.astype(BF16), wo_ref[...])
    tm = x_ref.shape[1]
    x1 = x_ref[0] + _mod_rows(gt1_ref, tm) * _rms(mix, gpost1_ref[...])
    h2 = _rms(x1, gpre2_ref[...]) * (1.0 + _mod_rows(sc2_ref, tm)) + _mod_rows(sh2_ref, tm)
    x1_ref[0] = x1
    h2_ref[0] = h2

    hh = h2.astype(BF16)
    hl = (h2 - hh.astype(F32)).astype(BF16)
    logits = _dot(hh, rwh_ref[...]) + _dot(hl, rwh_ref[...]) + _dot(hh, rwl_ref[...]) + rb_ref[...]
    tm = logits.shape[0]
    lane = lax.broadcasted_iota(jnp.int32, (tm, LANES), 1)
    lanef = lane.astype(F32)
    hits, vals, idxs = [], [], []
    for _ in range(TOP_K):
        best = jnp.max(logits, axis=1, keepdims=True)
        idx = jnp.min(jnp.where(logits == best, lanef, float(LANES)), axis=1, keepdims=True)
        hit = lanef == idx
        hits.append(hit)
        vals.append(best)
        idxs.append(idx)
        logits = jnp.where(hit, -jnp.inf, logits)
    ex = [jnp.exp(v - vals[0]) for v in vals]
    den = ex[0]
    for e in ex[1:]:
        den = den + e
    onehot = jnp.where(hits[0], 1.0, 0.0)
    for h in hits[1:]:
        onehot = onehot + jnp.where(h, 1.0, 0.0)
    before = _dot(tri_ref[...], onehot.astype(BF16)) + cnt_ref[...]
    route = jnp.zeros((tm, LANES), F32)
    for k in range(TOP_K):
        rank = jnp.sum(jnp.where(hits[k], before, 0.0), axis=1, keepdims=True)
        route = route + jnp.where(lane == k, idxs[k], 0.0)
        route = route + jnp.where(lane == TOP_K + k, ex[k] / den, 0.0)
        route = route + jnp.where(lane == 2 * TOP_K + k, rank, 0.0)
    route_ref[0] = route
    cnt_ref[...] = cnt_ref[...] + jnp.sum(onehot, axis=0, keepdims=True)


def _merge(x3, py, ay, mg, gt1, sh2, sc2, mod_map, weights, cnt0, tm):
    nb, s, dm = x3.shape
    row = lambda b, i: (b, i, 0)
    mod_block = (1,) + gt1.shape[1:]
    tri = jnp.asarray(np.tril(np.ones((tm, tm), np.float32), -1), BF16)
    consts = list(weights) + [tri, cnt0]
    return pl.pallas_call(
        functools.partial(_merge_kernel, dm=dm),
        grid=(nb, s // tm),
        in_specs=[pl.BlockSpec((1, tm, dm), row),
                  pl.BlockSpec((1, tm, py.shape[-1]), row),
                  pl.BlockSpec((1, tm, ay.shape[-1]), row),
                  pl.BlockSpec((1, tm, 2 * dm), row),
                  pl.BlockSpec(mod_block, mod_map), pl.BlockSpec(mod_block, mod_map), pl.BlockSpec(mod_block, mod_map)]
                 + [_full_spec(w) for w in consts],
        out_specs=[pl.BlockSpec((1, tm, dm), row), pl.BlockSpec((1, tm, dm), row),
                   pl.BlockSpec((1, tm, LANES), row), pl.BlockSpec((1, LANES), lambda b, i: (0, 0))],
        out_shape=[jax.ShapeDtypeStruct((nb, s, dm), F32), jax.ShapeDtypeStruct((nb, s, dm), F32),
                   jax.ShapeDtypeStruct((nb, s, LANES), F32), jax.ShapeDtypeStruct((1, LANES), F32)],
        compiler_params=_cparams("arbitrary", "arbitrary"),
        name="merge",
    )(x3, py, ay, mg, gt1, sh2, sc2, *consts)


def _scatter_rows(dest_ref, h_ref, xs_out, sem, tm):
    def row_copy(r, d):
        return pltpu.make_async_copy(h_ref.at[pl.ds(r, 1), :], xs_out.at[pl.ds(d, 1), :], sem)

    def start(r, c):
        for k in range(TOP_K):
            row_copy(r, dest_ref[r * TOP_K + k]).start(priority=k % 2)
        return c

    def wait(r, c):
        for k in range(TOP_K):
            row_copy(0, 0).wait()
        return c

    lax.fori_loop(0, tm, start, 0)
    lax.fori_loop(0, tm, wait, 0)


def _dispatch_kernel(clr_ref, nl_ref, dest_ref, ha_ref, hb_ref, xs_out, sem, zero_ref, zsem, *, td, nta, n_exp,
                     n_blocks):
    i = pl.program_id(0)

    @pl.when(i == 0)
    def _():
        zero_ref[...] = jnp.zeros(zero_ref.shape, F32)

        def zcopy(row0):
            return pltpu.make_async_copy(zero_ref, xs_out.at[pl.ds(pl.multiple_of(row0, MOE_ROWS), MOE_ROWS), :], zsem)

        def tail_start(j, c):
            zcopy(j * MOE_ROWS).start()
            return c

        def tail_wait(j, c):
            zcopy(0).wait()
            return c

        for e in range(n_exp):
            zcopy(clr_ref[e]).start()
        lax.fori_loop(nl_ref[0], n_blocks, tail_start, 0)
        for e in range(n_exp):
            zcopy(0).wait()
        lax.fori_loop(nl_ref[0], n_blocks, tail_wait, 0)

    @pl.when(i < nta)
    def _():
        _scatter_rows(dest_ref, ha_ref, xs_out, sem, td)

    @pl.when(i >= nta)
    def _():
        _scatter_rows(dest_ref, hb_ref, xs_out, sem, td)


def _dispatch(clear_rows, n_live, dest, h_a, h_b, n_blocks, td):
    ta, dm = h_a.shape
    tb = h_b.shape[0]
    n_exp = clear_rows.shape[0]
    nta = ta // td
    return pl.pallas_call(
        functools.partial(_dispatch_kernel, td=td, nta=nta, n_exp=n_exp, n_blocks=n_blocks),
        grid=(nta + tb // td,),
        in_specs=[pl.BlockSpec(memory_space=pltpu.SMEM),
                  pl.BlockSpec(memory_space=pltpu.SMEM),
                  pl.BlockSpec((td * TOP_K,), lambda i: (i,), memory_space=pltpu.SMEM),
                  pl.BlockSpec((td, dm), lambda i: (jnp.minimum(i, nta - 1), 0)),
                  pl.BlockSpec((td, dm), lambda i: (jnp.maximum(i - nta, 0), 0))],
        out_specs=pl.BlockSpec(memory_space=pl.ANY),
        out_shape=jax.ShapeDtypeStruct((n_blocks * MOE_ROWS, dm), F32),
        scratch_shapes=[pltpu.SemaphoreType.DMA(()), pltpu.VMEM((MOE_ROWS, dm), F32), pltpu.SemaphoreType.DMA(())],
        compiler_params=_cparams("arbitrary"),
        name="dispatch",
    )(clear_rows, n_live, dest, h_a, h_b)


def _expert_kernel(be_ref, nv_ref, x_ref, wgu_ref, bgu_ref, wd_ref, bd_ref, y_ref, wgu_bf, wd_bf, *, dff):
    i = pl.program_id(0)
    live = i < nv_ref[0]
    fresh = (i == 0) | (be_ref[i] != be_ref[jnp.maximum(i - 1, 0)])

    @pl.when(live & fresh)
    def _():
        wgu_bf[...] = wgu_ref[0].astype(BF16)
        wd_bf[...] = wd_ref[0].astype(BF16)

    @pl.when(live)
    def _():
        gu = _dot(x_ref[...].astype(BF16), wgu_bf[...]) + bgu_ref[0]
        gate = jnp.minimum(gu[:, :dff], SWIGLU_LIMIT)
        up = jnp.clip(gu[:, dff:], -SWIGLU_LIMIT, SWIGLU_LIMIT)
        act = (up + 1.0) * gate * jax.nn.sigmoid(SWIGLU_ALPHA * gate)
        y_ref[...] = _dot(act.astype(BF16), wd_bf[...]) + bd_ref[0]

    @pl.when(jnp.logical_not(live))
    def _():
        y_ref[...] = jnp.zeros(y_ref.shape, F32)


def _experts(blk_e, n_live, xs, w_gu, b_gu, w_down, b_down):
    rows, dm = xs.shape
    n_exp, _, dff2 = w_gu.shape
    dff = dff2 // 2
    nblk = rows // MOE_ROWS
    xmap = lambda i, be, nv: (jnp.minimum(i, nv[0] - 1), 0)
    emap = lambda i, be, nv: (be[i], 0, 0)
    return pl.pallas_call(
        functools.partial(_expert_kernel, dff=dff),
        grid_spec=pltpu.PrefetchScalarGridSpec(
            num_scalar_prefetch=2,
            grid=(nblk,),
            in_specs=[pl.BlockSpec((MOE_ROWS, dm), xmap),
                      pl.BlockSpec((1, dm, dff2), emap),
                      pl.BlockSpec((1, 1, dff2), emap),
                      pl.BlockSpec((1, dff, dm), emap),
                      pl.BlockSpec((1, 1, dm), emap)],
            out_specs=pl.BlockSpec((MOE_ROWS, dm), lambda i, be, nv: (i, 0)),
            scratch_shapes=[pltpu.VMEM((dm, dff2), BF16), pltpu.VMEM((dff, dm), BF16)]),
        out_shape=jax.ShapeDtypeStruct((rows, dm), F32),
        compiler_params=_cparams("arbitrary"),
        name="experts",
    )(blk_e, n_live, xs, w_gu, b_gu.reshape(n_exp, 1, dff2), w_down, b_down.reshape(n_exp, 1, dm))


def _final_kernel(dest_ref, dnext_ref, route_ref, x1_ref, gt2_ref, gpost2_ref, ys_hbm, out_ref, rows_buf, sem, *, tm):
    nt = pl.num_programs(1)
    step = pl.program_id(0) * nt + pl.program_id(1)
    slot = step % 2

    def row_copy(sl, r, k, d):
        return pltpu.make_async_copy(ys_hbm.at[pl.ds(d, 1), :], rows_buf.at[sl, k, pl.ds(r, 1), :], sem.at[sl])

    def fetch(idx_ref, sl):
        def start(r, c):
            for k in range(TOP_K):
                row_copy(sl, r, k, idx_ref[r * TOP_K + k]).start(priority=k % 2)
            return c
        lax.fori_loop(0, tm, start, 0)

    @pl.when(step == 0)
    def _():
        fetch(dest_ref, 0)

    @pl.when(step + 1 < pl.num_programs(0) * nt)
    def _():
        fetch(dnext_ref, 1 - slot)

    def wait(r, c):
        for k in range(TOP_K):
            row_copy(slot, 0, k, 0).wait()
        return c

    lax.fori_loop(0, tm, wait, 0)
    route = route_ref[0]
    lane = lax.broadcasted_iota(jnp.int32, route.shape, 1)
    f = None
    for k in range(TOP_K):
        gate = jnp.sum(jnp.where(lane == TOP_K + k, route, 0.0), axis=1, keepdims=True)
        term = gate * rows_buf[slot, k]
        f = term if f is None else f + term
    out_ref[0] = x1_ref[0] + _mod_rows(gt2_ref, tm) * _rms(f, gpost2_ref[...])


def _final(dest, route, x1, gt2, mod_map, g_post2, ys, tm):
    nb, s, dm = x1.shape
    nt = s // tm
    last = nb * nt - 1
    row = lambda b, i: (b, i, 0)
    mod_block = (1,) + gt2.shape[1:]
    return pl.pallas_call(
        functools.partial(_final_kernel, tm=tm),
        grid=(nb, nt),
        in_specs=[pl.BlockSpec((tm * TOP_K,), lambda b, i: (b * nt + i,), memory_space=pltpu.SMEM),
                  pl.BlockSpec((tm * TOP_K,), lambda b, i: (jnp.minimum(b * nt + i + 1, last),),
                               memory_space=pltpu.SMEM),
                  pl.BlockSpec((1, tm, LANES), row),
                  pl.BlockSpec((1, tm, dm), row),
                  pl.BlockSpec(mod_block, mod_map),
                  _full_spec(g_post2),
                  pl.BlockSpec(memory_space=pl.ANY)],
        out_specs=pl.BlockSpec((1, tm, dm), row),
        out_shape=jax.ShapeDtypeStruct((nb, s, dm), F32),
        scratch_shapes=[pltpu.VMEM((2, TOP_K, tm, dm), F32), pltpu.SemaphoreType.DMA((2,))],
        compiler_params=_cparams("arbitrary", "arbitrary"),
        name="final",
    )(dest, dest, route, x1, gt2, g_post2, ys)


def _transpose_rows(kv):
    lead = kv.shape[:-4]
    n = len(lead)
    perm = tuple(range(n)) + (n + 1, n + 2, n + 3, n)
    return kv.transpose(perm).reshape(lead + (KV_ROWS, kv.shape[-4]))


def _untranspose_rows(kvt):
    lead = kvt.shape[:-2]
    n = len(lead)
    x = kvt.reshape(lead + (2, N_KV_HEADS, HEAD_DIM, kvt.shape[-1]))
    return x.transpose(tuple(range(n)) + (n + 3, n, n + 1, n + 2))


def _layer(xp, xs, cache_cmp, cache_sel, st_win, st_pool, page_table, cp, cs, w):
    nb, s, dm = xp.shape
    bd, qn, _ = xs.shape
    ts_ = bd * qn
    n_pages = page_table.shape[1]
    assert cache_cmp.shape[1] == LANES, "page size must equal the lane count"
    past = n_pages * LANES
    tm_p = 256
    tm_s = min(256, ts_)
    assert s % tm_p == 0 and ts_ % tm_s == 0 and qn % SUBLANES == 0 and qn <= LANES
    n_exp = w['router_w'].shape[1]
    assert n_exp <= LANES

    c_all = jnp.concatenate([cp, cs], axis=0)
    c_all = jnp.pad(c_all, ((0, -c_all.shape[0] % SUBLANES), (0, 0)))
    mods = _ada(c_all, w['w_ada'], w['b_ada'][None])
    mp = mods[:nb].reshape(nb, 6, 1, dm)
    mod_p = [mp[:, j] for j in range(6)]
    ms = mods[nb:nb + bd].reshape(ts_ // tm_s, tm_s // qn, 6, dm)
    mod_s = [ms[:, :, j] for j in range(6)]
    map_p = lambda b, i: (b, 0, 0)
    map_s = lambda b, i: (i, 0, 0)

    wa, wkvt, pw, aw, order = _prep_in_weights(w['w_in'])
    g_pre1 = w['g_pre1'][None]
    u_p, q_p, mg_p, ng_p, kvc_p, kvs_p, kvw_p = _proj(xp, mod_p[1], mod_p[0], map_p, g_pre1, wa, wkvt, tm_p, pw, aw)
    xs3 = xs.reshape(1, ts_, dm)
    u_s, q_s, mg_s, ng_s, *kv_new = _proj(xs3, mod_s[1], mod_s[0], map_s, g_pre1, wa, wkvt, tm_s, pw, aw)

    cw = _prep_cmp_weights(w['cmp_pe'], w['cmp_w1'], w['cmp_b1'], w['cmp_w2'])
    page_tab = page_table.reshape(-1).astype(jnp.int32)
    ckvt_p = _compress_prompt(kvc_p, cw)
    ckvt_s = _compress_paged(page_tab, n_pages, _transpose_rows(cache_cmp), cw)
    attn_p = _nsa_prompt(q_p, ng_p, ckvt_p, kvs_p, kvw_p, LANES)
    new_t = jnp.concatenate(kv_new, axis=1)[0].reshape(3 * KV_ROWS, bd, qn).transpose(1, 0, 2)
    new_pad = jnp.pad(new_t, ((0, 0), (0, 0), (0, LANES - qn)))
    attn_s = _nsa_sample(page_tab, n_pages, q_s.reshape(bd, qn, aw), ng_s.reshape(bd, qn, LANES), ckvt_s,
                         _transpose_rows(cache_sel), new_pad[:, KV_ROWS:2 * KV_ROWS], _transpose_rows(st_win),
                         new_pad[:, 2 * KV_ROWS:])

    pool_w, pool_scale = w['pool_w'], w['pool_scale'][None]
    hpt = tm_p // POOL_HIST
    py_p = _pool(u_p, lambda b, i: (b, jnp.maximum(i * hpt - 1, 0), 0), u_p, pool_w, pool_scale, tm_p, 0, True)
    p_buf = st_pool.shape[1]
    assert p_buf == max(POOL_WINDOWS) - 1
    u_sb = u_s.reshape(bd, qn, pw)
    hist_s = jnp.pad(st_pool, ((0, 0), (POOL_HIST - p_buf, 0), (0, 0)))
    py_s = _pool(hist_s, lambda b, i: (b, 0, 0), u_sb, pool_w, pool_scale, qn, past, False)

    rw = jnp.pad(w['router_w'], ((0, 0), (0, LANES - n_exp)))
    rwh = rw.astype(BF16)
    rwl = (rw - rwh.astype(F32)).astype(BF16)
    rb = jnp.pad(w['router_b'], (0, LANES - n_exp), constant_values=NEG_BIG)[None]
    wba = w['w_br_attn'].reshape(N_HEADS, HEAD_DIM, dm)[order].reshape(aw, dm).astype(BF16)
    mweights = [w['g_post1'][None], w['g_pre2'][None], w['w_br_pool'].astype(BF16), wba, w['w_o'].astype(BF16),
                rwh, rwl, rb]
    cnt0 = jnp.zeros((1, LANES), F32)
    x1_p, h2_p, route_p, cnt_p = _merge(xp, py_p, attn_p, mg_p, mod_p[2], mod_p[3], mod_p[4], map_p, mweights,
                                        cnt0, tm_p)
    x1_s, h2_s, route_s, cnt_a = _merge(xs3, py_s.reshape(1, ts_, pw), attn_s.reshape(1, ts_, aw), mg_s, mod_s[2],
                                        mod_s[3], mod_s[4], map_s, mweights, cnt_p, tm_s)

    tp_ = nb * s
    route = jnp.concatenate([route_p.reshape(tp_, LANES), route_s.reshape(ts_, LANES)], axis=0)
    eidx = route[:, 0:TOP_K].astype(jnp.int32)
    rank = route[:, 2 * TOP_K:3 * TOP_K].astype(jnp.int32)
    counts = cnt_a[0, :n_exp].astype(jnp.int32)
    padded = (counts + MOE_ROWS - 1) // MOE_ROWS * MOE_ROWS
    pend = jnp.cumsum(padded)
    first = jnp.where(eidx[:, :, None] == jnp.arange(n_exp, dtype=jnp.int32), pend - padded, 0).sum(-1)
    dest = (first + rank).reshape(-1)
    n_blocks = (tp_ + ts_) * TOP_K // MOE_ROWS + n_exp
    blk_start = jnp.arange(n_blocks, dtype=jnp.int32) * MOE_ROWS
    blk_e = jnp.minimum(jnp.sum((pend[None, :] <= blk_start[:, None]).astype(jnp.int32), axis=1), n_exp - 1)
    n_live = (pend[-1:] // MOE_ROWS).astype(jnp.int32)

    clear_rows = jnp.maximum(pend - MOE_ROWS, 0).astype(jnp.int32)
    assert tp_ % tm_s == 0
    xs_rows = _dispatch(clear_rows, n_live, dest, h2_p.reshape(tp_, dm), h2_s.reshape(ts_, dm), n_blocks, tm_s)
    ys = _experts(blk_e, n_live, xs_rows, w['w_gu'], w['b_gu'], w['w_down'], w['b_down'])
    g_post2 = w['g_post2'][None]
    y_p = _final(dest[:tp_ * TOP_K], route_p, x1_p, mod_p[5], map_p, g_post2, ys, tm_p)
    y_s = _final(dest[tp_ * TOP_K:], route_s, x1_s, mod_s[5], map_s, g_post2, ys, tm_s)

    wlen = min(WINDOW, s)
    kv_p = [_untranspose_rows(kvc_p), _untranspose_rows(kvs_p)]
    kvw_p = _untranspose_rows(kvw_p[:, :, s - wlen:])
    pool_p = u_p[:, s - min(POOL_HIST - 1, s):]
    kv_s = [_untranspose_rows(new_t[:, j * KV_ROWS:(j + 1) * KV_ROWS]) for j in range(3)]
    w_buf = st_win.shape[1]
    kvw_s = jnp.concatenate([st_win, kv_s[2]], axis=1)[:, -w_buf:]
    pool_s = jnp.concatenate([st_pool, u_sb], axis=1)[:, -p_buf:]
    return y_p, y_s.reshape(bd, qn, dm), (kv_p[0], kv_p[1], kvw_p, pool_p, kv_s[0], kv_s[1], kvw_s, pool_s)


def kernel(x_prompt, x_sample, cache_kv_cmp, cache_kv_sel, state_kv_win, state_pool, page_table, c_prompt, c_sample,
           w_ada, b_ada, g_pre1, g_post1, g_pre2, g_post2, w_in, pool_w, pool_scale, cmp_pe, cmp_w1, cmp_b1, cmp_w2,
           w_br_pool, w_br_attn, w_o, router_w, router_b, w_gu, b_gu, w_down, b_down):
    names = ('w_ada', 'b_ada', 'g_pre1', 'g_post1', 'g_pre2', 'g_post2', 'w_in', 'pool_w', 'pool_scale', 'cmp_pe',
             'cmp_w1', 'cmp_b1', 'cmp_w2', 'w_br_pool', 'w_br_attn', 'w_o', 'router_w', 'router_b', 'w_gu', 'b_gu',
             'w_down', 'b_down')
    stacked = (w_ada, b_ada, g_pre1, g_post1, g_pre2, g_post2, w_in, pool_w, pool_scale, cmp_pe, cmp_w1, cmp_b1,
               cmp_w2, w_br_pool, w_br_attn, w_o, router_w, router_b, w_gu, b_gu, w_down, b_down)
    xp, xs = x_prompt, x_sample
    states = []
    for l in range(w_ada.shape[0]):
        w = {n: a[l] for n, a in zip(names, stacked)}
        xp, xs, st = _layer(xp, xs, cache_kv_cmp[l], cache_kv_sel[l], state_kv_win[l], state_pool[l], page_table,
                            c_prompt, c_sample, w)
        states.append(st)
    return (xp, xs) + tuple(jnp.stack([st[j] for st in states]) for j in range(8))
```
